```python
import jax, jax.numpy as jnp
from jax import lax
import numpy as np

D_MODEL = 1024
BATCH = 16
SEQ = 2048
DEPTH = 1

HEAD_DIM = 64
ATTN_HEADS = 8
RWKV_HEADS = 8
ATTN_WIDTH = ATTN_HEADS * HEAD_DIM
RWKV_WIDTH = RWKV_HEADS * HEAD_DIM
MIX_WIDTH = ATTN_WIDTH + RWKV_WIDTH
DECAY_LORA = 64
ICLR_LORA = 64
GATE_LORA = 128
RWKV_COLS = 3 * RWKV_WIDTH + DECAY_LORA + ICLR_LORA + GATE_LORA
IN_COLS = 3 * ATTN_WIDTH + RWKV_COLS
DILATED_PATTERNS = ((128, 1), (512, 4), (2048, 16))
ATTN_BLOCK = 128
D_FF = -(-8 * D_MODEL // (3 * 256)) * 256
NORM_EPS = 1e-6
LNX_EPS = 64e-5

kernel_name = 'hymba_dilated_attn_rwkv7_swiglu'


def rms_norm(x, g):
    xf = x.astype(jnp.float32)
    return xf * lax.rsqrt(jnp.mean(xf * xf, axis=-1, keepdims=True) + NORM_EPS) * g.astype(jnp.float32)


def alibi_slopes(n_heads):
    return jnp.exp2(-8.0 * jnp.arange(1, n_heads + 1, dtype=jnp.float32) / n_heads)


def dilated_band(q, k, v, slopes, window, dilation):
    B, S, H, Dh = q.shape
    steps = window // dilation
    bq = ATTN_BLOCK
    assert steps <= bq
    unit = dilation * bq
    s_pad = -(-S // unit) * unit
    n_sub = s_pad // dilation
    n_blk = n_sub // bq

    def to_blocks(t):
        t = jnp.pad(t, ((0, 0), (0, s_pad - S), (0, 0), (0, 0)))
        t = t.reshape(B, n_sub, dilation, H, Dh).transpose(0, 2, 3, 1, 4)
        return t.reshape(B, dilation, H, n_blk, bq, Dh)

    def with_prev(t):
        prev = jnp.pad(t, ((0, 0), (0, 0), (0, 0), (1, 0), (0, 0), (0, 0)))[:, :, :, :-1]
        return jnp.concatenate([prev, t], axis=4)

    qb = to_blocks(q)
    kw = with_prev(to_blocks(k))
    vw = with_prev(to_blocks(v))
    scores = jnp.einsum('bdhnqe,bdhnke->bdhnqk', qb, kw)
    q_idx = jnp.arange(bq)[:, None] + bq
    k_idx = jnp.arange(2 * bq)[None, :]
    diff = q_idx - k_idx
    key_abs = jnp.arange(n_blk)[:, None, None] * bq + k_idx[None] - bq
    valid = (diff >= 0) & (diff <= steps) & (key_abs >= 0)
    bias = -slopes[:, None, None, None] * (diff.astype(jnp.float32) * dilation)
    scores = jnp.where(valid, scores + bias, -jnp.inf)
    m = jnp.max(scores, axis=-1)
    p = jnp.exp(scores - m[..., None])
    den = jnp.sum(p, axis=-1)
    num = jnp.einsum('bdhnqk,bdhnke->bdhnqe', p, vw)

    def from_blocks(t):
        tail = t.shape[5:]
        t = t.reshape((B, dilation, H, n_sub) + tail)
        t = jnp.moveaxis(t, 3, 1)
        return t.reshape((B, s_pad, H) + tail)[:, :S]

    return from_blocks(num), from_blocks(m), from_blocks(den)


def dilated_attention(q, k, v, q_norm_g, k_norm_g):
    B, S, _ = q.shape
    shp = (B, S, ATTN_HEADS, HEAD_DIM)
    q = rms_norm(q.reshape(shp), q_norm_g) * HEAD_DIM ** -0.5
    k = rms_norm(k.reshape(shp), k_norm_g)
    v = v.reshape(shp).astype(jnp.float32)
    slopes = alibi_slopes(ATTN_HEADS)
    nums, maxs, dens = [], [], []
    for window, dilation in DILATED_PATTERNS:
        num, m, den = dilated_band(q, k, v, slopes, window, dilation)
        nums.append(num); maxs.append(m); dens.append(den)
    maxs = jnp.stack(maxs)
    scale = jnp.exp(maxs - jnp.max(maxs, axis=0))
    total_num = jnp.sum(scale[..., None] * jnp.stack(nums), axis=0)
    total_den = jnp.sum(scale * jnp.stack(dens), axis=0)
    return total_num / total_den[..., None]


def wkv7_scan(r, w, k, v, a, b):
    B, S, H, Dh = r.shape

    def step(state, inp):
        r_t, w_t, k_t, v_t, a_t, b_t = inp
        sa = jnp.einsum('bhij,bhj->bhi', state, a_t)
        state = (state * w_t[:, :, None, :] + sa[..., None] * b_t[:, :, None, :]
                 + v_t[..., None] * k_t[:, :, None, :])
        return state, jnp.einsum('bhij,bhj->bhi', state, r_t)

    xs = tuple(jnp.moveaxis(t, 1, 0) for t in (r, w, k, v, a, b))
    state0 = jnp.zeros((B, H, Dh, Dh), jnp.float32)
    _, y = lax.scan(step, state0, xs)
    return jnp.moveaxis(y, 0, 1)


def rwkv7_time_mix(p_rw, mu, w0, w2, a0, a2, g2, k_k, k_a, r_k, lnx_g, lnx_b):
    B, S, _ = p_rw.shape
    p = p_rw.astype(jnp.float32)
    prev = jnp.pad(p, ((0, 0), (1, 0), (0, 0)))[:, :-1]
    xs = p + (prev - p) * mu
    cuts = [RWKV_WIDTH, 2 * RWKV_WIDTH, 3 * RWKV_WIDTH,
            3 * RWKV_WIDTH + DECAY_LORA, 3 * RWKV_WIDTH + DECAY_LORA + ICLR_LORA]
    r, k, v, wl, al, gl = jnp.split(xs, cuts, axis=-1)
    w = -jax.nn.softplus(-(w0 + jnp.tanh(wl) @ w2)) - 0.5
    decay = jnp.exp(-jnp.exp(w))
    a = jax.nn.sigmoid(a0 + al @ a2)
    g = jax.nn.sigmoid(gl) @ g2

    def heads(t):
        return t.reshape(B, S, RWKV_HEADS, HEAD_DIM)

    kk = heads(k * k_k)
    kk = kk / jnp.maximum(jnp.sqrt(jnp.sum(kk * kk, axis=-1, keepdims=True)), 1e-12)
    k = k * (1.0 + (a - 1.0) * k_a)
    r_h, k_h, v_h, w_h, a_h = heads(r), heads(k), heads(v), heads(decay), heads(a)
    y = wkv7_scan(r_h, w_h, k_h, v_h, -kk, kk * a_h)
    mean = jnp.mean(y, axis=-1, keepdims=True)
    var = jnp.mean(jnp.square(y - mean), axis=-1, keepdims=True)
    y = ((y - mean) * lax.rsqrt(var + LNX_EPS)).reshape(B, S, RWKV_WIDTH) * lnx_g + lnx_b
    bonus = jnp.sum(r_h * k_h * r_k, axis=-1, keepdims=True) * v_h
    return (y + bonus.reshape(B, S, RWKV_WIDTH)) * g


def hybrid_layer(x, norm1_g, w_in, q_norm_g, k_norm_g, attn_out_g, rwkv_mu, w0, w2, a0, a2, g2,
                 k_k, k_a, r_k, lnx_g, lnx_b, w_out, norm2_g, w_gate, w_up, w_down):
    B, S, _ = x.shape
    xn = rms_norm(x, norm1_g).astype(x.dtype)
    proj = xn @ w_in
    q, k, v, p_rw = jnp.split(proj, [ATTN_WIDTH, 2 * ATTN_WIDTH, 3 * ATTN_WIDTH], axis=-1)
    attn = dilated_attention(q, k, v, q_norm_g, k_norm_g)
    attn = rms_norm(attn, attn_out_g.reshape(ATTN_HEADS, HEAD_DIM)).reshape(B, S, ATTN_WIDTH)
    rwkv = rwkv7_time_mix(p_rw, rwkv_mu, w0, w2, a0, a2, g2, k_k, k_a, r_k, lnx_g, lnx_b)
    mixed = jnp.concatenate([attn, rwkv], axis=-1).astype(x.dtype)
    x = x + mixed @ w_out
    xn2 = rms_norm(x, norm2_g).astype(x.dtype)
    ffn = (jax.nn.silu(xn2 @ w_gate) * (xn2 @ w_up)) @ w_down
    return x + ffn


def setup_inputs(seed: int = 0) -> dict:
    key = jax.random.key(seed)
    ks = jax.random.split(key, 24)
    L = DEPTH
    f32 = jnp.float32

    def nrm(k, shape, scale):
        return jax.random.normal(k, shape, f32) * scale

    return {
        'x': nrm(ks[0], (BATCH, SEQ, D_MODEL), 1.0),
        'norm1_g': 1.0 + nrm(ks[1], (L, D_MODEL), 0.02),
        'w_in': nrm(ks[2], (L, D_MODEL, IN_COLS), D_MODEL ** -0.5),
        'q_norm_g': 1.0 + nrm(ks[3], (L, HEAD_DIM), 0.02),
        'k_norm_g': 1.0 + nrm(ks[4], (L, HEAD_DIM), 0.02),
        'attn_out_g': 1.0 + nrm(ks[5], (L, ATTN_WIDTH), 0.02),
        'rwkv_mu': jax.random.uniform(ks[6], (L, RWKV_COLS), f32),
        'w0': jax.random.uniform(ks[7], (L, RWKV_WIDTH), f32, minval=-6.0, maxval=1.0),
        'w2': nrm(ks[8], (L, DECAY_LORA, RWKV_WIDTH), 0.5 * DECAY_LORA ** -0.5),
        'a0': nrm(ks[9], (L, RWKV_WIDTH), 0.1),
        'a2': nrm(ks[10], (L, ICLR_LORA, RWKV_WIDTH), ICLR_LORA ** -0.5),
        'g2': nrm(ks[11], (L, GATE_LORA, RWKV_WIDTH), GATE_LORA ** -0.5),
        'k_k': 0.85 + nrm(ks[12], (L, RWKV_WIDTH), 0.02),
        'k_a': 1.0 + nrm(ks[13], (L, RWKV_WIDTH), 0.02),
        'r_k': nrm(ks[14], (L, RWKV_HEADS, HEAD_DIM), 0.1),
        'lnx_g': 1.0 + nrm(ks[15], (L, RWKV_WIDTH), 0.02),
        'lnx_b': nrm(ks[16], (L, RWKV_WIDTH), 0.01),
        'w_out': nrm(ks[17], (L, MIX_WIDTH, D_MODEL), MIX_WIDTH ** -0.5),
        'norm2_g': 1.0 + nrm(ks[18], (L, D_MODEL), 0.02),
        'w_gate': nrm(ks[19], (L, D_MODEL, D_FF), D_MODEL ** -0.5),
        'w_up': nrm(ks[20], (L, D_MODEL, D_FF), D_MODEL ** -0.5),
        'w_down': nrm(ks[21], (L, D_FF, D_MODEL), D_FF ** -0.5),
    }


def reference(x, norm1_g, w_in, q_norm_g, k_norm_g, attn_out_g, rwkv_mu, w0, w2, a0, a2, g2,
              k_k, k_a, r_k, lnx_g, lnx_b, w_out, norm2_g, w_gate, w_up, w_down):
    h = x
    for layer in range(DEPTH):
        h = hybrid_layer(h, norm1_g[layer], w_in[layer], q_norm_g[layer], k_norm_g[layer],
                         attn_out_g[layer], rwkv_mu[layer], w0[layer], w2[layer], a0[layer],
                         a2[layer], g2[layer], k_k[layer], k_a[layer], r_k[layer], lnx_g[layer],
                         lnx_b[layer], w_out[layer], norm2_g[layer], w_gate[layer], w_up[layer],
                         w_down[layer])
    return h
```

```python
import functools

import jax
import jax.numpy as jnp
from jax import lax
from jax.experimental import pallas as pl
from jax.experimental.pallas import tpu as pltpu

F32 = jnp.float32
BF16 = jnp.bfloat16

HEAD_DIM = 64
LANES = 128
ATTN_HEADS = 8
RWKV_HEADS = 8
ATTN_WIDTH = ATTN_HEADS * HEAD_DIM
RWKV_WIDTH = RWKV_HEADS * HEAD_DIM
DECAY_LORA = 64
ICLR_LORA = 64
GATE_LORA = 128
DILATED_PATTERNS = ((128, 1), (512, 4), (2048, 16))
BLK = 128
NORM_EPS = 1e-6
LNX_EPS = 64e-5
MASKED = -1e30
VMEM_LIMIT = 56 * 1024 * 1024

_QB, _KB, _VB = 0, ATTN_WIDTH // LANES, 2 * ATTN_WIDTH // LANES
_RW0 = 3 * ATTN_WIDTH // LANES
_N_PAIR = RWKV_WIDTH // LANES


def _nt_dot(a, b):
    return lax.dot_general(a, b, (((1,), (1,)), ((), ())), preferred_element_type=F32)


def _tn_dot(a, b):
    return lax.dot_general(a, b, (((0,), (0,)), ((), ())), preferred_element_type=F32)


def _dot(a, b):
    return jnp.dot(a, b, preferred_element_type=F32)


def _head0_lanes():
    return lax.broadcasted_iota(jnp.int32, (1, LANES), 1) < HEAD_DIM


def _head_sum(x, head0):
    s0 = jnp.sum(jnp.where(head0, x, 0.0), axis=-1, keepdims=True)
    s1 = jnp.sum(jnp.where(head0, 0.0, x), axis=-1, keepdims=True)
    return jnp.where(head0, s0, s1)


def _stack_heads(x, head0):
    return jnp.concatenate([jnp.where(head0, x, 0.0), jnp.where(head0, 0.0, x)], axis=0)


def _in_proj_kernel(x_ref, g_ref, w_ref, o_ref):
    x = x_ref[...]
    ms = jnp.mean(x * x, axis=-1, keepdims=True)
    xn = x * lax.rsqrt(ms + NORM_EPS) * g_ref[...]
    o_ref[...] = _dot(xn.astype(BF16), w_ref[...])


def _in_proj(x2d, g, w_bf16, tm):
    t, d = x2d.shape
    n = w_bf16.shape[1]
    return pl.pallas_call(
        _in_proj_kernel,
        grid=(t // tm,),
        in_specs=[
            pl.BlockSpec((tm, d), lambda i: (i, 0)),
            pl.BlockSpec((1, d), lambda i: (0, 0)),
            pl.BlockSpec((d, n), lambda i: (0, 0), pipeline_mode=pl.Buffered(1)),
        ],
        out_specs=pl.BlockSpec((tm, n), lambda i: (i, 0)),
        out_shape=jax.ShapeDtypeStruct((t, n), F32),
        compiler_params=pltpu.CompilerParams(
            dimension_semantics=("parallel",), vmem_limit_bytes=VMEM_LIMIT),
        name="in_proj",
    )(x2d, g, w_bf16)


def _attn_kernel(slope_ref, q_ref, k_ref, v_ref, gq_ref, gk_ref, go_ref, o_ref,
                 qn_ref, kn_ref, m_ref, l_ref, acc_ref, bias_ref):
    seq = q_ref.shape[1]
    hp = pl.program_id(1)
    head0 = _head0_lanes()
    n_rows = seq // BLK

    def prep(i, carry):
        rows = pl.ds(pl.multiple_of(i * BLK, BLK), BLK)
        q = q_ref[0, rows, :]
        k = k_ref[0, rows, :]
        qn_ref[rows, :] = (q * lax.rsqrt(_head_sum(q * q, head0) * (1.0 / HEAD_DIM) + NORM_EPS)
                           * gq_ref[...] * (HEAD_DIM ** -0.5))
        kn_ref[rows, :] = (k * lax.rsqrt(_head_sum(k * k, head0) * (1.0 / HEAD_DIM) + NORM_EPS)
                           * gk_ref[...])
        m_ref[rows, :] = jnp.full((BLK, LANES), MASKED, F32)
        l_ref[rows, :] = jnp.zeros((BLK, LANES), F32)
        acc_ref[rows, :] = jnp.zeros((BLK, LANES), F32)
        return carry

    lax.fori_loop(0, n_rows, prep, 0)

    qi = lax.broadcasted_iota(jnp.int32, (2 * BLK, 2 * BLK), 0)
    ci = lax.broadcasted_iota(jnp.int32, (2 * BLK, 2 * BLK), 1)
    back = jnp.where(qi < BLK, qi, qi - BLK) + BLK - ci
    slope = jnp.where(qi < BLK, slope_ref[2 * hp], slope_ref[2 * hp + 1])
    for p, (window, dil) in enumerate(DILATED_PATTERNS):
        steps = window // dil
        valid = (back >= 0) & (back <= steps)
        bias_ref[p] = jnp.where(valid, -slope * (back.astype(F32) * dil), MASKED)

    def band_block(p, dil, start, has_prev):
        rows = pl.ds(start, BLK, stride=dil) if dil > 1 else pl.ds(start, BLK)
        qb = qn_ref[rows, :]
        kb = kn_ref[rows, :]
        vb = v_ref[0, rows, :]
        if has_prev:
            prev_start = start - BLK * dil
            prows = (pl.ds(prev_start, BLK, stride=dil) if dil > 1 else pl.ds(prev_start, BLK))
            kb = jnp.concatenate([kn_ref[prows, :], kb], axis=0)
            vb = jnp.concatenate([v_ref[0, prows, :], vb], axis=0)
            bias = bias_ref[p]
        else:
            bias = bias_ref[p, :, BLK:]
        s = _nt_dot(_stack_heads(qb, head0).astype(BF16), kb.astype(BF16)) + bias
        m_blk = jnp.max(s, axis=-1, keepdims=True)
        pr = jnp.exp(s - m_blk)
        l_blk = jnp.sum(pr, axis=-1, keepdims=True)
        pv = _dot(pr.astype(BF16), vb.astype(BF16))
        m_pair = jnp.where(head0, m_blk[:BLK], m_blk[BLK:])
        l_pair = jnp.where(head0, l_blk[:BLK], l_blk[BLK:])
        pv_pair = jnp.where(head0, pv[:BLK], pv[BLK:])
        m_old = m_ref[rows, :]
        m_new = jnp.maximum(m_old, m_pair)
        a_old = jnp.exp(m_old - m_new)
        a_blk = jnp.exp(m_pair - m_new)
        m_ref[rows, :] = m_new
        l_ref[rows, :] = a_old * l_ref[rows, :] + a_blk * l_pair
        acc_ref[rows, :] = a_old * acc_ref[rows, :] + a_blk * pv_pair

    for p, (window, dil) in enumerate(DILATED_PATTERNS):
        n_blk = seq // (BLK * dil)

        def residue(r, carry, p=p, dil=dil, n_blk=n_blk):
            band_block(p, dil, r, False)
            if n_blk > 1:
                def later(n, c2):
                    band_block(p, dil, r + n * (BLK * dil), True)
                    return c2
                lax.fori_loop(1, n_blk, later, 0)
            return carry

        if dil == 1:
            residue(0, 0)
        else:
            lax.fori_loop(0, dil, residue, 0)

    def finish(i, carry):
        rows = pl.ds(pl.multiple_of(i * BLK, BLK), BLK)
        o = acc_ref[rows, :] / l_ref[rows, :]
        ms = _head_sum(o * o, head0) * (1.0 / HEAD_DIM)
        o_ref[0, rows, :] = o * lax.rsqrt(ms + NORM_EPS) * go_ref[...]
        return carry

    lax.fori_loop(0, n_rows, finish, 0)


def _attention(proj, slopes, gq, gk, go):
    b, s, _ = proj.shape
    blk = lambda off: pl.BlockSpec((1, s, LANES), lambda i, j: (i, 0, off + j))
    vec = lambda: pl.BlockSpec((1, LANES), lambda i, j: (0, 0))
    return pl.pallas_call(
        _attn_kernel,
        grid=(b, _N_PAIR),
        in_specs=[
            pl.BlockSpec(memory_space=pltpu.SMEM),
            blk(_QB), blk(_KB), blk(_VB),
            vec(), vec(),
            pl.BlockSpec((1, LANES), lambda i, j: (0, j)),
        ],
        out_specs=pl.BlockSpec((1, s, LANES), lambda i, j: (i, 0, j)),
        out_shape=jax.ShapeDtypeStruct((b, s, ATTN_WIDTH), F32),
        scratch_shapes=[
            pltpu.VMEM((s, LANES), F32),
            pltpu.VMEM((s, LANES), F32),
            pltpu.VMEM((s, LANES), F32),
            pltpu.VMEM((s, LANES), F32),
            pltpu.VMEM((s, LANES), F32),
            pltpu.VMEM((len(DILATED_PATTERNS), 2 * BLK, 2 * BLK), F32),
        ],
        compiler_params=pltpu.CompilerParams(
            dimension_semantics=("parallel", "parallel"), vmem_limit_bytes=VMEM_LIMIT),
        name="dilated_attn",
    )(slopes, proj, proj, proj, gq, gk, go)


def _wkv_kernel(pr_ref, pk_ref, pv_ref, pwa_ref, pg_ref,
                mur_ref, muk_ref, muv_ref, muwa_ref, mug_ref,
                w0_ref, a0_ref, kk_ref, ka_ref, rk_ref, lng_ref, lnb_ref,
                wwa_ref, g2_ref, o_ref, h_ref):
    seq = pr_ref.shape[1]
    n_chunks = seq // BLK
    head0 = _head0_lanes()
    row = lax.broadcasted_iota(jnp.int32, (BLK, BLK), 0)
    col = lax.broadcasted_iota(jnp.int32, (BLK, BLK), 1)
    strict = col < row
    incl = col <= row
    tril_incl = jnp.where(incl, 1.0, 0.0).astype(BF16)
    first_row = lax.broadcasted_iota(jnp.int32, (BLK, LANES), 0) == 0
    srow = lax.broadcasted_iota(jnp.int32, (2 * BLK, LANES), 0) < BLK
    slane = lax.broadcasted_iota(jnp.int32, (2 * BLK, LANES), 1) < HEAD_DIM
    own_head = srow == slane
    zeros_blk = jnp.zeros((BLK, BLK), F32)

    h_ref[...] = jnp.zeros((LANES, LANES), F32)

    def chunk(c, carry):
        row0 = pl.multiple_of(c * BLK, BLK)
        rows = pl.ds(row0, BLK)
        prev8 = pl.ds(pl.multiple_of(jnp.maximum(row0 - 8, 0), 8), 8)
        has_prev = (c > 0).astype(F32)

        def shifted(ref, mu_ref):
            p = ref[0, rows, :]
            last = ref[0, prev8, :][7:8, :] * has_prev
            prev = jnp.where(first_row, last, pltpu.roll(p, 1, 0))
            return p + (prev - p) * mu_ref[...]

        xr = shifted(pr_ref, mur_ref)
        xk = shifted(pk_ref, muk_ref)
        xv = shifted(pv_ref, muv_ref)
        xwa = shifted(pwa_ref, muwa_ref)
        xg = shifted(pg_ref, mug_ref)

        lora = _dot(jnp.where(head0, jnp.tanh(xwa), xwa).astype(BF16), wwa_ref[0])
        zw = -(w0_ref[...] + lora[:, :LANES])
        softplus = jnp.maximum(zw, 0.0) + jnp.log(1.0 + jnp.exp(-jnp.abs(zw)))
        log_decay = -jnp.exp(-softplus - 0.5)
        a_sig = jax.nn.sigmoid(a0_ref[...] + lora[:, LANES:])
        gate = _dot(jax.nn.sigmoid(xg).astype(BF16), g2_ref[...])

        kkv = xk * kk_ref[...]
        kk = kkv / jnp.maximum(jnp.sqrt(_head_sum(kkv * kkv, head0)), 1e-12)
        k2 = xk * (1.0 + (a_sig - 1.0) * ka_ref[...])
        a_vec = -kk
        b_vec = kk * a_sig

        ld_hi = log_decay.astype(BF16)
        rem = log_decay - ld_hi.astype(F32)
        ld_mid = rem.astype(BF16)
        ld_lo = (rem - ld_mid.astype(F32)).astype(BF16)
        cum = _dot(tril_incl, jnp.concatenate([ld_hi, ld_mid, ld_lo], axis=1))
        lw = cum[:, :LANES] + cum[:, LANES:2 * LANES] + cum[:, 2 * LANES:]
        lw_last = lw[BLK - 1:BLK, :]
        w_in = jnp.exp(lw)
        w_ex = jnp.exp(lw - log_decay)
        w_inv = jnp.exp(-lw)
        w_tail = jnp.exp(lw_last - lw)
        w_all = jnp.exp(lw_last)

        at = a_vec * w_ex
        rt = xr * w_in
        bt = b_vec * w_inv
        kt = k2 * w_inv
        bh = b_vec * w_tail
        kh = k2 * w_tail

        lhs = jnp.concatenate([jnp.where(head0, at, 0.0), jnp.where(head0, rt, 0.0),
                               jnp.where(head0, 0.0, at), jnp.where(head0, 0.0, rt)], axis=0)
        rhs = jnp.concatenate([bt, kt], axis=0)
        pp = _nt_dot(lhs.astype(BF16), rhs.astype(BF16))
        aab0 = jnp.where(strict, pp[0:BLK, :BLK], 0.0)
        aak0 = jnp.where(strict, pp[0:BLK, BLK:], 0.0)
        arb0 = jnp.where(incl, pp[BLK:2 * BLK, :BLK], 0.0)
        ark0 = jnp.where(incl, pp[BLK:2 * BLK, BLK:], 0.0)
        aab1 = jnp.where(strict, pp[2 * BLK:3 * BLK, :BLK], 0.0)
        aak1 = jnp.where(strict, pp[2 * BLK:3 * BLK, BLK:], 0.0)
        arb1 = jnp.where(incl, pp[3 * BLK:, :BLK], 0.0)
        ark1 = jnp.where(incl, pp[3 * BLK:, BLK:], 0.0)

        nmat = jnp.concatenate([jnp.concatenate([aab0, zeros_blk], axis=1),
                                jnp.concatenate([zeros_blk, aab1], axis=1)], axis=0)
        akv = _dot(jnp.concatenate([aak0, aak1], axis=0).astype(BF16), xv.astype(BF16))
        x = jnp.concatenate([_stack_heads(at, head0), jnp.where(own_head, akv, 0.0)], axis=1)
        n_bf = nmat.astype(BF16)
        span = 1
        while span < BLK:
            x = x + _dot(n_bf, x.astype(BF16))
            span *= 2
            if span < BLK:
                n_bf = _dot(n_bf, n_bf).astype(BF16)
        x_bf = x.astype(BF16)

        v_s = _stack_heads(xv, head0).astype(BF16)
        z = _dot(jnp.concatenate([arb0, arb1], axis=1).astype(BF16), x_bf)
        r_eff = rt + z[:, :LANES]
        y0 = z[:, LANES:] + _dot(jnp.concatenate([ark0, ark1], axis=1).astype(BF16), v_s)
        d1 = _tn_dot(_stack_heads(bh, head0).astype(BF16), x_bf)
        d2 = _tn_dot(_stack_heads(kh, head0).astype(BF16), v_s)
        m_eff = jnp.where(row == col, w_all, 0.0) + d1[:, :LANES]
        g_eff = d1[:, LANES:] + d2

        rm = _dot(jnp.concatenate([r_eff, m_eff], axis=0).astype(BF16), h_ref[...].astype(BF16))
        y = rm[:BLK] + y0
        h_ref[...] = rm[BLK:] + g_eff

        mean = _head_sum(y, head0) * (1.0 / HEAD_DIM)
        yc = y - mean
        var = _head_sum(yc * yc, head0) * (1.0 / HEAD_DIM)
        yn = yc * lax.rsqrt(var + LNX_EPS) * lng_ref[...] + lnb_ref[...]
        bonus = _head_sum(xr * k2 * rk_ref[...], head0) * xv
        o_ref[0, rows, :] = (yn + bonus) * gate
        return carry

    lax.fori_loop(0, n_chunks, chunk, 0)


def _rwkv(proj, mu, w0, a0, k_k, k_a, r_k, lnx_g, lnx_b, wwa, g2_bf16):
    b, s, _ = proj.shape
    blk = lambda off: pl.BlockSpec((1, s, LANES), lambda i, j: (i, 0, off + j))
    fixed = lambda off: pl.BlockSpec((1, s, LANES), lambda i, j: (i, 0, off))
    vec = lambda off: pl.BlockSpec((1, LANES), lambda i, j: (0, off + j))
    vfix = lambda off: pl.BlockSpec((1, LANES), lambda i, j: (0, off))
    np_ = _N_PAIR
    return pl.pallas_call(
        _wkv_kernel,
        grid=(b, np_),
        in_specs=[
            blk(_RW0), blk(_RW0 + np_), blk(_RW0 + 2 * np_),
            fixed(_RW0 + 3 * np_), fixed(_RW0 + 3 * np_ + 1),
            vec(0), vec(np_), vec(2 * np_), vfix(3 * np_), vfix(3 * np_ + 1),
            vec(0), vec(0), vec(0), vec(0), vec(0), vec(0), vec(0),
            pl.BlockSpec((1, LANES, 2 * LANES), lambda i, j: (j, 0, 0)),
            pl.BlockSpec((GATE_LORA, LANES), lambda i, j: (0, j)),
        ],
        out_specs=pl.BlockSpec((1, s, LANES), lambda i, j: (i, 0, j)),
        out_shape=jax.ShapeDtypeStruct((b, s, RWKV_WIDTH), F32),
        scratch_shapes=[pltpu.VMEM((LANES, LANES), F32)],
        compiler_params=pltpu.CompilerParams(
            dimension_semantics=("parallel", "parallel"), vmem_limit_bytes=VMEM_LIMIT),
        name="rwkv7",
    )(proj, proj, proj, proj, proj, mu, mu, mu, mu, mu,
      w0, a0, k_k, k_a, r_k, lnx_g, lnx_b, wwa, g2_bf16)


def _out_ffn_kernel(x_ref, attn_ref, rwkv_ref, woa_ref, wor_ref, g2_ref,
                    wg_ref, wu_ref, wd_ref, o_ref):
    x1 = (x_ref[...] + _dot(attn_ref[...].astype(BF16), woa_ref[...])
          + _dot(rwkv_ref[...].astype(BF16), wor_ref[...]))
    ms = jnp.mean(x1 * x1, axis=-1, keepdims=True)
    xn = (x1 * lax.rsqrt(ms + NORM_EPS) * g2_ref[...]).astype(BF16)
    gate = _dot(xn, wg_ref[...])
    up = _dot(xn, wu_ref[...])
    hidden = (gate * jax.nn.sigmoid(gate) * up).astype(BF16)
    o_ref[...] = x1 + _dot(hidden, wd_ref[...])


def _out_ffn(x2d, attn2d, rwkv2d, woa, wor, g2, wg, wu, wd, tm):
    t, d = x2d.shape
    f = wg.shape[1]
    tok = lambda w: pl.BlockSpec((tm, w), lambda i: (i, 0))
    res = lambda shape: pl.BlockSpec(shape, lambda i: (0, 0), pipeline_mode=pl.Buffered(1))
    return pl.pallas_call(
        _out_ffn_kernel,
        grid=(t // tm,),
        in_specs=[
            tok(d), tok(ATTN_WIDTH), tok(RWKV_WIDTH),
            res((ATTN_WIDTH, d)), res((RWKV_WIDTH, d)),
            pl.BlockSpec((1, d), lambda i: (0, 0)),
            res((d, f)), res((d, f)), res((f, d)),
        ],
        out_specs=tok(d),
        out_shape=jax.ShapeDtypeStruct((t, d), F32),
        compiler_params=pltpu.CompilerParams(
            dimension_semantics=("parallel",), vmem_limit_bytes=VMEM_LIMIT),
        name="out_ffn",
    )(x2d, attn2d, rwkv2d, woa, wor, g2, wg, wu, wd)


def _layer(x, norm1_g, w_in, q_norm_g, k_norm_g, attn_out_g, rwkv_mu, w0, w2, a0, a2, g2,
           k_k, k_a, r_k, lnx_g, lnx_b, w_out, norm2_g, w_gate, w_up, w_down):
    b, s, d = x.shape
    assert s % (BLK * max(dil for _, dil in DILATED_PATTERNS)) == 0
    assert all(window // dil == BLK for window, dil in DILATED_PATTERNS)
    t = b * s
    tm = 512
    assert t % tm == 0
    row = lambda v: v.reshape(1, -1).astype(F32)

    x2d = x.reshape(t, d)
    proj = _in_proj(x2d, row(norm1_g), w_in.astype(BF16), tm).reshape(b, s, -1)

    slopes = jnp.exp2(-8.0 * jnp.arange(1, ATTN_HEADS + 1, dtype=F32) / ATTN_HEADS)
    pair = lambda v: jnp.tile(v.reshape(1, HEAD_DIM), (1, LANES // HEAD_DIM)).astype(F32)
    attn = _attention(proj, slopes, pair(q_norm_g), pair(k_norm_g), row(attn_out_g))

    w2p = w2.reshape(DECAY_LORA, _N_PAIR, LANES).transpose(1, 0, 2)
    a2p = a2.reshape(ICLR_LORA, _N_PAIR, LANES).transpose(1, 0, 2)
    zero = jnp.zeros_like(w2p)
    wwa = jnp.concatenate([jnp.concatenate([w2p, zero], axis=2),
                           jnp.concatenate([jnp.zeros_like(a2p), a2p], axis=2)], axis=1).astype(BF16)
    rwkv = _rwkv(proj, row(rwkv_mu), row(w0), row(a0), row(k_k), row(k_a), row(r_k),
                 row(lnx_g), row(lnx_b), wwa, g2.astype(BF16))

    out = _out_ffn(x2d, attn.reshape(t, ATTN_WIDTH), rwkv.reshape(t, RWKV_WIDTH),
                   w_out[:ATTN_WIDTH].astype(BF16), w_out[ATTN_WIDTH:].astype(BF16),
                   row(norm2_g), w_gate.astype(BF16), w_up.astype(BF16), w_down.astype(BF16), tm)
    return out.reshape(b, s, d)


def kernel(x, norm1_g, w_in, q_norm_g, k_norm_g, attn_out_g, rwkv_mu, w0, w2, a0, a2, g2,
           k_k, k_a, r_k, lnx_g, lnx_b, w_out, norm2_g, w_gate, w_up, w_down):
    h = x
    for layer in range(norm1_g.shape[0]):
        h = _layer(h, norm1_g[layer], w_in[layer], q_norm_g[layer], k_norm_g[layer],
                   attn_out_g[layer], rwkv_mu[layer], w0[layer], w2[layer], a0[layer],
                   a2[layer], g2[layer], k_k[layer], k_a[layer], r_k[layer], lnx_g[layer],
                   lnx_b[layer], w_out[layer], norm2_g[layer], w_gate[layer], w_up[layer],
                   w_down[layer])
    return h
```

```python
import functools

import jax
import jax.numpy as jnp
from jax import lax
from jax.experimental import pallas as pl
from jax.experimental.pallas import tpu as pltpu

F32 = jnp.float32
BF16 = jnp.bfloat16

HEAD_DIM = 64
LANES = 128
ATTN_HEADS = 8
RWKV_HEADS = 8
ATTN_WIDTH = ATTN_HEADS * HEAD_DIM
RWKV_WIDTH = RWKV_HEADS * HEAD_DIM
DECAY_LORA = 64
ICLR_LORA = 64
GATE_LORA = 128
DILATED_PATTERNS = ((128, 1), (512, 4), (2048, 16))
BLK = 128
WKV_GROUP = 4
ATTN_GROUP = 4
ROW_TILE = 512
NORM_EPS = 1e-6
LNX_EPS = 64e-5
MASKED = -1e30
VMEM_LIMIT = 56 * 1024 * 1024

_QB, _KB, _VB = 0, ATTN_WIDTH // LANES, 2 * ATTN_WIDTH // LANES
_RW0 = 3 * ATTN_WIDTH // LANES
_N_PAIR = RWKV_WIDTH // LANES


def _nt_dot(a, b):
    return lax.dot_general(a, b, (((1,), (1,)), ((), ())), preferred_element_type=F32)


def _tn_dot(a, b):
    return lax.dot_general(a, b, (((0,), (0,)), ((), ())), preferred_element_type=F32)


def _dot(a, b):
    return jnp.dot(a, b, preferred_element_type=F32)


def _head0_lanes():
    return lax.broadcasted_iota(jnp.int32, (1, LANES), 1) < HEAD_DIM


def _head_ones():
    i = lax.broadcasted_iota(jnp.int32, (2 * LANES, LANES), 0)
    j = lax.broadcasted_iota(jnp.int32, (2 * LANES, LANES), 1)
    same = ((i % LANES) < HEAD_DIM) == (j < HEAD_DIM)
    return jnp.where(same, 1.0, 0.0).astype(BF16)


def _head_sum_mxu(x, head_ones):
    hi = x.astype(BF16)
    lo = (x - hi.astype(F32)).astype(BF16)
    return _dot(jnp.concatenate([hi, lo], axis=1), head_ones)


def _head_sum(x, head0):
    s0 = jnp.sum(jnp.where(head0, x, 0.0), axis=-1, keepdims=True)
    s1 = jnp.sum(jnp.where(head0, 0.0, x), axis=-1, keepdims=True)
    return jnp.where(head0, s0, s1)


def _stack_heads(x, head0):
    return jnp.concatenate([jnp.where(head0, x, 0.0), jnp.where(head0, 0.0, x)], axis=0)


def _in_proj_kernel(x_ref, g_ref, w_ref, o_ref):
    x = x_ref[...]
    ms = jnp.mean(x * x, axis=-1, keepdims=True)
    xn = x * lax.rsqrt(ms + NORM_EPS) * g_ref[...]
    o_ref[...] = _dot(xn.astype(BF16), w_ref[...])


def _in_proj(x2d, g, w_bf16, tm):
    t, d = x2d.shape
    n = w_bf16.shape[1]
    return pl.pallas_call(
        _in_proj_kernel,
        grid=(t // tm,),
        in_specs=[
            pl.BlockSpec((tm, d), lambda i: (i, 0)),
            pl.BlockSpec((1, d), lambda i: (0, 0)),
            pl.BlockSpec((d, n), lambda i: (0, 0), pipeline_mode=pl.Buffered(1)),
        ],
        out_specs=pl.BlockSpec((tm, n), lambda i: (i, 0)),
        out_shape=jax.ShapeDtypeStruct((t, n), F32),
        compiler_params=pltpu.CompilerParams(
            dimension_semantics=("parallel",), vmem_limit_bytes=VMEM_LIMIT),
        name="in_proj",
    )(x2d, g, w_bf16)


def _attn_kernel(slope_ref, q_ref, k_ref, v_ref, gq_ref, gk_ref, go_ref, o_ref,
                 qn_ref, kn_ref, m_ref, l_ref, acc_ref, bias_ref):
    seq = q_ref.shape[1]
    hp = pl.program_id(1)
    head0 = _head0_lanes()
    head_ones = _head_ones()

    def prep(i, carry):
        rows = pl.ds(pl.multiple_of(i * ROW_TILE, ROW_TILE), ROW_TILE)
        q = q_ref[0, rows, :]
        k = k_ref[0, rows, :]
        qn_ref[rows, :] = (q * lax.rsqrt(_head_sum_mxu(q * q, head_ones) * (1.0 / HEAD_DIM)
                                         + NORM_EPS) * gq_ref[...] * (HEAD_DIM ** -0.5))
        kn_ref[rows, :] = (k * lax.rsqrt(_head_sum_mxu(k * k, head_ones) * (1.0 / HEAD_DIM)
                                         + NORM_EPS) * gk_ref[...])
        m_ref[rows, :] = jnp.full((ROW_TILE, LANES), MASKED, F32)
        l_ref[rows, :] = jnp.zeros((ROW_TILE, LANES), F32)
        acc_ref[rows, :] = jnp.zeros((ROW_TILE, LANES), F32)
        return carry

    lax.fori_loop(0, seq // ROW_TILE, prep, 0)

    qi = lax.broadcasted_iota(jnp.int32, (2 * BLK, 2 * BLK), 0)
    ci = lax.broadcasted_iota(jnp.int32, (2 * BLK, 2 * BLK), 1)
    back = jnp.where(qi < BLK, qi, qi - BLK) + BLK - ci
    slope = jnp.where(qi < BLK, slope_ref[2 * hp], slope_ref[2 * hp + 1])
    for p, (window, dil) in enumerate(DILATED_PATTERNS):
        steps = window // dil
        valid = (back >= 0) & (back <= steps)
        bias_ref[p] = jnp.where(valid, -slope * (back.astype(F32) * dil), MASKED)

    def band_blocks(p, dil, starts, has_prevs):
        each = lambda f, *lists: [f(*args) for args in zip(*lists)]
        bf = lambda v: v.astype(BF16)
        ds = lambda start: pl.ds(start, BLK, stride=dil) if dil > 1 else pl.ds(start, BLK)
        rowss = [ds(start) for start in starts]

        def keys_values(start, rows, has_prev):
            kb = kn_ref[rows, :]
            vb = v_ref[0, rows, :]
            if has_prev:
                prows = ds(start - BLK * dil)
                kb = jnp.concatenate([kn_ref[prows, :], kb], axis=0)
                vb = jnp.concatenate([v_ref[0, prows, :], vb], axis=0)
            return bf(kb), bf(vb)
        kv = each(keys_values, starts, rowss, has_prevs)
        qs = each(lambda rows: bf(_stack_heads(qn_ref[rows, :], head0)), rowss)
        s = each(lambda q, kvb, has_prev:
                 _nt_dot(q, kvb[0]) + (bias_ref[p] if has_prev else bias_ref[p, :, BLK:]),
                 qs, kv, has_prevs)
        m_blk = each(lambda v: jnp.max(v, axis=-1, keepdims=True), s)
        pr = each(lambda v, m: jnp.exp(v - m), s, m_blk)
        l_blk = each(lambda v: jnp.sum(v, axis=-1, keepdims=True), pr)
        pv = each(lambda v, kvb: _dot(bf(v), kvb[1]), pr, kv)

        def merge(rows, m_b, l_b, pv_b):
            m_pair = jnp.where(head0, m_b[:BLK], m_b[BLK:])
            l_pair = jnp.where(head0, l_b[:BLK], l_b[BLK:])
            pv_pair = jnp.where(head0, pv_b[:BLK], pv_b[BLK:])
            m_old = m_ref[rows, :]
            m_new = jnp.maximum(m_old, m_pair)
            a_old = jnp.exp(m_old - m_new)
            a_blk = jnp.exp(m_pair - m_new)
            m_ref[rows, :] = m_new
            l_ref[rows, :] = a_old * l_ref[rows, :] + a_blk * l_pair
            acc_ref[rows, :] = a_old * acc_ref[rows, :] + a_blk * pv_pair
        each(merge, rowss, m_blk, l_blk, pv)

    for p, (window, dil) in enumerate(DILATED_PATTERNS):
        n_blk = seq // (BLK * dil)
        span = BLK * dil
        if n_blk == 1:
            def residues(i, carry, p=p, dil=dil):
                band_blocks(p, dil, [i * ATTN_GROUP + g for g in range(ATTN_GROUP)],
                            [False] * ATTN_GROUP)
                return carry
            lax.fori_loop(0, dil // ATTN_GROUP, residues, 0)
        else:
            assert n_blk % ATTN_GROUP == 0
            def first_blocks(r, carry, p=p, dil=dil, span=span):
                band_blocks(p, dil, [r + g * span for g in range(ATTN_GROUP)],
                            [g > 0 for g in range(ATTN_GROUP)])
                return carry
            def later_blocks(i, carry, p=p, dil=dil, span=span, n_blk=n_blk):
                groups = n_blk // ATTN_GROUP - 1
                r = i // groups
                first = (i % groups + 1) * ATTN_GROUP
                starts = [r + (first + g) * span for g in range(ATTN_GROUP)]
                if dil == 1:
                    starts = [pl.multiple_of(st, BLK) for st in starts]
                band_blocks(p, dil, starts, [True] * ATTN_GROUP)
                return carry
            if dil == 1:
                first_blocks(0, 0)
            else:
                lax.fori_loop(0, dil, first_blocks, 0)
            if n_blk > ATTN_GROUP:
                lax.fori_loop(0, dil * (n_blk // ATTN_GROUP - 1), later_blocks, 0)

    def finish(i, carry):
        rows = pl.ds(pl.multiple_of(i * ROW_TILE, ROW_TILE), ROW_TILE)
        o = acc_ref[rows, :] / l_ref[rows, :]
        ms = _head_sum_mxu(o * o, head_ones) * (1.0 / HEAD_DIM)
        o_ref[0, rows, :] = o * lax.rsqrt(ms + NORM_EPS) * go_ref[...]
        return carry

    lax.fori_loop(0, seq // ROW_TILE, finish, 0)


def _attention(proj, slopes, gq, gk, go):
    b, s, _ = proj.shape
    blk = lambda off: pl.BlockSpec((1, s, LANES), lambda i, j: (i, 0, off + j))
    vec = lambda: pl.BlockSpec((1, LANES), lambda i, j: (0, 0))
    return pl.pallas_call(
        _attn_kernel,
        grid=(b, _N_PAIR),
        in_specs=[
            pl.BlockSpec(memory_space=pltpu.SMEM),
            blk(_QB), blk(_KB), blk(_VB),
            vec(), vec(),
            pl.BlockSpec((1, LANES), lambda i, j: (0, j)),
        ],
        out_specs=pl.BlockSpec((1, s, LANES), lambda i, j: (i, 0, j)),
        out_shape=jax.ShapeDtypeStruct((b, s, ATTN_WIDTH), F32),
        scratch_shapes=[
            pltpu.VMEM((s, LANES), F32),
            pltpu.VMEM((s, LANES), F32),
            pltpu.VMEM((s, LANES), F32),
            pltpu.VMEM((s, LANES), F32),
            pltpu.VMEM((s, LANES), F32),
            pltpu.VMEM((len(DILATED_PATTERNS), 2 * BLK, 2 * BLK), F32),
        ],
        compiler_params=pltpu.CompilerParams(
            dimension_semantics=("parallel", "parallel"), vmem_limit_bytes=VMEM_LIMIT),
        name="dilated_attn",
    )(slopes, proj, proj, proj, gq, gk, go)


def _wkv_kernel(pr_ref, pk_ref, pv_ref, pwa_ref, pg_ref,
                mur_ref, muk_ref, muv_ref, muwa_ref, mug_ref,
                w0_ref, a0_ref, kk_ref, ka_ref, rk_ref, lng_ref, lnb_ref,
                wwa_ref, g2_ref, o_ref, h_ref):
    seq = pr_ref.shape[1]
    n_chunks = seq // BLK
    head0 = _head0_lanes()
    row = lax.broadcasted_iota(jnp.int32, (BLK, BLK), 0)
    col = lax.broadcasted_iota(jnp.int32, (BLK, BLK), 1)
    strict = col < row
    incl = col <= row
    tril_incl = jnp.where(incl, 1.0, 0.0).astype(BF16)
    first_row = lax.broadcasted_iota(jnp.int32, (BLK, LANES), 0) == 0
    srow = lax.broadcasted_iota(jnp.int32, (2 * BLK, LANES), 0) < BLK
    slane = lax.broadcasted_iota(jnp.int32, (2 * BLK, LANES), 1) < HEAD_DIM
    own_head = srow == slane
    zeros_blk = jnp.zeros((BLK, BLK), F32)

    h_ref[...] = jnp.zeros((LANES, LANES), F32)

    def chunks_local(cs):
        each = lambda f, *lists: [f(*args) for args in zip(*lists)]
        bf = lambda v: v.astype(BF16)

        row0s = [pl.multiple_of(c * BLK, BLK) for c in cs]
        rowss = [pl.ds(r0, BLK) for r0 in row0s]
        prev8s = [pl.ds(pl.multiple_of(jnp.maximum(r0 - 8, 0), 8), 8) for r0 in row0s]
        has_prevs = [(c > 0).astype(F32) for c in cs]

        def shifted(ref, mu_ref):
            def one(rows, prev8, has_prev):
                p = ref[0, rows, :]
                last = ref[0, prev8, :][7:8, :] * has_prev
                prev = jnp.where(first_row, last, pltpu.roll(p, 1, 0))
                return p + (prev - p) * mu_ref[...]
            return each(one, rowss, prev8s, has_prevs)

        xr = shifted(pr_ref, mur_ref)
        xk = shifted(pk_ref, muk_ref)
        xv = shifted(pv_ref, muv_ref)
        xwa = shifted(pwa_ref, muwa_ref)
        xg = shifted(pg_ref, mug_ref)

        lora = each(lambda v: _dot(bf(jnp.where(head0, jnp.tanh(v), v)), wwa_ref[0]), xwa)
        gate = each(lambda v: _dot(bf(jax.nn.sigmoid(v)), g2_ref[...]), xg)

        def decay_log(lo):
            zw = -(w0_ref[...] + lo[:, :LANES])
            softplus = jnp.maximum(zw, 0.0) + jnp.log(1.0 + jnp.exp(-jnp.abs(zw)))
            return -jnp.exp(-softplus - 0.5)
        log_decay = each(decay_log, lora)
        a_sig = each(lambda lo: jax.nn.sigmoid(a0_ref[...] + lo[:, LANES:]), lora)

        def unit_key(k):
            kkv = k * kk_ref[...]
            return kkv / jnp.maximum(jnp.sqrt(_head_sum(kkv * kkv, head0)), 1e-12)
        kk = each(unit_key, xk)
        k2 = each(lambda k, a: k * (1.0 + (a - 1.0) * ka_ref[...]), xk, a_sig)
        b_vec = each(lambda u, a: u * a, kk, a_sig)

        def cumulative(ld):
            ld_hi = bf(ld)
            rem = ld - ld_hi.astype(F32)
            ld_mid = bf(rem)
            ld_lo = bf(rem - ld_mid.astype(F32))
            cum = _dot(tril_incl, jnp.concatenate([ld_hi, ld_mid, ld_lo], axis=1))
            return cum[:, :LANES] + cum[:, LANES:2 * LANES] + cum[:, 2 * LANES:]
        lw = each(cumulative, log_decay)
        lw_last = [v[BLK - 1:BLK, :] for v in lw]
        w_inv = each(lambda v: jnp.exp(-v), lw)
        w_tail = each(lambda v, last: jnp.exp(last - v), lw, lw_last)
        w_all = each(jnp.exp, lw_last)

        at = each(lambda u, v, ld: -u * jnp.exp(v - ld), kk, lw, log_decay)
        rt = each(lambda r, v: r * jnp.exp(v), xr, lw)
        bt = each(lambda b, w: b * w, b_vec, w_inv)
        kt = each(lambda k, w: k * w, k2, w_inv)
        bh = each(lambda b, w: b * w, b_vec, w_tail)
        kh = each(lambda k, w: k * w, k2, w_tail)

        def pair_products(a, r, b, k):
            lhs = jnp.concatenate([jnp.where(head0, a, 0.0), jnp.where(head0, r, 0.0),
                                   jnp.where(head0, 0.0, a), jnp.where(head0, 0.0, r)], axis=0)
            return _nt_dot(bf(lhs), bf(jnp.concatenate([b, k], axis=0)))
        pp = each(pair_products, at, rt, bt, kt)

        def n_matrix(q):
            aab0 = jnp.where(strict, q[0:BLK, :BLK], 0.0)
            aab1 = jnp.where(strict, q[2 * BLK:3 * BLK, :BLK], 0.0)
            return bf(jnp.concatenate([jnp.concatenate([aab0, zeros_blk], axis=1),
                                       jnp.concatenate([zeros_blk, aab1], axis=1)], axis=0))
        n_bf = each(n_matrix, pp)

        def x_init(q, a, v):
            aak = jnp.concatenate([jnp.where(strict, q[0:BLK, BLK:], 0.0),
                                   jnp.where(strict, q[2 * BLK:3 * BLK, BLK:], 0.0)], axis=0)
            akv = _dot(bf(aak), bf(v))
            return jnp.concatenate([_stack_heads(a, head0), jnp.where(own_head, akv, 0.0)], axis=1)
        x = each(x_init, pp, at, xv)

        span = 1
        while span < BLK:
            x = each(lambda xx, nn: xx + _dot(nn, bf(xx)), x, n_bf)
            span *= 2
            if span < BLK:
                n_bf = each(lambda nn: bf(_dot(nn, nn)), n_bf)
        x_bf = each(bf, x)

        v_s = each(lambda v: bf(_stack_heads(v, head0)), xv)

        def read_side(q, xb, vs, r):
            arb = jnp.concatenate([jnp.where(incl, q[BLK:2 * BLK, :BLK], 0.0),
                                   jnp.where(incl, q[3 * BLK:, :BLK], 0.0)], axis=1)
            ark = jnp.concatenate([jnp.where(incl, q[BLK:2 * BLK, BLK:], 0.0),
                                   jnp.where(incl, q[3 * BLK:, BLK:], 0.0)], axis=1)
            z = _dot(bf(arb), xb)
            return r + z[:, :LANES], z[:, LANES:] + _dot(bf(ark), vs)
        r_eff_y0 = each(read_side, pp, x_bf, v_s, rt)

        def state_side(b, k, xb, vs, wa):
            d1 = _tn_dot(bf(_stack_heads(b, head0)), xb)
            d2 = _tn_dot(bf(_stack_heads(k, head0)), vs)
            return jnp.where(row == col, wa, 0.0) + d1[:, :LANES], d1[:, LANES:] + d2
        m_eff_g_eff = each(state_side, bh, kh, x_bf, v_s, w_all)

        rm_lhs = each(lambda ry, mg: bf(jnp.concatenate([ry[0], mg[0]], axis=0)),
                      r_eff_y0, m_eff_g_eff)
        bonus = each(lambda r, k, v: _head_sum(r * k * rk_ref[...], head0) * v, xr, k2, xv)
        return [(rowss[j], rm_lhs[j], r_eff_y0[j][1], m_eff_g_eff[j][1], bonus[j], gate[j])
                for j in range(len(cs))]

    def chunk_group(i, carry):
        local = chunks_local([i * WKV_GROUP + j for j in range(WKV_GROUP)])
        h = h_ref[...]
        for rows, rm_lhs, y0, g_eff, bonus, gate in local:
            rm = _dot(rm_lhs, h.astype(BF16))
            y = rm[:BLK] + y0
            h = rm[BLK:] + g_eff
            mean = _head_sum(y, head0) * (1.0 / HEAD_DIM)
            yc = y - mean
            var = _head_sum(yc * yc, head0) * (1.0 / HEAD_DIM)
            yn = yc * lax.rsqrt(var + LNX_EPS) * lng_ref[...] + lnb_ref[...]
            o_ref[0, rows, :] = (yn + bonus) * gate
        h_ref[...] = h
        return carry

    lax.fori_loop(0, n_chunks // WKV_GROUP, chunk_group, 0)


def _rwkv(proj, mu, w0, a0, k_k, k_a, r_k, lnx_g, lnx_b, wwa, g2_bf16):
    b, s, _ = proj.shape
    blk = lambda off: pl.BlockSpec((1, s, LANES), lambda i, j: (i, 0, off + j))
    fixed = lambda off: pl.BlockSpec((1, s, LANES), lambda i, j: (i, 0, off))
    vec = lambda off: pl.BlockSpec((1, LANES), lambda i, j: (0, off + j))
    vfix = lambda off: pl.BlockSpec((1, LANES), lambda i, j: (0, off))
    np_ = _N_PAIR
    return pl.pallas_call(
        _wkv_kernel,
        grid=(b, np_),
        in_specs=[
            blk(_RW0), blk(_RW0 + np_), blk(_RW0 + 2 * np_),
            fixed(_RW0 + 3 * np_), fixed(_RW0 + 3 * np_ + 1),
            vec(0), vec(np_), vec(2 * np_), vfix(3 * np_), vfix(3 * np_ + 1),
            vec(0), vec(0), vec(0), vec(0), vec(0), vec(0), vec(0),
            pl.BlockSpec((1, LANES, 2 * LANES), lambda i, j: (j, 0, 0)),
            pl.BlockSpec((GATE_LORA, LANES), lambda i, j: (0, j)),
        ],
        out_specs=pl.BlockSpec((1, s, LANES), lambda i, j: (i, 0, j)),
        out_shape=jax.ShapeDtypeStruct((b, s, RWKV_WIDTH), F32),
        scratch_shapes=[pltpu.VMEM((LANES, LANES), F32)],
        compiler_params=pltpu.CompilerParams(
            dimension_semantics=("parallel", "parallel"), vmem_limit_bytes=VMEM_LIMIT),
        name="rwkv7",
    )(proj, proj, proj, proj, proj, mu, mu, mu, mu, mu,
      w0, a0, k_k, k_a, r_k, lnx_g, lnx_b, wwa, g2_bf16)


def _out_ffn_kernel(x_ref, attn_ref, rwkv_ref, woa_ref, wor_ref, g2_ref,
                    wg_ref, wu_ref, wd_ref, o_ref):
    x1 = (x_ref[...] + _dot(attn_ref[...].astype(BF16), woa_ref[...])
          + _dot(rwkv_ref[...].astype(BF16), wor_ref[...]))
    ms = jnp.mean(x1 * x1, axis=-1, keepdims=True)
    xn = (x1 * lax.rsqrt(ms + NORM_EPS) * g2_ref[...]).astype(BF16)
    gate = _dot(xn, wg_ref[...])
    up = _dot(xn, wu_ref[...])
    hidden = (gate * jax.nn.sigmoid(gate) * up).astype(BF16)
    o_ref[...] = x1 + _dot(hidden, wd_ref[...])


def _out_ffn(x2d, attn2d, rwkv2d, woa, wor, g2, wg, wu, wd, tm):
    t, d = x2d.shape
    f = wg.shape[1]
    tok = lambda w: pl.BlockSpec((tm, w), lambda i: (i, 0))
    res = lambda shape: pl.BlockSpec(shape, lambda i: (0, 0), pipeline_mode=pl.Buffered(1))
    return pl.pallas_call(
        _out_ffn_kernel,
        grid=(t // tm,),
        in_specs=[
            tok(d), tok(ATTN_WIDTH), tok(RWKV_WIDTH),
            res((ATTN_WIDTH, d)), res((RWKV_WIDTH, d)),
            pl.BlockSpec((1, d), lambda i: (0, 0)),
            res((d, f)), res((d, f)), res((f, d)),
        ],
        out_specs=tok(d),
        out_shape=jax.ShapeDtypeStruct((t, d), F32),
        compiler_params=pltpu.CompilerParams(
            dimension_semantics=("parallel",), vmem_limit_bytes=VMEM_LIMIT),
        name="out_ffn",
    )(x2d, attn2d, rwkv2d, woa, wor, g2, wg, wu, wd)


def _layer(x, norm1_g, w_in, q_norm_g, k_norm_g, attn_out_g, rwkv_mu, w0, w2, a0, a2, g2,
           k_k, k_a, r_k, lnx_g, lnx_b, w_out, norm2_g, w_gate, w_up, w_down):
    b, s, d = x.shape
    assert s % (BLK * max(dil for _, dil in DILATED_PATTERNS)) == 0
    assert all(window // dil == BLK for window, dil in DILATED_PATTERNS)
    t = b * s
    tm = 512
    assert t % tm == 0
    row = lambda v: v.reshape(1, -1).astype(F32)

    x2d = x.reshape(t, d)
    proj = _in_proj(x2d, row(norm1_g), w_in.astype(BF16), tm).reshape(b, s, -1)

    slopes = jnp.exp2(-8.0 * jnp.arange(1, ATTN_HEADS + 1, dtype=F32) / ATTN_HEADS)
    pair = lambda v: jnp.tile(v.reshape(1, HEAD_DIM), (1, LANES // HEAD_DIM)).astype(F32)
    attn = _attention(proj, slopes, pair(q_norm_g), pair(k_norm_g), row(attn_out_g))

    w2p = w2.reshape(DECAY_LORA, _N_PAIR, LANES).transpose(1, 0, 2)
    a2p = a2.reshape(ICLR_LORA, _N_PAIR, LANES).transpose(1, 0, 2)
    zero = jnp.zeros_like(w2p)
    wwa = jnp.concatenate([jnp.concatenate([w2p, zero], axis=2),
                           jnp.concatenate([jnp.zeros_like(a2p), a2p], axis=2)], axis=1).astype(BF16)
    rwkv = _rwkv(proj, row(rwkv_mu), row(w0), row(a0), row(k_k), row(k_a), row(r_k),
                 row(lnx_g), row(lnx_b), wwa, g2.astype(BF16))

    out = _out_ffn(x2d, attn.reshape(t, ATTN_WIDTH), rwkv.reshape(t, RWKV_WIDTH),
                   w_out[:ATTN_WIDTH].astype(BF16), w_out[ATTN_WIDTH:].astype(BF16),
                   row(norm2_g), w_gate.astype(BF16), w_up.astype(BF16), w_down.astype(BF16), tm)
    return out.reshape(b, s, d)


def kernel(x, norm1_g, w_in, q_norm_g, k_norm_g, attn_out_g, rwkv_mu, w0, w2, a0, a2, g2,
           k_k, k_a, r_k, lnx_g, lnx_b, w_out, norm2_g, w_gate, w_up, w_down):
    h = x
    for layer in range(norm1_g.shape[0]):
        h = _layer(h, norm1_g[layer], w_in[layer], q_norm_g[layer], k_norm_g[layer],
                   attn_out_g[layer], rwkv_mu[layer], w0[layer], w2[layer], a0[layer],
                   a2[layer], g2[layer], k_k[layer], k_a[layer], r_k[layer], lnx_g[layer],
                   lnx_b[layer], w_out[layer], norm2_g[layer], w_gate[layer], w_up[layer],
                   w_down[layer])
    return h
```

```python
import functools

import jax
import jax.numpy as jnp
from jax import lax
from jax.experimental import pallas as pl
from jax.experimental.pallas import tpu as pltpu

F32 = jnp.float32
BF16 = jnp.bfloat16

HEAD_DIM = 64
LANES = 128
ATTN_HEADS = 8
RWKV_HEADS = 8
ATTN_WIDTH = ATTN_HEADS * HEAD_DIM
RWKV_WIDTH = RWKV_HEADS * HEAD_DIM
DECAY_LORA = 64
ICLR_LORA = 64
GATE_LORA = 128
DILATED_PATTERNS = ((128, 1), (512, 4), (2048, 16))
BLK = 128
WKV_GROUP = 4
ATTN_GROUP = 8
ROW_TILE = 512
NORM_EPS = 1e-6
LNX_EPS = 64e-5
MASKED = -1e30
LOG2_E = 1.4426950408889634
VMEM_LIMIT = 56 * 1024 * 1024

_QB, _KB, _VB = 0, ATTN_WIDTH // LANES, 2 * ATTN_WIDTH // LANES
_RW0 = 3 * ATTN_WIDTH // LANES
_N_PAIR = RWKV_WIDTH // LANES


def _nt_dot(a, b):
    return lax.dot_general(a, b, (((1,), (1,)), ((), ())), preferred_element_type=F32)


def _tn_dot(a, b):
    return lax.dot_general(a, b, (((0,), (0,)), ((), ())), preferred_element_type=F32)


def _dot(a, b):
    return jnp.dot(a, b, preferred_element_type=F32)


def _head0_lanes():
    return lax.broadcasted_iota(jnp.int32, (1, LANES), 1) < HEAD_DIM


def _head_ones():
    i = lax.broadcasted_iota(jnp.int32, (2 * LANES, LANES), 0)
    j = lax.broadcasted_iota(jnp.int32, (2 * LANES, LANES), 1)
    same = ((i % LANES) < HEAD_DIM) == (j < HEAD_DIM)
    return jnp.where(same, 1.0, 0.0).astype(BF16)


def _head_sum_mxu(x, head_ones):
    hi = x.astype(BF16)
    lo = (x - hi.astype(F32)).astype(BF16)
    return _dot(jnp.concatenate([hi, lo], axis=1), head_ones)


def _head_sum(x, head0):
    s0 = jnp.sum(jnp.where(head0, x, 0.0), axis=-1, keepdims=True)
    s1 = jnp.sum(jnp.where(head0, 0.0, x), axis=-1, keepdims=True)
    return jnp.where(head0, s0, s1)


def _stack_heads(x, head0):
    return jnp.concatenate([jnp.where(head0, x, 0.0), jnp.where(head0, 0.0, x)], axis=0)


def _in_proj_kernel(x_ref, g_ref, w_ref, o_ref):
    x = x_ref[...]
    ms = jnp.mean(x * x, axis=-1, keepdims=True)
    xn = x * lax.rsqrt(ms + NORM_EPS) * g_ref[...]
    o_ref[...] = _dot(xn.astype(BF16), w_ref[...])


def _in_proj(x2d, g, w_bf16, tm):
    t, d = x2d.shape
    n = w_bf16.shape[1]
    return pl.pallas_call(
        _in_proj_kernel,
        grid=(t // tm,),
        in_specs=[
            pl.BlockSpec((tm, d), lambda i: (i, 0)),
            pl.BlockSpec((1, d), lambda i: (0, 0)),
            pl.BlockSpec((d, n), lambda i: (0, 0), pipeline_mode=pl.Buffered(1)),
        ],
        out_specs=pl.BlockSpec((tm, n), lambda i: (i, 0)),
        out_shape=jax.ShapeDtypeStruct((t, n), F32),
        compiler_params=pltpu.CompilerParams(
            dimension_semantics=("parallel",), vmem_limit_bytes=VMEM_LIMIT),
        name="in_proj",
    )(x2d, g, w_bf16)


def _attn_kernel(slope_ref, q_ref, k_ref, v_ref, gq_ref, gk_ref, go_ref, o_ref,
                 qn_ref, kn_ref, m_ref, l_ref, acc_ref, bias_ref):
    seq = q_ref.shape[1]
    hp = pl.program_id(1)
    head0 = _head0_lanes()
    head_ones = _head_ones()

    def prep(i, carry):
        rows = pl.ds(pl.multiple_of(i * ROW_TILE, ROW_TILE), ROW_TILE)
        q = q_ref[0, rows, :]
        k = k_ref[0, rows, :]
        qn_ref[rows, :] = (q * lax.rsqrt(_head_sum_mxu(q * q, head_ones) * (1.0 / HEAD_DIM)
                                         + NORM_EPS) * gq_ref[...] * (HEAD_DIM ** -0.5 * LOG2_E))
        kn_ref[rows, :] = (k * lax.rsqrt(_head_sum_mxu(k * k, head_ones) * (1.0 / HEAD_DIM)
                                         + NORM_EPS) * gk_ref[...])
        return carry

    lax.fori_loop(0, seq // ROW_TILE, prep, 0)

    qi = lax.broadcasted_iota(jnp.int32, (2 * BLK, 2 * BLK), 0)
    ci = lax.broadcasted_iota(jnp.int32, (2 * BLK, 2 * BLK), 1)
    back = jnp.where(qi < BLK, qi, qi - BLK) + BLK - ci
    slope = jnp.where(qi < BLK, slope_ref[2 * hp], slope_ref[2 * hp + 1]) * LOG2_E
    for p, (window, dil) in enumerate(DILATED_PATTERNS):
        steps = window // dil
        valid = (back >= 0) & (back <= steps)
        bias_ref[p] = jnp.where(valid, -slope * (back.astype(F32) * dil), MASKED)
    ones_keys = jnp.ones((2 * BLK, LANES), BF16)

    def band_blocks(p, dil, starts, has_prevs):
        each = lambda f, *lists: [f(*args) for args in zip(*lists)]
        bf = lambda v: v.astype(BF16)
        ds = lambda start: pl.ds(start, BLK, stride=dil) if dil > 1 else pl.ds(start, BLK)
        rowss = [ds(start) for start in starts]

        def keys_values(start, rows, has_prev):
            kb = kn_ref[rows, :]
            vb = v_ref[0, rows, :]
            if has_prev:
                prows = ds(start - BLK * dil)
                kb = jnp.concatenate([kn_ref[prows, :], kb], axis=0)
                vb = jnp.concatenate([v_ref[0, prows, :], vb], axis=0)
            return bf(kb), jnp.concatenate([bf(vb), ones_keys[:vb.shape[0]]], axis=1)
        kv = each(keys_values, starts, rowss, has_prevs)
        qs = each(lambda rows: bf(_stack_heads(qn_ref[rows, :], head0)), rowss)
        s = each(lambda q, kvb, has_prev:
                 _nt_dot(q, kvb[0]) + (bias_ref[p] if has_prev else bias_ref[p, :, BLK:]),
                 qs, kv, has_prevs)
        m_blk = each(lambda v: jnp.max(v, axis=-1, keepdims=True), s)
        pr = each(lambda v, m: bf(jnp.exp2(v - m)), s, m_blk)
        pv = each(lambda v, kvb: _dot(v, kvb[1]), pr, kv)

        def keep(rows, m_b, pv_b):
            m_ref[p, rows, :] = jnp.where(head0, m_b[:BLK], m_b[BLK:])
            acc_ref[p, rows, :] = jnp.where(head0, pv_b[:BLK, :LANES], pv_b[BLK:, :LANES])
            l_ref[p, rows, :] = jnp.where(head0, pv_b[:BLK, LANES:], pv_b[BLK:, LANES:])
        each(keep, rowss, m_blk, pv)

    for p, (window, dil) in enumerate(DILATED_PATTERNS):
        n_blk = seq // (BLK * dil)
        span = BLK * dil
        if n_blk >= ATTN_GROUP:
            assert n_blk % ATTN_GROUP == 0
            groups = n_blk // ATTN_GROUP

            def residue(r, carry, p=p, dil=dil, span=span, groups=groups):
                for grp in range(groups):
                    blocks = range(grp * ATTN_GROUP, (grp + 1) * ATTN_GROUP)
                    band_blocks(p, dil, [r + n * span for n in blocks], [n > 0 for n in blocks])
                return carry
        else:
            assert ATTN_GROUP % n_blk == 0 and dil % (ATTN_GROUP // n_blk) == 0
            per_group = ATTN_GROUP // n_blk

            def residue(i, carry, p=p, dil=dil, span=span, n_blk=n_blk, per_group=per_group):
                classes = [i * per_group + g for g in range(per_group)]
                band_blocks(p, dil, [r + n * span for r in classes for n in range(n_blk)],
                            [n > 0 for r in classes for n in range(n_blk)])
                return carry
        n_steps = dil if n_blk >= ATTN_GROUP else dil // (ATTN_GROUP // n_blk)
        if n_steps == 1:
            residue(0, 0)
        else:
            lax.fori_loop(0, n_steps, residue, 0)

    def finish(i, carry):
        rows = pl.ds(pl.multiple_of(i * ROW_TILE, ROW_TILE), ROW_TILE)
        n_pat = len(DILATED_PATTERNS)
        m_p = [m_ref[p, rows, :] for p in range(n_pat)]
        m_all = functools.reduce(jnp.maximum, m_p)
        scale = [jnp.exp2(m - m_all) for m in m_p]
        num = sum(sc * acc_ref[p, rows, :] for p, sc in enumerate(scale))
        den = sum(sc * l_ref[p, rows, :] for p, sc in enumerate(scale))
        o = num / den
        ms = _head_sum_mxu(o * o, head_ones) * (1.0 / HEAD_DIM)
        o_ref[0, rows, :] = o * lax.rsqrt(ms + NORM_EPS) * go_ref[...]
        return carry

    lax.fori_loop(0, seq // ROW_TILE, finish, 0)


def _attention(proj, slopes, gq, gk, go):
    b, s, _ = proj.shape
    blk = lambda off: pl.BlockSpec((1, s, LANES), lambda i, j: (i, 0, off + j))
    vec = lambda: pl.BlockSpec((1, LANES), lambda i, j: (0, 0))
    return pl.pallas_call(
        _attn_kernel,
        grid=(b, _N_PAIR),
        in_specs=[
            pl.BlockSpec(memory_space=pltpu.SMEM),
            blk(_QB), blk(_KB), blk(_VB),
            vec(), vec(),
            pl.BlockSpec((1, LANES), lambda i, j: (0, j)),
        ],
        out_specs=pl.BlockSpec((1, s, LANES), lambda i, j: (i, 0, j)),
        out_shape=jax.ShapeDtypeStruct((b, s, ATTN_WIDTH), F32),
        scratch_shapes=[
            pltpu.VMEM((s, LANES), F32),
            pltpu.VMEM((s, LANES), F32),
            pltpu.VMEM((len(DILATED_PATTERNS), s, LANES), F32),
            pltpu.VMEM((len(DILATED_PATTERNS), s, LANES), F32),
            pltpu.VMEM((len(DILATED_PATTERNS), s, LANES), F32),
            pltpu.VMEM((len(DILATED_PATTERNS), 2 * BLK, 2 * BLK), F32),
        ],
        compiler_params=pltpu.CompilerParams(
            dimension_semantics=("parallel", "parallel"), vmem_limit_bytes=VMEM_LIMIT),
        name="dilated_attn",
    )(slopes, proj, proj, proj, gq, gk, go)


def _wkv_kernel(pr_ref, pk_ref, pv_ref, pwa_ref, pg_ref,
                mur_ref, muk_ref, muv_ref, muwa_ref, mug_ref,
                w0_ref, a0_ref, kk_ref, ka_ref, rk_ref, lng_ref, lnb_ref,
                wwa_ref, g2_ref, o_ref, h_ref):
    seq = pr_ref.shape[1]
    n_chunks = seq // BLK
    head0 = _head0_lanes()
    row = lax.broadcasted_iota(jnp.int32, (BLK, BLK), 0)
    col = lax.broadcasted_iota(jnp.int32, (BLK, BLK), 1)
    strict = col < row
    incl = col <= row
    tril_incl = jnp.where(incl, 1.0, 0.0).astype(BF16)
    first_row = lax.broadcasted_iota(jnp.int32, (BLK, LANES), 0) == 0
    srow = lax.broadcasted_iota(jnp.int32, (2 * BLK, LANES), 0) < BLK
    slane = lax.broadcasted_iota(jnp.int32, (2 * BLK, LANES), 1) < HEAD_DIM
    own_head = srow == slane
    zeros_blk = jnp.zeros((BLK, BLK), F32)

    h_ref[...] = jnp.zeros((LANES, LANES), F32)

    def chunks_local(cs):
        each = lambda f, *lists: [f(*args) for args in zip(*lists)]
        bf = lambda v: v.astype(BF16)

        row0s = [pl.multiple_of(c * BLK, BLK) for c in cs]
        rowss = [pl.ds(r0, BLK) for r0 in row0s]
        prev8s = [pl.ds(pl.multiple_of(jnp.maximum(r0 - 8, 0), 8), 8) for r0 in row0s]
        has_prevs = [jnp.where(c > 0, 1.0, 0.0).astype(F32) for c in cs]

        def shifted(ref, mu_ref):
            def one(rows, prev8, has_prev):
                p = ref[0, rows, :]
                last = ref[0, prev8, :][7:8, :] * has_prev
                prev = jnp.where(first_row, last, pltpu.roll(p, 1, 0))
                return p + (prev - p) * mu_ref[...]
            return each(one, rowss, prev8s, has_prevs)

        xr = shifted(pr_ref, mur_ref)
        xk = shifted(pk_ref, muk_ref)
        xv = shifted(pv_ref, muv_ref)
        xwa = shifted(pwa_ref, muwa_ref)
        xg = shifted(pg_ref, mug_ref)

        lora = each(lambda v: _dot(bf(jnp.where(head0, jnp.tanh(v), v)), wwa_ref[0]), xwa)
        gate = each(lambda v: _dot(bf(jax.nn.sigmoid(v)), g2_ref[...]), xg)

        def decay_log(lo):
            zw = -(w0_ref[...] + lo[:, :LANES])
            softplus = jnp.maximum(zw, 0.0) + jnp.log(1.0 + jnp.exp(-jnp.abs(zw)))
            return -jnp.exp(-softplus - 0.5)
        log_decay = each(decay_log, lora)
        a_sig = each(lambda lo: jax.nn.sigmoid(a0_ref[...] + lo[:, LANES:]), lora)

        def unit_key(k):
            kkv = k * kk_ref[...]
            return kkv / jnp.maximum(jnp.sqrt(_head_sum(kkv * kkv, head0)), 1e-12)
        kk = each(unit_key, xk)
        k2 = each(lambda k, a: k * (1.0 + (a - 1.0) * ka_ref[...]), xk, a_sig)
        b_vec = each(lambda u, a: u * a, kk, a_sig)

        def cumulative(ld):
            ld_hi = bf(ld)
            rem = ld - ld_hi.astype(F32)
            ld_mid = bf(rem)
            ld_lo = bf(rem - ld_mid.astype(F32))
            cum = _dot(tril_incl, jnp.concatenate([ld_hi, ld_mid, ld_lo], axis=1))
            return cum[:, :LANES] + cum[:, LANES:2 * LANES] + cum[:, 2 * LANES:]
        lw = each(cumulative, log_decay)
        lw_last = [v[BLK - 1:BLK, :] for v in lw]
        w_inv = each(lambda v: jnp.exp(-v), lw)
        w_tail = each(lambda v, last: jnp.exp(last - v), lw, lw_last)
        w_all = each(jnp.exp, lw_last)

        at = each(lambda u, v, ld: -u * jnp.exp(v - ld), kk, lw, log_decay)
        rt = each(lambda r, v: r * jnp.exp(v), xr, lw)
        bt = each(lambda b, w: b * w, b_vec, w_inv)
        kt = each(lambda k, w: k * w, k2, w_inv)
        bh = each(lambda b, w: b * w, b_vec, w_tail)
        kh = each(lambda k, w: k * w, k2, w_tail)

        def pair_products(a, r, b, k):
            lhs = jnp.concatenate([jnp.where(head0, a, 0.0), jnp.where(head0, r, 0.0),
                                   jnp.where(head0, 0.0, a), jnp.where(head0, 0.0, r)], axis=0)
            return _nt_dot(bf(lhs), bf(jnp.concatenate([b, k], axis=0)))
        pp = each(pair_products, at, rt, bt, kt)

        def n_matrix(q):
            aab0 = jnp.where(strict, q[0:BLK, :BLK], 0.0)
            aab1 = jnp.where(strict, q[2 * BLK:3 * BLK, :BLK], 0.0)
            return bf(jnp.concatenate([jnp.concatenate([aab0, zeros_blk], axis=1),
                                       jnp.concatenate([zeros_blk, aab1], axis=1)], axis=0))
        n_bf = each(n_matrix, pp)

        def x_init(q, a, v):
            aak = jnp.concatenate([jnp.where(strict, q[0:BLK, BLK:], 0.0),
                                   jnp.where(strict, q[2 * BLK:3 * BLK, BLK:], 0.0)], axis=0)
            akv = _dot(bf(aak), bf(v))
            return jnp.concatenate([_stack_heads(a, head0), jnp.where(own_head, akv, 0.0)], axis=1)
        x = each(x_init, pp, at, xv)

        span = 1
        while span < BLK:
            x = each(lambda xx, nn: xx + _dot(nn, bf(xx)), x, n_bf)
            span *= 2
            if span < BLK:
                n_bf = each(lambda nn: bf(_dot(nn, nn)), n_bf)
        x_bf = each(bf, x)

        v_s = each(lambda v: bf(_stack_heads(v, head0)), xv)

        def read_side(q, xb, vs, r):
            arb = jnp.concatenate([jnp.where(incl, q[BLK:2 * BLK, :BLK], 0.0),
                                   jnp.where(incl, q[3 * BLK:, :BLK], 0.0)], axis=1)
            ark = jnp.concatenate([jnp.where(incl, q[BLK:2 * BLK, BLK:], 0.0),
                                   jnp.where(incl, q[3 * BLK:, BLK:], 0.0)], axis=1)
            z = _dot(bf(arb), xb)
            return r + z[:, :LANES], z[:, LANES:] + _dot(bf(ark), vs)
        r_eff_y0 = each(read_side, pp, x_bf, v_s, rt)

        def state_side(b, k, xb, vs, wa):
            d1 = _tn_dot(bf(_stack_heads(b, head0)), xb)
            d2 = _tn_dot(bf(_stack_heads(k, head0)), vs)
            return jnp.where(row == col, wa, 0.0) + d1[:, :LANES], d1[:, LANES:] + d2
        m_eff_g_eff = each(state_side, bh, kh, x_bf, v_s, w_all)

        rm_lhs = each(lambda ry, mg: bf(jnp.concatenate([ry[0], mg[0]], axis=0)),
                      r_eff_y0, m_eff_g_eff)
        bonus = each(lambda r, k, v: _head_sum(r * k * rk_ref[...], head0) * v, xr, k2, xv)
        return [(rowss[j], rm_lhs[j], r_eff_y0[j][1], m_eff_g_eff[j][1], bonus[j], gate[j])
                for j in range(len(cs))]

    def chunk_group(i, carry):
        local = chunks_local([i * WKV_GROUP + j for j in range(WKV_GROUP)])
        h = h_ref[...]
        for rows, rm_lhs, y0, g_eff, bonus, gate in local:
            rm = _dot(rm_lhs, h.astype(BF16))
            y = rm[:BLK] + y0
            h = rm[BLK:] + g_eff
            mean = _head_sum(y, head0) * (1.0 / HEAD_DIM)
            yc = y - mean
            var = _head_sum(yc * yc, head0) * (1.0 / HEAD_DIM)
            yn = yc * lax.rsqrt(var + LNX_EPS) * lng_ref[...] + lnb_ref[...]
            o_ref[0, rows, :] = (yn + bonus) * gate
        h_ref[...] = h
        return carry

    lax.fori_loop(0, n_chunks // WKV_GROUP, chunk_group, 0)


def _rwkv(proj, mu, w0, a0, k_k, k_a, r_k, lnx_g, lnx_b, wwa, g2_bf16):
    b, s, _ = proj.shape
    blk = lambda off: pl.BlockSpec((1, s, LANES), lambda i, j: (i, 0, off + j))
    fixed = lambda off: pl.BlockSpec((1, s, LANES), lambda i, j: (i, 0, off))
    vec = lambda off: pl.BlockSpec((1, LANES), lambda i, j: (0, off + j))
    vfix = lambda off: pl.BlockSpec((1, LANES), lambda i, j: (0, off))
    np_ = _N_PAIR
    return pl.pallas_call(
        _wkv_kernel,
        grid=(b, np_),
        in_specs=[
            blk(_RW0), blk(_RW0 + np_), blk(_RW0 + 2 * np_),
            fixed(_RW0 + 3 * np_), fixed(_RW0 + 3 * np_ + 1),
            vec(0), vec(np_), vec(2 * np_), vfix(3 * np_), vfix(3 * np_ + 1),
            vec(0), vec(0), vec(0), vec(0), vec(0), vec(0), vec(0),
            pl.BlockSpec((1, LANES, 2 * LANES), lambda i, j: (j, 0, 0)),
            pl.BlockSpec((GATE_LORA, LANES), lambda i, j: (0, j)),
        ],
        out_specs=pl.BlockSpec((1, s, LANES), lambda i, j: (i, 0, j)),
        out_shape=jax.ShapeDtypeStruct((b, s, RWKV_WIDTH), F32),
        scratch_shapes=[pltpu.VMEM((LANES, LANES), F32)],
        compiler_params=pltpu.CompilerParams(
            dimension_semantics=("parallel", "parallel"), vmem_limit_bytes=VMEM_LIMIT),
        name="rwkv7",
    )(proj, proj, proj, proj, proj, mu, mu, mu, mu, mu,
      w0, a0, k_k, k_a, r_k, lnx_g, lnx_b, wwa, g2_bf16)


def _out_ffn_kernel(x_ref, attn_ref, rwkv_ref, woa_ref, wor_ref, g2_ref,
                    wg_ref, wu_ref, wd_ref, o_ref):
    x1 = (x_ref[...] + _dot(attn_ref[...].astype(BF16), woa_ref[...])
          + _dot(rwkv_ref[...].astype(BF16), wor_ref[...]))
    ms = jnp.mean(x1 * x1, axis=-1, keepdims=True)
    xn = (x1 * lax.rsqrt(ms + NORM_EPS) * g2_ref[...]).astype(BF16)
    gate = _dot(xn, wg_ref[...])
    up = _dot(xn, wu_ref[...])
    hidden = (gate * jax.nn.sigmoid(gate) * up).astype(BF16)
    o_ref[...] = x1 + _dot(hidden, wd_ref[...])


def _out_ffn(x2d, attn2d, rwkv2d, woa, wor, g2, wg, wu, wd, tm):
    t, d = x2d.shape
    f = wg.shape[1]
    tok = lambda w: pl.BlockSpec((tm, w), lambda i: (i, 0))
    res = lambda shape: pl.BlockSpec(shape, lambda i: (0, 0), pipeline_mode=pl.Buffered(1))
    return pl.pallas_call(
        _out_ffn_kernel,
        grid=(t // tm,),
        in_specs=[
            tok(d), tok(ATTN_WIDTH), tok(RWKV_WIDTH),
            res((ATTN_WIDTH, d)), res((RWKV_WIDTH, d)),
            pl.BlockSpec((1, d), lambda i: (0, 0)),
            res((d, f)), res((d, f)), res((f, d)),
        ],
        out_specs=tok(d),
        out_shape=jax.ShapeDtypeStruct((t, d), F32),
        compiler_params=pltpu.CompilerParams(
            dimension_semantics=("parallel",), vmem_limit_bytes=VMEM_LIMIT),
        name="out_ffn",
    )(x2d, attn2d, rwkv2d, woa, wor, g2, wg, wu, wd)


def _layer(x, norm1_g, w_in, q_norm_g, k_norm_g, attn_out_g, rwkv_mu, w0, w2, a0, a2, g2,
           k_k, k_a, r_k, lnx_g, lnx_b, w_out, norm2_g, w_gate, w_up, w_down):
    b, s, d = x.shape
    assert s % (BLK * max(dil for _, dil in DILATED_PATTERNS)) == 0
    assert all(window // dil == BLK for window, dil in DILATED_PATTERNS)
    t = b * s
    tm = 512
    assert t % tm == 0
    row = lambda v: v.reshape(1, -1).astype(F32)

    x2d = x.reshape(t, d)
    proj = _in_proj(x2d, row(norm1_g), w_in.astype(BF16), tm).reshape(b, s, -1)

    slopes = jnp.exp2(-8.0 * jnp.arange(1, ATTN_HEADS + 1, dtype=F32) / ATTN_HEADS)
    pair = lambda v: jnp.tile(v.reshape(1, HEAD_DIM), (1, LANES // HEAD_DIM)).astype(F32)
    attn = _attention(proj, slopes, pair(q_norm_g), pair(k_norm_g), row(attn_out_g))

    w2p = w2.reshape(DECAY_LORA, _N_PAIR, LANES).transpose(1, 0, 2)
    a2p = a2.reshape(ICLR_LORA, _N_PAIR, LANES).transpose(1, 0, 2)
    zero = jnp.zeros_like(w2p)
    wwa = jnp.concatenate([jnp.concatenate([w2p, zero], axis=2),
                           jnp.concatenate([jnp.zeros_like(a2p), a2p], axis=2)], axis=1).astype(BF16)
    rwkv = _rwkv(proj, row(rwkv_mu), row(w0), row(a0), row(k_k), row(k_a), row(r_k),
                 row(lnx_g), row(lnx_b), wwa, g2.astype(BF16))

    out = _out_ffn(x2d, attn.reshape(t, ATTN_WIDTH), rwkv.reshape(t, RWKV_WIDTH),
                   w_out[:ATTN_WIDTH].astype(BF16), w_out[ATTN_WIDTH:].astype(BF16),
                   row(norm2_g), w_gate.astype(BF16), w_up.astype(BF16), w_down.astype(BF16), tm)
    return out.reshape(b, s, d)


def kernel(x, norm1_g, w_in, q_norm_g, k_norm_g, attn_out_g, rwkv_mu, w0, w2, a0, a2, g2,
           k_k, k_a, r_k, lnx_g, lnx_b, w_out, norm2_g, w_gate, w_up, w_down):
    h = x
    for layer in range(norm1_g.shape[0]):
        h = _layer(h, norm1_g[layer], w_in[layer], q_norm_g[layer], k_norm_g[layer],
                   attn_out_g[layer], rwkv_mu[layer], w0[layer], w2[layer], a0[layer],
                   a2[layer], g2[layer], k_k[layer], k_a[layer], r_k[layer], lnx_g[layer],
                   lnx_b[layer], w_out[layer], norm2_g[layer], w_gate[layer], w_up[layer],
                   w_down[layer])
    return h
```

```python
import functools

import jax
import jax.numpy as jnp
from jax import lax
from jax.experimental import pallas as pl
from jax.experimental.pallas import tpu as pltpu

F32 = jnp.float32
BF16 = jnp.bfloat16

HEAD_DIM = 64
LANES = 128
ATTN_HEADS = 8
RWKV_HEADS = 8
ATTN_WIDTH = ATTN_HEADS * HEAD_DIM
RWKV_WIDTH = RWKV_HEADS * HEAD_DIM
DECAY_LORA = 64
ICLR_LORA = 64
GATE_LORA = 128
DILATED_PATTERNS = ((128, 1), (512, 4), (2048, 16))
BLK = 128
ATTN_GROUP = 8
ROW_TILE = 512
NORM_EPS = 1e-6
LNX_EPS = 64e-5
MASKED = -1e30
LOG2_E = 1.4426950408889634
VMEM_LIMIT = 56 * 1024 * 1024

_QB, _KB, _VB = 0, ATTN_WIDTH // LANES, 2 * ATTN_WIDTH // LANES
_RW0 = 3 * ATTN_WIDTH // LANES
_N_PAIR = RWKV_WIDTH // LANES


def _nt_dot(a, b):
    return lax.dot_general(a, b, (((1,), (1,)), ((), ())), preferred_element_type=F32)


def _tn_dot(a, b):
    return lax.dot_general(a, b, (((0,), (0,)), ((), ())), preferred_element_type=F32)


def _dot(a, b):
    return jnp.dot(a, b, preferred_element_type=F32)


def _aligned(index, multiple):
    if isinstance(index, int):
        return index
    return pl.multiple_of(index, multiple)


def _head0_lanes():
    return lax.broadcasted_iota(jnp.int32, (1, LANES), 1) < HEAD_DIM


def _head_ones():
    i = lax.broadcasted_iota(jnp.int32, (2 * LANES, LANES), 0)
    j = lax.broadcasted_iota(jnp.int32, (2 * LANES, LANES), 1)
    same = ((i % LANES) < HEAD_DIM) == (j < HEAD_DIM)
    return jnp.where(same, 1.0, 0.0).astype(BF16)


def _head_sum_mxu(x, head_ones):
    hi = x.astype(BF16)
    lo = (x - hi.astype(F32)).astype(BF16)
    return _dot(jnp.concatenate([hi, lo], axis=1), head_ones)


def _head_sum(x, head0):
    s0 = jnp.sum(jnp.where(head0, x, 0.0), axis=-1, keepdims=True)
    s1 = jnp.sum(jnp.where(head0, 0.0, x), axis=-1, keepdims=True)
    return jnp.where(head0, s0, s1)


def _stack_heads(x, head0):
    return jnp.concatenate([jnp.where(head0, x, 0.0), jnp.where(head0, 0.0, x)], axis=0)


def _in_proj_kernel(x_ref, g_ref, w_ref, o_ref):
    x = x_ref[...]
    ms = jnp.mean(x * x, axis=-1, keepdims=True)
    xn = x * lax.rsqrt(ms + NORM_EPS) * g_ref[...]
    o_ref[...] = _dot(xn.astype(BF16), w_ref[...])


def _in_proj(x2d, g, w_bf16, tm):
    t, d = x2d.shape
    n = w_bf16.shape[1]
    return pl.pallas_call(
        _in_proj_kernel,
        grid=(t // tm,),
        in_specs=[
            pl.BlockSpec((tm, d), lambda i: (i, 0)),
            pl.BlockSpec((1, d), lambda i: (0, 0)),
            pl.BlockSpec((d, n), lambda i: (0, 0), pipeline_mode=pl.Buffered(1)),
        ],
        out_specs=pl.BlockSpec((tm, n), lambda i: (i, 0)),
        out_shape=jax.ShapeDtypeStruct((t, n), F32),
        compiler_params=pltpu.CompilerParams(
            dimension_semantics=("parallel",), vmem_limit_bytes=VMEM_LIMIT),
        name="in_proj",
    )(x2d, g, w_bf16)


def _attn_kernel(slope_ref, q_ref, k_ref, v_ref, gq_ref, gk_ref, go_ref, o_ref,
                 qn_ref, kn_ref, m_ref, l_ref, acc_ref, bias_ref):
    seq = q_ref.shape[1]
    hp = pl.program_id(1)
    head0 = _head0_lanes()
    head_ones = _head_ones()

    def prep(i, carry):
        rows = pl.ds(pl.multiple_of(i * ROW_TILE, ROW_TILE), ROW_TILE)
        q = q_ref[0, rows, :]
        k = k_ref[0, rows, :]
        qn_ref[rows, :] = (q * lax.rsqrt(_head_sum_mxu(q * q, head_ones) * (1.0 / HEAD_DIM)
                                         + NORM_EPS) * gq_ref[...] * (HEAD_DIM ** -0.5 * LOG2_E))
        kn_ref[rows, :] = (k * lax.rsqrt(_head_sum_mxu(k * k, head_ones) * (1.0 / HEAD_DIM)
                                         + NORM_EPS) * gk_ref[...])
        return carry

    lax.fori_loop(0, seq // ROW_TILE, prep, 0)

    qi = lax.broadcasted_iota(jnp.int32, (2 * BLK, 2 * BLK), 0)
    ci = lax.broadcasted_iota(jnp.int32, (2 * BLK, 2 * BLK), 1)
    back = jnp.where(qi < BLK, qi, qi - BLK) + BLK - ci
    slope = jnp.where(qi < BLK, slope_ref[2 * hp], slope_ref[2 * hp + 1]) * LOG2_E
    for p, (window, dil) in enumerate(DILATED_PATTERNS):
        steps = window // dil
        valid = (back >= 0) & (back <= steps)
        bias_ref[p] = jnp.where(valid, -slope * (back.astype(F32) * dil), MASKED)
    ones_keys = jnp.ones((2 * BLK, LANES), BF16)

    def band_blocks(p, dil, starts, has_prevs):
        each = lambda f, *lists: [f(*args) for args in zip(*lists)]
        bf = lambda v: v.astype(BF16)
        ds = lambda start: pl.ds(start, BLK, stride=dil) if dil > 1 else pl.ds(start, BLK)
        rowss = [ds(start) for start in starts]

        def keys_values(start, rows, has_prev):
            kb = kn_ref[rows, :]
            vb = v_ref[0, rows, :]
            if has_prev:
                prows = ds(start - BLK * dil)
                kb = jnp.concatenate([kn_ref[prows, :], kb], axis=0)
                vb = jnp.concatenate([v_ref[0, prows, :], vb], axis=0)
            return bf(kb), jnp.concatenate([bf(vb), ones_keys[:vb.shape[0]]], axis=1)
        kv = each(keys_values, starts, rowss, has_prevs)
        qs = each(lambda rows: bf(_stack_heads(qn_ref[rows, :], head0)), rowss)
        s = each(lambda q, kvb, has_prev:
                 _nt_dot(q, kvb[0]) + (bias_ref[p] if has_prev else bias_ref[p, :, BLK:]),
                 qs, kv, has_prevs)
        m_blk = each(lambda v: jnp.max(v, axis=-1, keepdims=True), s)
        pr = each(lambda v, m: bf(jnp.exp2(v - m)), s, m_blk)
        pv = each(lambda v, kvb: _dot(v, kvb[1]), pr, kv)

        def keep(rows, m_b, pv_b):
            m_ref[p, rows, :] = jnp.where(head0, m_b[:BLK], m_b[BLK:])
            acc_ref[p, rows, :] = jnp.where(head0, pv_b[:BLK, :LANES], pv_b[BLK:, :LANES])
            l_ref[p, rows, :] = jnp.where(head0, pv_b[:BLK, LANES:], pv_b[BLK:, LANES:])
        each(keep, rowss, m_blk, pv)

    for p, (window, dil) in enumerate(DILATED_PATTERNS):
        n_blk = seq // (BLK * dil)
        span = BLK * dil
        if n_blk >= ATTN_GROUP:
            assert n_blk % ATTN_GROUP == 0
            groups = n_blk // ATTN_GROUP

            def residue(r, carry, p=p, dil=dil, span=span, groups=groups):
                for grp in range(groups):
                    blocks = range(grp * ATTN_GROUP, (grp + 1) * ATTN_GROUP)
                    band_blocks(p, dil, [r + n * span for n in blocks], [n > 0 for n in blocks])
                return carry
        else:
            assert ATTN_GROUP % n_blk == 0 and dil % (ATTN_GROUP // n_blk) == 0
            per_group = ATTN_GROUP // n_blk

            def residue(i, carry, p=p, dil=dil, span=span, n_blk=n_blk, per_group=per_group):
                classes = [i * per_group + g for g in range(per_group)]
                band_blocks(p, dil, [r + n * span for r in classes for n in range(n_blk)],
                            [n > 0 for r in classes for n in range(n_blk)])
                return carry
        n_steps = dil if n_blk >= ATTN_GROUP else dil // (ATTN_GROUP // n_blk)
        if n_steps == 1:
            residue(0, 0)
        else:
            lax.fori_loop(0, n_steps, residue, 0)

    def finish(i, carry):
        rows = pl.ds(pl.multiple_of(i * ROW_TILE, ROW_TILE), ROW_TILE)
        n_pat = len(DILATED_PATTERNS)
        m_p = [m_ref[p, rows, :] for p in range(n_pat)]
        m_all = functools.reduce(jnp.maximum, m_p)
        scale = [jnp.exp2(m - m_all) for m in m_p]
        num = sum(sc * acc_ref[p, rows, :] for p, sc in enumerate(scale))
        den = sum(sc * l_ref[p, rows, :] for p, sc in enumerate(scale))
        o = num / den
        ms = _head_sum_mxu(o * o, head_ones) * (1.0 / HEAD_DIM)
        o_ref[0, rows, :] = o * lax.rsqrt(ms + NORM_EPS) * go_ref[...]
        return carry

    lax.fori_loop(0, seq // ROW_TILE, finish, 0)


def _attention(proj, slopes, gq, gk, go):
    b, s, _ = proj.shape
    blk = lambda off: pl.BlockSpec((1, s, LANES), lambda i, j: (i, 0, off + j))
    vec = lambda: pl.BlockSpec((1, LANES), lambda i, j: (0, 0))
    return pl.pallas_call(
        _attn_kernel,
        grid=(b, _N_PAIR),
        in_specs=[
            pl.BlockSpec(memory_space=pltpu.SMEM),
            blk(_QB), blk(_KB), blk(_VB),
            vec(), vec(),
            pl.BlockSpec((1, LANES), lambda i, j: (0, j)),
        ],
        out_specs=pl.BlockSpec((1, s, LANES), lambda i, j: (i, 0, j)),
        out_shape=jax.ShapeDtypeStruct((b, s, ATTN_WIDTH), F32),
        scratch_shapes=[
            pltpu.VMEM((s, LANES), F32),
            pltpu.VMEM((s, LANES), F32),
            pltpu.VMEM((len(DILATED_PATTERNS), s, LANES), F32),
            pltpu.VMEM((len(DILATED_PATTERNS), s, LANES), F32),
            pltpu.VMEM((len(DILATED_PATTERNS), s, LANES), F32),
            pltpu.VMEM((len(DILATED_PATTERNS), 2 * BLK, 2 * BLK), F32),
        ],
        compiler_params=pltpu.CompilerParams(
            dimension_semantics=("parallel", "parallel"), vmem_limit_bytes=VMEM_LIMIT),
        name="dilated_attn",
    )(slopes, proj, proj, proj, gq, gk, go)


def _wkv_kernel(pr_ref, pk_ref, pv_ref, pwa_ref, pg_ref,
                mur_ref, muk_ref, muv_ref, muwa_ref, mug_ref,
                w0_ref, a0_ref, kk_ref, ka_ref, rk_ref, lng_ref, lnb_ref,
                wwa_ref, g2_ref, o_ref, h_ref, *slots):
    seq = pr_ref.shape[1]
    n_chunks = seq // BLK
    head0 = _head0_lanes()
    row = lax.broadcasted_iota(jnp.int32, (BLK, BLK), 0)
    col = lax.broadcasted_iota(jnp.int32, (BLK, BLK), 1)
    strict = col < row
    incl = col <= row
    tril_incl = jnp.where(incl, 1.0, 0.0).astype(BF16)
    first_row = lax.broadcasted_iota(jnp.int32, (BLK, LANES), 0) == 0
    srow = lax.broadcasted_iota(jnp.int32, (2 * BLK, LANES), 0) < BLK
    slane = lax.broadcasted_iota(jnp.int32, (2 * BLK, LANES), 1) < HEAD_DIM
    own_head = srow == slane
    zeros_blk = jnp.zeros((BLK, BLK), F32)

    h_ref[...] = jnp.zeros(h_ref.shape, F32)

    each = lambda f, *lists: [f(*args) for args in zip(*lists)]
    bf = lambda v: v.astype(BF16)
    pairs = range(_N_PAIR)
    lanes = [slice(j * LANES, (j + 1) * LANES) for j in pairs]
    param = lambda ref: [ref[:, sl] for sl in lanes]

    def produce(c, slot):
        row0 = _aligned(c * BLK, BLK)
        rows = pl.ds(row0, BLK)
        prev8 = pl.ds(_aligned(max(row0 - 8, 0) if isinstance(row0, int)
                               else jnp.maximum(row0 - 8, 0), 8), 8)
        has_prev = jnp.where(c > 0, 1.0, 0.0).astype(F32)

        def shifted(ref, mu_ref, sl):
            p = ref[0, rows, sl]
            last = ref[0, prev8, sl][7:8, :] * has_prev
            prev = jnp.where(first_row, last, pltpu.roll(p, 1, 0))
            return p + (prev - p) * mu_ref[:, sl]

        xr = [shifted(pr_ref, mur_ref, sl) for sl in lanes]
        xk = [shifted(pk_ref, muk_ref, sl) for sl in lanes]
        xv = [shifted(pv_ref, muv_ref, sl) for sl in lanes]
        xwa = shifted(pwa_ref, muwa_ref, slice(None))
        xg = shifted(pg_ref, mug_ref, slice(None))
        yield

        lora_all = _dot(bf(jnp.where(head0, jnp.tanh(xwa), xwa)), wwa_ref[...])
        lora = [lora_all[:, 2 * LANES * j:2 * LANES * (j + 1)] for j in pairs]
        gate_all = _dot(bf(jax.nn.sigmoid(xg)), g2_ref[...])
        gate = [gate_all[:, sl] for sl in lanes]
        yield

        def decay_log(lo, w0):
            zw = -(w0 + lo[:, :LANES])
            softplus = jnp.maximum(zw, 0.0) + jnp.log(1.0 + jnp.exp(-jnp.abs(zw)))
            return -jnp.exp(-softplus - 0.5)
        log_decay = each(decay_log, lora, param(w0_ref))
        a_sig = each(lambda lo, a0: jax.nn.sigmoid(a0 + lo[:, LANES:]), lora, param(a0_ref))

        def unit_key(k, k_k):
            kkv = k * k_k
            return kkv / jnp.maximum(jnp.sqrt(_head_sum(kkv * kkv, head0)), 1e-12)
        kk = each(unit_key, xk, param(kk_ref))
        k2 = each(lambda k, a, k_a: k * (1.0 + (a - 1.0) * k_a), xk, a_sig, param(ka_ref))
        b_vec = each(lambda u, a: u * a, kk, a_sig)
        yield

        def cumulative(ld):
            ld_hi = bf(ld)
            rem = ld - ld_hi.astype(F32)
            ld_mid = bf(rem)
            ld_lo = bf(rem - ld_mid.astype(F32))
            cum = _dot(tril_incl, jnp.concatenate([ld_hi, ld_mid, ld_lo], axis=1))
            return cum[:, :LANES] + cum[:, LANES:2 * LANES] + cum[:, 2 * LANES:]
        lw = each(cumulative, log_decay)
        yield
        lw_last = [v[BLK - 1:BLK, :] for v in lw]
        w_inv = each(lambda v: jnp.exp(-v), lw)
        w_tail = each(lambda v, last: jnp.exp(last - v), lw, lw_last)
        w_all = each(jnp.exp, lw_last)

        at = each(lambda u, v, ld: -u * jnp.exp(v - ld), kk, lw, log_decay)
        rt = each(lambda r, v: r * jnp.exp(v), xr, lw)
        bt = each(lambda b, w: b * w, b_vec, w_inv)
        kt = each(lambda k, w: k * w, k2, w_inv)
        bh = each(lambda b, w: b * w, b_vec, w_tail)
        kh = each(lambda k, w: k * w, k2, w_tail)
        yield

        def pair_products(a, r, b, k):
            lhs = jnp.concatenate([jnp.where(head0, a, 0.0), jnp.where(head0, r, 0.0),
                                   jnp.where(head0, 0.0, a), jnp.where(head0, 0.0, r)], axis=0)
            return _nt_dot(bf(lhs), bf(jnp.concatenate([b, k], axis=0)))
        pp = each(pair_products, at, rt, bt, kt)
        yield

        def n_matrix(q):
            aab0 = jnp.where(strict, q[0:BLK, :BLK], 0.0)
            aab1 = jnp.where(strict, q[2 * BLK:3 * BLK, :BLK], 0.0)
            return bf(jnp.concatenate([jnp.concatenate([aab0, zeros_blk], axis=1),
                                       jnp.concatenate([zeros_blk, aab1], axis=1)], axis=0))

        def x_init(q, a, v):
            aak = jnp.concatenate([jnp.where(strict, q[0:BLK, BLK:], 0.0),
                                   jnp.where(strict, q[2 * BLK:3 * BLK, BLK:], 0.0)], axis=0)
            akv = _dot(bf(aak), bf(v))
            return jnp.concatenate([_stack_heads(a, head0), jnp.where(own_head, akv, 0.0)], axis=1)

        def read_matrices(q):
            arb = jnp.concatenate([jnp.where(incl, q[BLK:2 * BLK, :BLK], 0.0),
                                   jnp.where(incl, q[3 * BLK:, :BLK], 0.0)], axis=1)
            ark = jnp.concatenate([jnp.where(incl, q[BLK:2 * BLK, BLK:], 0.0),
                                   jnp.where(incl, q[3 * BLK:, BLK:], 0.0)], axis=1)
            return bf(arb), bf(ark)

        for j in pairs:
            slot["n"][j] = n_matrix(pp[j])
            slot["x"][j] = x_init(pp[j], at[j], xv[j])
            slot["arb"][j], slot["ark"][j] = read_matrices(pp[j])
        yield
        r_k = param(rk_ref)
        for j in pairs:
            slot["rt"][j] = rt[j]
            slot["bhs"][j] = bf(_stack_heads(bh[j], head0))
            slot["khs"][j] = bf(_stack_heads(kh[j], head0))
            slot["vs"][j] = bf(_stack_heads(xv[j], head0))
            slot["wall"][j] = jnp.broadcast_to(w_all[j], (8, LANES))
            slot["bonus"][j] = _head_sum(xr[j] * k2[j] * r_k[j], head0) * xv[j]
            slot["gate"][j] = gate[j]

    def consume(c, slot):
        n_chunk = _N_PAIR
        n_bf = [slot["n"][j] for j in range(n_chunk)]
        x = [slot["x"][j] for j in range(n_chunk)]
        span = 1
        while span < BLK:
            x = each(lambda xx, nn: xx + _dot(nn, bf(xx)), x, n_bf)
            span *= 2
            if span < BLK:
                n_bf = each(lambda nn: bf(_dot(nn, nn)), n_bf)
            yield
        x_bf = each(bf, x)
        v_s = [slot["vs"][j] for j in range(n_chunk)]

        def read_side(j, xb, vs):
            z = _dot(slot["arb"][j], xb)
            return slot["rt"][j] + z[:, :LANES], z[:, LANES:] + _dot(slot["ark"][j], vs)
        r_eff_y0 = each(read_side, range(n_chunk), x_bf, v_s)
        yield

        def state_side(j, xb, vs):
            d1 = _tn_dot(slot["bhs"][j], xb)
            d2 = _tn_dot(slot["khs"][j], vs)
            w_all = slot["wall"][j][0:1, :]
            return jnp.where(row == col, w_all, 0.0) + d1[:, :LANES], d1[:, LANES:] + d2
        m_eff_g_eff = each(state_side, range(n_chunk), x_bf, v_s)
        rm_lhs = each(lambda ry, mg: bf(jnp.concatenate([ry[0], mg[0]], axis=0)),
                      r_eff_y0, m_eff_g_eff)
        yield

        rm = each(lambda j, lhs: _dot(lhs, bf(h_ref[j])), pairs, rm_lhs)
        y = each(lambda v, ry: v[:BLK] + ry[1], rm, r_eff_y0)
        for j in pairs:
            h_ref[j] = rm[j][BLK:] + m_eff_g_eff[j][1]
        yield
        mean = each(lambda v: _head_sum(v, head0) * (1.0 / HEAD_DIM), y)
        yc = each(lambda v, m: v - m, y, mean)
        var = each(lambda v: _head_sum(v * v, head0) * (1.0 / HEAD_DIM), yc)
        rows = pl.ds(_aligned(c * BLK, BLK), BLK)
        lnx_g, lnx_b = param(lng_ref), param(lnb_ref)
        for j in pairs:
            yn = yc[j] * lax.rsqrt(var[j] + LNX_EPS) * lnx_g[j] + lnx_b[j]
            o_ref[0, rows, lanes[j]] = (yn + slot["bonus"][j]) * slot["gate"][j]

    def alternate(*stages):
        live = list(stages)
        while live:
            for gen in list(live):
                try:
                    next(gen)
                except StopIteration:
                    live.remove(gen)

    names = ("n", "x", "arb", "ark", "rt", "bhs", "khs", "vs", "wall", "bonus", "gate")
    slot_a = dict(zip(names, slots[:len(names)]))
    slot_b = dict(zip(names, slots[len(names):]))
    assert n_chunks % 2 == 0

    alternate(produce(0, slot_a))

    def chunk_pair(i, carry):
        c = 2 * i
        alternate(consume(c, slot_a), produce(c + 1, slot_b))
        alternate(consume(c + 1, slot_b), produce(jnp.minimum(c + 2, n_chunks - 1), slot_a))
        return carry

    lax.fori_loop(0, n_chunks // 2, chunk_pair, 0)


def _wkv_slot_shapes():
    g = _N_PAIR
    return [
        pltpu.VMEM((g, 2 * BLK, 2 * BLK), BF16),
        pltpu.VMEM((g, 2 * BLK, 2 * LANES), F32),
        pltpu.VMEM((g, BLK, 2 * BLK), BF16),
        pltpu.VMEM((g, BLK, 2 * BLK), BF16),
        pltpu.VMEM((g, BLK, LANES), F32),
        pltpu.VMEM((g, 2 * BLK, LANES), BF16),
        pltpu.VMEM((g, 2 * BLK, LANES), BF16),
        pltpu.VMEM((g, 2 * BLK, LANES), BF16),
        pltpu.VMEM((g, 8, LANES), F32),
        pltpu.VMEM((g, BLK, LANES), F32),
        pltpu.VMEM((g, BLK, LANES), F32),
    ]


def _rwkv(proj, mu, w0, a0, k_k, k_a, r_k, lnx_g, lnx_b, wwa, g2_bf16):
    b, s, _ = proj.shape
    np_ = _N_PAIR
    w = RWKV_WIDTH
    wide = lambda off: pl.BlockSpec((1, s, w), lambda i: (i, 0, off))
    lane = lambda off: pl.BlockSpec((1, s, LANES), lambda i: (i, 0, off))
    vec_w = lambda off: pl.BlockSpec((1, w), lambda i: (0, off))
    vec_l = lambda off: pl.BlockSpec((1, LANES), lambda i: (0, off))
    full = lambda shape: pl.BlockSpec(shape, lambda i: (0,) * len(shape))
    rw0 = _RW0 * LANES // w
    return pl.pallas_call(
        _wkv_kernel,
        grid=(b,),
        in_specs=[
            wide(rw0), wide(rw0 + 1), wide(rw0 + 2),
            lane(_RW0 + 3 * np_), lane(_RW0 + 3 * np_ + 1),
            vec_w(0), vec_w(1), vec_w(2), vec_l(3 * np_), vec_l(3 * np_ + 1),
            vec_w(0), vec_w(0), vec_w(0), vec_w(0), vec_w(0), vec_w(0), vec_w(0),
            full((LANES, np_ * 2 * LANES)),
            full((GATE_LORA, w)),
        ],
        out_specs=pl.BlockSpec((1, s, w), lambda i: (i, 0, 0)),
        out_shape=jax.ShapeDtypeStruct((b, s, RWKV_WIDTH), F32),
        scratch_shapes=[pltpu.VMEM((np_, LANES, LANES), F32)] + 2 * _wkv_slot_shapes(),
        compiler_params=pltpu.CompilerParams(
            dimension_semantics=("parallel",), vmem_limit_bytes=VMEM_LIMIT),
        name="rwkv7",
    )(proj, proj, proj, proj, proj, mu, mu, mu, mu, mu,
      w0, a0, k_k, k_a, r_k, lnx_g, lnx_b, wwa, g2_bf16)


def _out_ffn_kernel(x_ref, attn_ref, rwkv_ref, woa_ref, wor_ref, g2_ref,
                    wg_ref, wu_ref, wd_ref, o_ref):
    x1 = (x_ref[...] + _dot(attn_ref[...].astype(BF16), woa_ref[...])
          + _dot(rwkv_ref[...].astype(BF16), wor_ref[...]))
    ms = jnp.mean(x1 * x1, axis=-1, keepdims=True)
    xn = (x1 * lax.rsqrt(ms + NORM_EPS) * g2_ref[...]).astype(BF16)
    gate = _dot(xn, wg_ref[...])
    up = _dot(xn, wu_ref[...])
    hidden = (gate * jax.nn.sigmoid(gate) * up).astype(BF16)
    o_ref[...] = x1 + _dot(hidden, wd_ref[...])


def _out_ffn(x2d, attn2d, rwkv2d, woa, wor, g2, wg, wu, wd, tm):
    t, d = x2d.shape
    f = wg.shape[1]
    tok = lambda w: pl.BlockSpec((tm, w), lambda i: (i, 0))
    res = lambda shape: pl.BlockSpec(shape, lambda i: (0, 0), pipeline_mode=pl.Buffered(1))
    return pl.pallas_call(
        _out_ffn_kernel,
        grid=(t // tm,),
        in_specs=[
            tok(d), tok(ATTN_WIDTH), tok(RWKV_WIDTH),
            res((ATTN_WIDTH, d)), res((RWKV_WIDTH, d)),
            pl.BlockSpec((1, d), lambda i: (0, 0)),
            res((d, f)), res((d, f)), res((f, d)),
        ],
        out_specs=tok(d),
        out_shape=jax.ShapeDtypeStruct((t, d), F32),
        compiler_params=pltpu.CompilerParams(
            dimension_semantics=("parallel",), vmem_limit_bytes=VMEM_LIMIT),
        name="out_ffn",
    )(x2d, attn2d, rwkv2d, woa, wor, g2, wg, wu, wd)


def _layer(x, norm1_g, w_in, q_norm_g, k_norm_g, attn_out_g, rwkv_mu, w0, w2, a0, a2, g2,
           k_k, k_a, r_k, lnx_g, lnx_b, w_out, norm2_g, w_gate, w_up, w_down):
    b, s, d = x.shape
    assert s % (BLK * max(dil for _, dil in DILATED_PATTERNS)) == 0
    assert all(window // dil == BLK for window, dil in DILATED_PATTERNS)
    t = b * s
    tm = 512
    assert t % tm == 0
    row = lambda v: v.reshape(1, -1).astype(F32)

    x2d = x.reshape(t, d)
    proj = _in_proj(x2d, row(norm1_g), w_in.astype(BF16), tm).reshape(b, s, -1)

    slopes = jnp.exp2(-8.0 * jnp.arange(1, ATTN_HEADS + 1, dtype=F32) / ATTN_HEADS)
    pair = lambda v: jnp.tile(v.reshape(1, HEAD_DIM), (1, LANES // HEAD_DIM)).astype(F32)
    attn = _attention(proj, slopes, pair(q_norm_g), pair(k_norm_g), row(attn_out_g))

    w2p = w2.reshape(DECAY_LORA, _N_PAIR, LANES).transpose(1, 0, 2)
    a2p = a2.reshape(ICLR_LORA, _N_PAIR, LANES).transpose(1, 0, 2)
    zero = jnp.zeros_like(w2p)
    wwa = jnp.concatenate([jnp.concatenate([w2p, zero], axis=2),
                           jnp.concatenate([jnp.zeros_like(a2p), a2p], axis=2)], axis=1)
    wwa = wwa.transpose(1, 0, 2).reshape(LANES, _N_PAIR * 2 * LANES).astype(BF16)
    rwkv = _rwkv(proj, row(rwkv_mu), row(w0), row(a0), row(k_k), row(k_a), row(r_k),
                 row(lnx_g), row(lnx_b), wwa, g2.astype(BF16))

    out = _out_ffn(x2d, attn.reshape(t, ATTN_WIDTH), rwkv.reshape(t, RWKV_WIDTH),
                   w_out[:ATTN_WIDTH].astype(BF16), w_out[ATTN_WIDTH:].astype(BF16),
                   row(norm2_g), w_gate.astype(BF16), w_up.astype(BF16), w_down.astype(BF16), tm)
    return out.reshape(b, s, d)


def kernel(x, norm1_g, w_in, q_norm_g, k_norm_g, attn_out_g, rwkv_mu, w0, w2, a0, a2, g2,
           k_k, k_a, r_k, lnx_g, lnx_b, w_out, norm2_g, w_gate, w_up, w_down):
    h = x
    for layer in range(norm1_g.shape[0]):
        h = _layer(h, norm1_g[layer], w_in[layer], q_norm_g[layer], k_norm_g[layer],
                   attn_out_g[layer], rwkv_mu[layer], w0[layer], w2[layer], a0[layer],
                   a2[layer], g2[layer], k_k[layer], k_a[layer], r_k[layer], lnx_g[layer],
                   lnx_b[layer], w_out[layer], norm2_g[layer], w_gate[layer], w_up[layer],
                   w_down[layer])
    return h
```

```python
import functools

import jax
import jax.numpy as jnp
from jax import lax
from jax.experimental import pallas as pl
from jax.experimental.pallas import tpu as pltpu

F32 = jnp.float32
BF16 = jnp.bfloat16

HEAD_DIM = 64
LANES = 128
ATTN_HEADS = 8
RWKV_HEADS = 8
ATTN_WIDTH = ATTN_HEADS * HEAD_DIM
RWKV_WIDTH = RWKV_HEADS * HEAD_DIM
DECAY_LORA = 64
ICLR_LORA = 64
GATE_LORA = 128
DILATED_PATTERNS = ((128, 1), (512, 4), (2048, 16))
BLK = 128
ATTN_GROUP = 8
ROW_TILE = 512
NORM_EPS = 1e-6
LNX_EPS = 64e-5
MASKED = -1e30
LOG2_E = 1.4426950408889634
VMEM_LIMIT = 56 * 1024 * 1024

_QB, _KB, _VB = 0, ATTN_WIDTH // LANES, 2 * ATTN_WIDTH // LANES
_RW0 = 3 * ATTN_WIDTH // LANES
_N_PAIR = RWKV_WIDTH // LANES


def _nt_dot(a, b):
    return lax.dot_general(a, b, (((1,), (1,)), ((), ())), preferred_element_type=F32)


def _tn_dot(a, b):
    return lax.dot_general(a, b, (((0,), (0,)), ((), ())), preferred_element_type=F32)


def _dot(a, b):
    return jnp.dot(a, b, preferred_element_type=F32)


def _aligned(index, multiple):
    if isinstance(index, int):
        return index
    return pl.multiple_of(index, multiple)


def _head0_lanes():
    return lax.broadcasted_iota(jnp.int32, (1, LANES), 1) < HEAD_DIM


def _head_ones():
    i = lax.broadcasted_iota(jnp.int32, (2 * LANES, LANES), 0)
    j = lax.broadcasted_iota(jnp.int32, (2 * LANES, LANES), 1)
    same = ((i % LANES) < HEAD_DIM) == (j < HEAD_DIM)
    return jnp.where(same, 1.0, 0.0).astype(BF16)


def _head_sum_mxu(x, head_ones):
    hi = x.astype(BF16)
    lo = (x - hi.astype(F32)).astype(BF16)
    return _dot(jnp.concatenate([hi, lo], axis=1), head_ones)


def _head_sum(x, head0):
    s0 = jnp.sum(jnp.where(head0, x, 0.0), axis=-1, keepdims=True)
    s1 = jnp.sum(jnp.where(head0, 0.0, x), axis=-1, keepdims=True)
    return jnp.where(head0, s0, s1)


def _stack_heads(x, head0):
    return jnp.concatenate([jnp.where(head0, x, 0.0), jnp.where(head0, 0.0, x)], axis=0)


def _in_proj_kernel(x_ref, g_ref, w_ref, o_ref):
    x = x_ref[...]
    ms = jnp.mean(x * x, axis=-1, keepdims=True)
    xn = x * lax.rsqrt(ms + NORM_EPS) * g_ref[...]
    o_ref[...] = _dot(xn.astype(BF16), w_ref[...])


def _in_proj(x2d, g, w_bf16, tm):
    t, d = x2d.shape
    n = w_bf16.shape[1]
    return pl.pallas_call(
        _in_proj_kernel,
        grid=(t // tm,),
        in_specs=[
            pl.BlockSpec((tm, d), lambda i: (i, 0)),
            pl.BlockSpec((1, d), lambda i: (0, 0)),
            pl.BlockSpec((d, n), lambda i: (0, 0), pipeline_mode=pl.Buffered(1)),
        ],
        out_specs=pl.BlockSpec((tm, n), lambda i: (i, 0)),
        out_shape=jax.ShapeDtypeStruct((t, n), F32),
        compiler_params=pltpu.CompilerParams(
            dimension_semantics=("parallel",), vmem_limit_bytes=VMEM_LIMIT),
        name="in_proj",
    )(x2d, g, w_bf16)


def _attn_kernel(slope_ref, q_ref, k_ref, v_ref, gq_ref, gk_ref, go_ref, o_ref,
                 qn_ref, kn_ref, m_ref, l_ref, acc_ref, bias_ref):
    seq = q_ref.shape[1]
    hp = pl.program_id(1)
    head0 = _head0_lanes()
    head_ones = _head_ones()

    def prep(i, carry):
        rows = pl.ds(pl.multiple_of(i * ROW_TILE, ROW_TILE), ROW_TILE)
        q = q_ref[0, rows, :]
        k = k_ref[0, rows, :]
        qn_ref[rows, :] = (q * lax.rsqrt(_head_sum_mxu(q * q, head_ones) * (1.0 / HEAD_DIM)
                                         + NORM_EPS) * gq_ref[...] * (HEAD_DIM ** -0.5 * LOG2_E))
        kn_ref[rows, :] = (k * lax.rsqrt(_head_sum_mxu(k * k, head_ones) * (1.0 / HEAD_DIM)
                                         + NORM_EPS) * gk_ref[...])
        return carry

    lax.fori_loop(0, seq // ROW_TILE, prep, 0)

    qi = lax.broadcasted_iota(jnp.int32, (2 * BLK, 2 * BLK), 0)
    ci = lax.broadcasted_iota(jnp.int32, (2 * BLK, 2 * BLK), 1)
    back = jnp.where(qi < BLK, qi, qi - BLK) + BLK - ci
    slope = jnp.where(qi < BLK, slope_ref[2 * hp], slope_ref[2 * hp + 1]) * LOG2_E
    for p, (window, dil) in enumerate(DILATED_PATTERNS):
        steps = window // dil
        valid = (back >= 0) & (back <= steps)
        bias_ref[p] = jnp.where(valid, -slope * (back.astype(F32) * dil), MASKED)
    ones_keys = jnp.ones((2 * BLK, LANES), BF16)

    def band_blocks(p, dil, starts, has_prevs):
        each = lambda f, *lists: [f(*args) for args in zip(*lists)]
        bf = lambda v: v.astype(BF16)
        ds = lambda start: pl.ds(start, BLK, stride=dil) if dil > 1 else pl.ds(start, BLK)
        rowss = [ds(start) for start in starts]

        def keys_values(start, rows, has_prev):
            kb = kn_ref[rows, :]
            vb = v_ref[0, rows, :]
            if has_prev:
                prows = ds(start - BLK * dil)
                kb = jnp.concatenate([kn_ref[prows, :], kb], axis=0)
                vb = jnp.concatenate([v_ref[0, prows, :], vb], axis=0)
            return bf(kb), jnp.concatenate([bf(vb), ones_keys[:vb.shape[0]]], axis=1)
        kv = each(keys_values, starts, rowss, has_prevs)
        qs = each(lambda rows: bf(_stack_heads(qn_ref[rows, :], head0)), rowss)
        s = each(lambda q, kvb, has_prev:
                 _nt_dot(q, kvb[0]) + (bias_ref[p] if has_prev else bias_ref[p, :, BLK:]),
                 qs, kv, has_prevs)
        m_blk = each(lambda v: jnp.max(v, axis=-1, keepdims=True), s)
        pr = each(lambda v, m: bf(jnp.exp2(v - m)), s, m_blk)
        pv = each(lambda v, kvb: _dot(v, kvb[1]), pr, kv)

        def keep(rows, m_b, pv_b):
            m_ref[p, rows, :] = jnp.where(head0, m_b[:BLK], m_b[BLK:])
            acc_ref[p, rows, :] = jnp.where(head0, pv_b[:BLK, :LANES], pv_b[BLK:, :LANES])
            l_ref[p, rows, :] = jnp.where(head0, pv_b[:BLK, LANES:], pv_b[BLK:, LANES:])
        each(keep, rowss, m_blk, pv)

    for p, (window, dil) in enumerate(DILATED_PATTERNS):
        n_blk = seq // (BLK * dil)
        span = BLK * dil
        if n_blk >= ATTN_GROUP:
            assert n_blk % ATTN_GROUP == 0
            groups = n_blk // ATTN_GROUP

            def residue(r, carry, p=p, dil=dil, span=span, groups=groups):
                for grp in range(groups):
                    blocks = range(grp * ATTN_GROUP, (grp + 1) * ATTN_GROUP)
                    band_blocks(p, dil, [r + n * span for n in blocks], [n > 0 for n in blocks])
                return carry
        else:
            assert ATTN_GROUP % n_blk == 0 and dil % (ATTN_GROUP // n_blk) == 0
            per_group = ATTN_GROUP // n_blk

            def residue(i, carry, p=p, dil=dil, span=span, n_blk=n_blk, per_group=per_group):
                classes = [i * per_group + g for g in range(per_group)]
                band_blocks(p, dil, [r + n * span for r in classes for n in range(n_blk)],
                            [n > 0 for r in classes for n in range(n_blk)])
                return carry
        n_steps = dil if n_blk >= ATTN_GROUP else dil // (ATTN_GROUP // n_blk)
        if n_steps == 1:
            residue(0, 0)
        else:
            lax.fori_loop(0, n_steps, residue, 0)

    def finish(i, carry):
        rows = pl.ds(pl.multiple_of(i * ROW_TILE, ROW_TILE), ROW_TILE)
        n_pat = len(DILATED_PATTERNS)
        m_p = [m_ref[p, rows, :] for p in range(n_pat)]
        m_all = functools.reduce(jnp.maximum, m_p)
        scale = [jnp.exp2(m - m_all) for m in m_p]
        num = sum(sc * acc_ref[p, rows, :] for p, sc in enumerate(scale))
        den = sum(sc * l_ref[p, rows, :] for p, sc in enumerate(scale))
        o = num / den
        ms = _head_sum_mxu(o * o, head_ones) * (1.0 / HEAD_DIM)
        o_ref[0, rows, :] = o * lax.rsqrt(ms + NORM_EPS) * go_ref[...]
        return carry

    lax.fori_loop(0, seq // ROW_TILE, finish, 0)


def _attention(proj, slopes, gq, gk, go):
    b, s, _ = proj.shape
    blk = lambda off: pl.BlockSpec((1, s, LANES), lambda i, j: (i, 0, off + j))
    vec = lambda: pl.BlockSpec((1, LANES), lambda i, j: (0, 0))
    return pl.pallas_call(
        _attn_kernel,
        grid=(b, _N_PAIR),
        in_specs=[
            pl.BlockSpec(memory_space=pltpu.SMEM),
            blk(_QB), blk(_KB), blk(_VB),
            vec(), vec(),
            pl.BlockSpec((1, LANES), lambda i, j: (0, j)),
        ],
        out_specs=pl.BlockSpec((1, s, LANES), lambda i, j: (i, 0, j)),
        out_shape=jax.ShapeDtypeStruct((b, s, ATTN_WIDTH), F32),
        scratch_shapes=[
            pltpu.VMEM((s, LANES), F32),
            pltpu.VMEM((s, LANES), F32),
            pltpu.VMEM((len(DILATED_PATTERNS), s, LANES), F32),
            pltpu.VMEM((len(DILATED_PATTERNS), s, LANES), F32),
            pltpu.VMEM((len(DILATED_PATTERNS), s, LANES), F32),
            pltpu.VMEM((len(DILATED_PATTERNS), 2 * BLK, 2 * BLK), F32),
        ],
        compiler_params=pltpu.CompilerParams(
            dimension_semantics=("parallel", "parallel"), vmem_limit_bytes=VMEM_LIMIT),
        name="dilated_attn",
    )(slopes, proj, proj, proj, gq, gk, go)


def _wkv_kernel(pr_ref, pk_ref, pv_ref, pwa_ref, pg_ref,
                mur_ref, muk_ref, muv_ref, muwa_ref, mug_ref,
                w0_ref, a0_ref, kk_ref, ka_ref, rk_ref, lng_ref, lnb_ref,
                wwa_ref, g2_ref, o_ref, h_ref, *slots):
    seq = pr_ref.shape[1]
    n_chunks = seq // BLK
    head0 = _head0_lanes()
    row = lax.broadcasted_iota(jnp.int32, (BLK, BLK), 0)
    col = lax.broadcasted_iota(jnp.int32, (BLK, BLK), 1)
    strict = col < row
    incl = col <= row
    tril_incl = jnp.where(incl, 1.0, 0.0).astype(BF16)
    first_row = lax.broadcasted_iota(jnp.int32, (BLK, LANES), 0) == 0
    srow = lax.broadcasted_iota(jnp.int32, (2 * BLK, LANES), 0) < BLK
    slane = lax.broadcasted_iota(jnp.int32, (2 * BLK, LANES), 1) < HEAD_DIM
    own_head = srow == slane
    zeros_blk = jnp.zeros((BLK, BLK), F32)
    zeros_bf = jnp.zeros((2 * BLK, 2 * BLK), BF16)
    eye_bf = jnp.where(lax.broadcasted_iota(jnp.int32, (2 * BLK, 2 * BLK), 0)
                       == lax.broadcasted_iota(jnp.int32, (2 * BLK, 2 * BLK), 1),
                       1.0, 0.0).astype(BF16)

    h_ref[...] = jnp.zeros(h_ref.shape, F32)

    each = lambda f, *lists: [f(*args) for args in zip(*lists)]
    bf = lambda v: v.astype(BF16)
    pairs = range(_N_PAIR)
    lanes = [slice(j * LANES, (j + 1) * LANES) for j in pairs]
    param = lambda ref: [ref[:, sl] for sl in lanes]

    def produce(c, slot):
        row0 = _aligned(c * BLK, BLK)
        rows = pl.ds(row0, BLK)
        prev8 = pl.ds(_aligned(max(row0 - 8, 0) if isinstance(row0, int)
                               else jnp.maximum(row0 - 8, 0), 8), 8)
        has_prev = jnp.where(c > 0, 1.0, 0.0).astype(F32)

        def shifted(ref, mu_ref, sl):
            p = ref[0, rows, sl]
            last = ref[0, prev8, sl][7:8, :] * has_prev
            prev = jnp.where(first_row, last, pltpu.roll(p, 1, 0))
            return p + (prev - p) * mu_ref[:, sl]

        xr = [shifted(pr_ref, mur_ref, sl) for sl in lanes]
        xk = [shifted(pk_ref, muk_ref, sl) for sl in lanes]
        xv = [shifted(pv_ref, muv_ref, sl) for sl in lanes]
        xwa = shifted(pwa_ref, muwa_ref, slice(None))
        xg = shifted(pg_ref, mug_ref, slice(None))
        yield

        lora_all = _dot(bf(jnp.where(head0, jnp.tanh(xwa), xwa)), wwa_ref[...])
        lora = [lora_all[:, 2 * LANES * j:2 * LANES * (j + 1)] for j in pairs]
        gate_all = _dot(bf(jax.nn.sigmoid(xg)), g2_ref[...])
        gate = [gate_all[:, sl] for sl in lanes]
        yield

        def decay_log(lo, w0):
            zw = -(w0 + lo[:, :LANES])
            softplus = jnp.maximum(zw, 0.0) + jnp.log(1.0 + jnp.exp(-jnp.abs(zw)))
            return -jnp.exp(-softplus - 0.5)
        log_decay = each(decay_log, lora, param(w0_ref))
        a_sig = each(lambda lo, a0: jax.nn.sigmoid(a0 + lo[:, LANES:]), lora, param(a0_ref))

        def unit_key(k, k_k):
            kkv = k * k_k
            return kkv / jnp.maximum(jnp.sqrt(_head_sum(kkv * kkv, head0)), 1e-12)
        kk = each(unit_key, xk, param(kk_ref))
        k2 = each(lambda k, a, k_a: k * (1.0 + (a - 1.0) * k_a), xk, a_sig, param(ka_ref))
        b_vec = each(lambda u, a: u * a, kk, a_sig)
        yield

        def cumulative(ld):
            ld_hi = bf(ld)
            rem = ld - ld_hi.astype(F32)
            ld_mid = bf(rem)
            ld_lo = bf(rem - ld_mid.astype(F32))
            cum = _dot(tril_incl, jnp.concatenate([ld_hi, ld_mid, ld_lo], axis=1))
            return cum[:, :LANES] + cum[:, LANES:2 * LANES] + cum[:, 2 * LANES:]
        lw = each(cumulative, log_decay)
        yield
        lw_last = [v[BLK - 1:BLK, :] for v in lw]
        w_inv = each(lambda v: jnp.exp(-v), lw)
        w_tail = each(lambda v, last: jnp.exp(last - v), lw, lw_last)
        w_all = each(jnp.exp, lw_last)

        at = each(lambda u, v, ld: -u * jnp.exp(v - ld), kk, lw, log_decay)
        rt = each(lambda r, v: r * jnp.exp(v), xr, lw)
        bt = each(lambda b, w: b * w, b_vec, w_inv)
        kt = each(lambda k, w: k * w, k2, w_inv)
        bh = each(lambda b, w: b * w, b_vec, w_tail)
        kh = each(lambda k, w: k * w, k2, w_tail)
        yield

        def pair_products(a, r, b, k):
            lhs = jnp.concatenate([jnp.where(head0, a, 0.0), jnp.where(head0, r, 0.0),
                                   jnp.where(head0, 0.0, a), jnp.where(head0, 0.0, r)], axis=0)
            return _nt_dot(bf(lhs), bf(jnp.concatenate([b, k], axis=0)))
        pp = each(pair_products, at, rt, bt, kt)
        yield

        def n_matrix(q):
            aab0 = jnp.where(strict, q[0:BLK, :BLK], 0.0)
            aab1 = jnp.where(strict, q[2 * BLK:3 * BLK, :BLK], 0.0)
            return bf(jnp.concatenate([jnp.concatenate([aab0, zeros_blk], axis=1),
                                       jnp.concatenate([zeros_blk, aab1], axis=1)], axis=0))

        def x_init(q, a, v):
            aak = jnp.concatenate([jnp.where(strict, q[0:BLK, BLK:], 0.0),
                                   jnp.where(strict, q[2 * BLK:3 * BLK, BLK:], 0.0)], axis=0)
            akv = _dot(bf(aak), bf(v))
            return jnp.concatenate([_stack_heads(a, head0), jnp.where(own_head, akv, 0.0)], axis=1)

        def read_matrices(q):
            arb = jnp.concatenate([jnp.where(incl, q[BLK:2 * BLK, :BLK], 0.0),
                                   jnp.where(incl, q[3 * BLK:, :BLK], 0.0)], axis=1)
            ark = jnp.concatenate([jnp.where(incl, q[BLK:2 * BLK, BLK:], 0.0),
                                   jnp.where(incl, q[3 * BLK:, BLK:], 0.0)], axis=1)
            return bf(arb), bf(ark)

        for j in pairs:
            slot["n"][j] = n_matrix(pp[j])
            slot["x"][j] = bf(x_init(pp[j], at[j], xv[j]))
            slot["arb"][j], slot["ark"][j] = read_matrices(pp[j])
        yield
        r_k = param(rk_ref)
        for j in pairs:
            slot["rt"][j] = rt[j]
            slot["bhs"][j] = bf(_stack_heads(bh[j], head0))
            slot["khs"][j] = bf(_stack_heads(kh[j], head0))
            slot["vs"][j] = bf(_stack_heads(xv[j], head0))
            slot["wall"][j] = jnp.broadcast_to(w_all[j], (8, LANES))
            slot["bonus"][j] = _head_sum(xr[j] * k2[j] * r_k[j], head0) * xv[j]
            slot["gate"][j] = gate[j]

    def consume(c, slot):
        n_chunk = _N_PAIR
        n_bf = [slot["n"][j] for j in range(n_chunk)]
        x_bf = [slot["x"][j] for j in range(n_chunk)]

        def live_rows(m, skip):
            if not skip:
                return m
            return jnp.concatenate([m[skip:BLK], m[BLK + skip:]], axis=0)

        def merge_rows(old, new, skip):
            if not skip:
                return new
            live = BLK - skip
            return jnp.concatenate([old[:skip], new[:live], old[BLK:BLK + skip], new[live:]],
                                   axis=0)

        span = 1
        while span < BLK:
            skip = span if span >= 16 else 0
            x_bf = each(lambda xx, nn: merge_rows(
                xx, bf(_dot(live_rows(nn + eye_bf, skip), xx)), skip), x_bf, n_bf)
            span *= 2
            if span < BLK:
                skip = span if span >= 16 else 0
                n_bf = each(lambda nn: merge_rows(
                    zeros_bf, bf(_dot(live_rows(nn, skip), nn)), skip), n_bf)
            yield
        v_s = [slot["vs"][j] for j in range(n_chunk)]

        def read_side(j, xb, vs):
            z = _dot(slot["arb"][j], xb)
            return slot["rt"][j] + z[:, :LANES], z[:, LANES:] + _dot(slot["ark"][j], vs)
        r_eff_y0 = each(read_side, range(n_chunk), x_bf, v_s)
        yield

        def state_side(j, xb, vs):
            d1 = _tn_dot(slot["bhs"][j], xb)
            d2 = _tn_dot(slot["khs"][j], vs)
            w_all = slot["wall"][j][0:1, :]
            return jnp.where(row == col, w_all, 0.0) + d1[:, :LANES], d1[:, LANES:] + d2
        m_eff_g_eff = each(state_side, range(n_chunk), x_bf, v_s)
        rm_lhs = each(lambda ry, mg: bf(jnp.concatenate([ry[0], mg[0]], axis=0)),
                      r_eff_y0, m_eff_g_eff)
        yield

        rm = each(lambda j, lhs: _dot(lhs, bf(h_ref[j])), pairs, rm_lhs)
        y = each(lambda v, ry: v[:BLK] + ry[1], rm, r_eff_y0)
        for j in pairs:
            h_ref[j] = rm[j][BLK:] + m_eff_g_eff[j][1]
        yield
        mean = each(lambda v: _head_sum(v, head0) * (1.0 / HEAD_DIM), y)
        yc = each(lambda v, m: v - m, y, mean)
        var = each(lambda v: _head_sum(v * v, head0) * (1.0 / HEAD_DIM), yc)
        rows = pl.ds(_aligned(c * BLK, BLK), BLK)
        lnx_g, lnx_b = param(lng_ref), param(lnb_ref)
        for j in pairs:
            yn = yc[j] * lax.rsqrt(var[j] + LNX_EPS) * lnx_g[j] + lnx_b[j]
            o_ref[0, rows, lanes[j]] = (yn + slot["bonus"][j]) * slot["gate"][j]

    def alternate(*stages):
        live = list(stages)
        while live:
            for gen in list(live):
                try:
                    next(gen)
                except StopIteration:
                    live.remove(gen)

    names = ("n", "x", "arb", "ark", "rt", "bhs", "khs", "vs", "wall", "bonus", "gate")
    slot_a = dict(zip(names, slots[:len(names)]))
    slot_b = dict(zip(names, slots[len(names):]))
    assert n_chunks % 2 == 0

    alternate(produce(0, slot_a))

    def chunk_pair(i, carry):
        c = 2 * i
        alternate(consume(c, slot_a), produce(c + 1, slot_b))
        alternate(consume(c + 1, slot_b), produce(jnp.minimum(c + 2, n_chunks - 1), slot_a))
        return carry

    lax.fori_loop(0, n_chunks // 2, chunk_pair, 0)


def _wkv_slot_shapes():
    g = _N_PAIR
    return [
        pltpu.VMEM((g, 2 * BLK, 2 * BLK), BF16),
        pltpu.VMEM((g, 2 * BLK, 2 * LANES), BF16),
        pltpu.VMEM((g, BLK, 2 * BLK), BF16),
        pltpu.VMEM((g, BLK, 2 * BLK), BF16),
        pltpu.VMEM((g, BLK, LANES), F32),
        pltpu.VMEM((g, 2 * BLK, LANES), BF16),
        pltpu.VMEM((g, 2 * BLK, LANES), BF16),
        pltpu.VMEM((g, 2 * BLK, LANES), BF16),
        pltpu.VMEM((g, 8, LANES), F32),
        pltpu.VMEM((g, BLK, LANES), F32),
        pltpu.VMEM((g, BLK, LANES), F32),
    ]


def _rwkv(proj, mu, w0, a0, k_k, k_a, r_k, lnx_g, lnx_b, wwa, g2_bf16):
    b, s, _ = proj.shape
    np_ = _N_PAIR
    w = RWKV_WIDTH
    wide = lambda off: pl.BlockSpec((1, s, w), lambda i: (i, 0, off))
    lane = lambda off: pl.BlockSpec((1, s, LANES), lambda i: (i, 0, off))
    vec_w = lambda off: pl.BlockSpec((1, w), lambda i: (0, off))
    vec_l = lambda off: pl.BlockSpec((1, LANES), lambda i: (0, off))
    full = lambda shape: pl.BlockSpec(shape, lambda i: (0,) * len(shape))
    rw0 = _RW0 * LANES // w
    return pl.pallas_call(
        _wkv_kernel,
        grid=(b,),
        in_specs=[
            wide(rw0), wide(rw0 + 1), wide(rw0 + 2),
            lane(_RW0 + 3 * np_), lane(_RW0 + 3 * np_ + 1),
            vec_w(0), vec_w(1), vec_w(2), vec_l(3 * np_), vec_l(3 * np_ + 1),
            vec_w(0), vec_w(0), vec_w(0), vec_w(0), vec_w(0), vec_w(0), vec_w(0),
            full((LANES, np_ * 2 * LANES)),
            full((GATE_LORA, w)),
        ],
        out_specs=pl.BlockSpec((1, s, w), lambda i: (i, 0, 0)),
        out_shape=jax.ShapeDtypeStruct((b, s, RWKV_WIDTH), F32),
        scratch_shapes=[pltpu.VMEM((np_, LANES, LANES), F32)] + 2 * _wkv_slot_shapes(),
        compiler_params=pltpu.CompilerParams(
            dimension_semantics=("parallel",), vmem_limit_bytes=VMEM_LIMIT),
        name="rwkv7",
    )(proj, proj, proj, proj, proj, mu, mu, mu, mu, mu,
      w0, a0, k_k, k_a, r_k, lnx_g, lnx_b, wwa, g2_bf16)


def _out_ffn_kernel(x_ref, attn_ref, rwkv_ref, woa_ref, wor_ref, g2_ref,
                    wg_ref, wu_ref, wd_ref, o_ref):
    x1 = (x_ref[...] + _dot(attn_ref[...].astype(BF16), woa_ref[...])
          + _dot(rwkv_ref[...].astype(BF16), wor_ref[...]))
    ms = jnp.mean(x1 * x1, axis=-1, keepdims=True)
    xn = (x1 * lax.rsqrt(ms + NORM_EPS) * g2_ref[...]).astype(BF16)
    gate = _dot(xn, wg_ref[...])
    up = _dot(xn, wu_ref[...])
    hidden = (gate * jax.nn.sigmoid(gate) * up).astype(BF16)
    o_ref[...] = x1 + _dot(hidden, wd_ref[...])


def _out_ffn(x2d, attn2d, rwkv2d, woa, wor, g2, wg, wu, wd, tm):
    t, d = x2d.shape
    f = wg.shape[1]
    tok = lambda w: pl.BlockSpec((tm, w), lambda i: (i, 0))
    res = lambda shape: pl.BlockSpec(shape, lambda i: (0, 0), pipeline_mode=pl.Buffered(1))
    return pl.pallas_call(
        _out_ffn_kernel,
        grid=(t // tm,),
        in_specs=[
            tok(d), tok(ATTN_WIDTH), tok(RWKV_WIDTH),
            res((ATTN_WIDTH, d)), res((RWKV_WIDTH, d)),
            pl.BlockSpec((1, d), lambda i: (0, 0)),
            res((d, f)), res((d, f)), res((f, d)),
        ],
        out_specs=tok(d),
        out_shape=jax.ShapeDtypeStruct((t, d), F32),
        compiler_params=pltpu.CompilerParams(
            dimension_semantics=("parallel",), vmem_limit_bytes=VMEM_LIMIT),
        name="out_ffn",
    )(x2d, attn2d, rwkv2d, woa, wor, g2, wg, wu, wd)


def _layer(x, norm1_g, w_in, q_norm_g, k_norm_g, attn_out_g, rwkv_mu, w0, w2, a0, a2, g2,
           k_k, k_a, r_k, lnx_g, lnx_b, w_out, norm2_g, w_gate, w_up, w_down):
    b, s, d = x.shape
    assert s % (BLK * max(dil for _, dil in DILATED_PATTERNS)) == 0
    assert all(window // dil == BLK for window, dil in DILATED_PATTERNS)
    t = b * s
    tm = 512
    assert t % tm == 0
    row = lambda v: v.reshape(1, -1).astype(F32)

    x2d = x.reshape(t, d)
    proj = _in_proj(x2d, row(norm1_g), w_in.astype(BF16), tm).reshape(b, s, -1)

    slopes = jnp.exp2(-8.0 * jnp.arange(1, ATTN_HEADS + 1, dtype=F32) / ATTN_HEADS)
    pair = lambda v: jnp.tile(v.reshape(1, HEAD_DIM), (1, LANES // HEAD_DIM)).astype(F32)
    attn = _attention(proj, slopes, pair(q_norm_g), pair(k_norm_g), row(attn_out_g))

    w2p = w2.reshape(DECAY_LORA, _N_PAIR, LANES).transpose(1, 0, 2)
    a2p = a2.reshape(ICLR_LORA, _N_PAIR, LANES).transpose(1, 0, 2)
    zero = jnp.zeros_like(w2p)
    wwa = jnp.concatenate([jnp.concatenate([w2p, zero], axis=2),
                           jnp.concatenate([jnp.zeros_like(a2p), a2p], axis=2)], axis=1)
    wwa = wwa.transpose(1, 0, 2).reshape(LANES, _N_PAIR * 2 * LANES).astype(BF16)
    rwkv = _rwkv(proj, row(rwkv_mu), row(w0), row(a0), row(k_k), row(k_a), row(r_k),
                 row(lnx_g), row(lnx_b), wwa, g2.astype(BF16))

    out = _out_ffn(x2d, attn.reshape(t, ATTN_WIDTH), rwkv.reshape(t, RWKV_WIDTH),
                   w_out[:ATTN_WIDTH].astype(BF16), w_out[ATTN_WIDTH:].astype(BF16),
                   row(norm2_g), w_gate.astype(BF16), w_up.astype(BF16), w_down.astype(BF16), tm)
    return out.reshape(b, s, d)


def kernel(x, norm1_g, w_in, q_norm_g, k_norm_g, attn_out_g, rwkv_mu, w0, w2, a0, a2, g2,
           k_k, k_a, r_k, lnx_g, lnx_b, w_out, norm2_g, w_gate, w_up, w_down):
    h = x
    for layer in range(norm1_g.shape[0]):
        h = _layer(h, norm1_g[layer], w_in[layer], q_norm_g[layer], k_norm_g[layer],
                   attn_out_g[layer], rwkv_mu[layer], w0[layer], w2[layer], a0[layer],
                   a2[layer], g2[layer], k_k[layer], k_a[layer], r_k[layer], lnx_g[layer],
                   lnx_b[layer], w_out[layer], norm2_g[layer], w_gate[layer], w_up[layer],
                   w_down[layer])
    return h
```

```python
import functools

import jax
import jax.numpy as jnp
from jax import lax
from jax.experimental import pallas as pl
from jax.experimental.pallas import tpu as pltpu

F32 = jnp.float32
BF16 = jnp.bfloat16

HEAD_DIM = 64
LANES = 128
ATTN_HEADS = 8
RWKV_HEADS = 8
ATTN_WIDTH = ATTN_HEADS * HEAD_DIM
RWKV_WIDTH = RWKV_HEADS * HEAD_DIM
DECAY_LORA = 64
ICLR_LORA = 64
GATE_LORA = 128
DILATED_PATTERNS = ((128, 1), (512, 4), (2048, 16))
BLK = 128
ATTN_GROUP = 8
ROW_TILE = 512
IN_PROJ_ROWS = 1024
FFN_ROWS = 512
NORM_EPS = 1e-6
LNX_EPS = 64e-5
MASKED = -1e30
LOG2_E = 1.4426950408889634
VMEM_LIMIT = 56 * 1024 * 1024

_QB, _KB, _VB = 0, ATTN_WIDTH // LANES, 2 * ATTN_WIDTH // LANES
_RW0 = 3 * ATTN_WIDTH // LANES
_N_PAIR = RWKV_WIDTH // LANES


def _nt_dot(a, b):
    return lax.dot_general(a, b, (((1,), (1,)), ((), ())), preferred_element_type=F32)


def _tn_dot(a, b):
    return lax.dot_general(a, b, (((0,), (0,)), ((), ())), preferred_element_type=F32)


def _dot(a, b):
    return jnp.dot(a, b, preferred_element_type=F32)


def _pairwise_dots(lhs, rhs):
    assert len(lhs) == len(rhs) and len(lhs) % 2 == 0
    out = []
    for j in range(0, len(lhs), 2):
        zero = jnp.zeros_like(rhs[j])
        both = _dot(jnp.concatenate([lhs[j], lhs[j + 1]], axis=1),
                    jnp.concatenate([jnp.concatenate([rhs[j], zero], axis=1),
                                     jnp.concatenate([zero, rhs[j + 1]], axis=1)], axis=0))
        out += [both[:, :LANES], both[:, LANES:]]
    return out


def _aligned(index, multiple):
    if isinstance(index, int):
        return index
    return pl.multiple_of(index, multiple)


def _head0_lanes():
    return lax.broadcasted_iota(jnp.int32, (1, LANES), 1) < HEAD_DIM


def _head_averager():
    i = lax.broadcasted_iota(jnp.int32, (LANES, LANES), 0)
    j = lax.broadcasted_iota(jnp.int32, (LANES, LANES), 1)
    return jnp.where((i < HEAD_DIM) == (j < HEAD_DIM), 1.0 / HEAD_DIM, 0.0).astype(BF16)


def _head_mean_mxu(x, averager):
    return _dot(x.astype(BF16), averager)


def _head_sum(x, head0):
    s0 = jnp.sum(jnp.where(head0, x, 0.0), axis=-1, keepdims=True)
    s1 = jnp.sum(jnp.where(head0, 0.0, x), axis=-1, keepdims=True)
    return jnp.where(head0, s0, s1)


def _stack_heads(x, head0):
    return jnp.concatenate([jnp.where(head0, x, 0.0), jnp.where(head0, 0.0, x)], axis=0)


def _in_proj_kernel(x_ref, g_ref, w_ref, o_ref):
    x = x_ref[...]
    ms = jnp.mean(x * x, axis=-1, keepdims=True)
    xn = x * lax.rsqrt(ms + NORM_EPS) * g_ref[...]
    o_ref[...] = _dot(xn.astype(BF16), w_ref[...])


def _in_proj(x2d, g, w_bf16, tm):
    t, d = x2d.shape
    n = w_bf16.shape[1]
    return pl.pallas_call(
        _in_proj_kernel,
        grid=(t // tm,),
        in_specs=[
            pl.BlockSpec((tm, d), lambda i: (i, 0)),
            pl.BlockSpec((1, d), lambda i: (0, 0)),
            pl.BlockSpec((d, n), lambda i: (0, 0), pipeline_mode=pl.Buffered(1)),
        ],
        out_specs=pl.BlockSpec((tm, n), lambda i: (i, 0)),
        out_shape=jax.ShapeDtypeStruct((t, n), F32),
        compiler_params=pltpu.CompilerParams(
            dimension_semantics=("parallel",), vmem_limit_bytes=VMEM_LIMIT),
        name="in_proj",
    )(x2d, g, w_bf16)


def _attn_kernel(slope_ref, q_ref, k_ref, v_ref, gq_ref, gk_ref, go_ref, o_ref,
                 qn_ref, kn_ref, m_ref, l_ref, acc_ref, bias_ref):
    seq = q_ref.shape[1]
    hp = pl.program_id(0)
    head0 = _head0_lanes()
    averager = _head_averager()

    q_gain = gq_ref[...] * (HEAD_DIM ** -0.5 * LOG2_E)

    def prep(i, carry):
        rows = pl.ds(pl.multiple_of(i * ROW_TILE, ROW_TILE), ROW_TILE)
        q = q_ref[0, rows, :]
        k = k_ref[0, rows, :]
        qn_ref[rows, :] = q * lax.rsqrt(_head_mean_mxu(q * q, averager) + NORM_EPS) * q_gain
        kn_ref[rows, :] = k * lax.rsqrt(_head_mean_mxu(k * k, averager) + NORM_EPS) * gk_ref[...]
        return carry

    lax.fori_loop(0, seq // ROW_TILE, prep, 0)

    @pl.when(pl.program_id(1) == 0)
    def _():
        qi = lax.broadcasted_iota(jnp.int32, (2 * BLK, 2 * BLK), 0)
        ci = lax.broadcasted_iota(jnp.int32, (2 * BLK, 2 * BLK), 1)
        back = jnp.where(qi < BLK, qi, qi - BLK) + BLK - ci
        slope = jnp.where(qi < BLK, slope_ref[2 * hp], slope_ref[2 * hp + 1]) * LOG2_E
        for p, (window, dil) in enumerate(DILATED_PATTERNS):
            steps = window // dil
            valid = (back >= 0) & (back <= steps)
            bias_ref[p] = jnp.where(valid, -slope * (back.astype(F32) * dil), MASKED)

    ones_keys = jnp.ones((2 * BLK, LANES), BF16)

    def band_blocks(p, dil, starts, has_prevs):
        each = lambda f, *lists: [f(*args) for args in zip(*lists)]
        bf = lambda v: v.astype(BF16)
        ds = lambda start: pl.ds(start, BLK, stride=dil) if dil > 1 else pl.ds(start, BLK)
        rowss = [ds(start) for start in starts]

        def keys_values(start, rows, has_prev):
            kb = kn_ref[rows, :]
            vb = v_ref[0, rows, :]
            if has_prev:
                prows = ds(start - BLK * dil)
                kb = jnp.concatenate([kn_ref[prows, :], kb], axis=0)
                vb = jnp.concatenate([v_ref[0, prows, :], vb], axis=0)
            return bf(kb), jnp.concatenate([bf(vb), ones_keys[:vb.shape[0]]], axis=1)
        kv = each(keys_values, starts, rowss, has_prevs)
        qs = each(lambda rows: bf(_stack_heads(qn_ref[rows, :], head0)), rowss)
        s = each(lambda q, kvb, has_prev:
                 _nt_dot(q, kvb[0]) + (bias_ref[p] if has_prev else bias_ref[p, :, BLK:]),
                 qs, kv, has_prevs)
        m_blk = each(lambda v: jnp.max(v, axis=-1, keepdims=True), s)
        pr = each(lambda v, m: bf(jnp.exp2(v - m)), s, m_blk)
        pv = each(lambda v, kvb: _dot(v, kvb[1]), pr, kv)

        def keep(rows, m_b, pv_b):
            m_ref[p, rows, :] = jnp.where(head0, m_b[:BLK], m_b[BLK:])
            acc_ref[p, rows, :] = jnp.where(head0, pv_b[:BLK, :LANES], pv_b[BLK:, :LANES])
            l_ref[p, rows, :] = jnp.where(head0, pv_b[:BLK, LANES:], pv_b[BLK:, LANES:])
        each(keep, rowss, m_blk, pv)

    for p, (window, dil) in enumerate(DILATED_PATTERNS):
        n_blk = seq // (BLK * dil)
        span = BLK * dil
        if n_blk >= ATTN_GROUP:
            assert n_blk % ATTN_GROUP == 0
            groups = n_blk // ATTN_GROUP

            def residue(r, carry, p=p, dil=dil, span=span, groups=groups):
                for grp in range(groups):
                    blocks = range(grp * ATTN_GROUP, (grp + 1) * ATTN_GROUP)
                    band_blocks(p, dil, [r + n * span for n in blocks], [n > 0 for n in blocks])
                return carry
        else:
            assert ATTN_GROUP % n_blk == 0 and dil % (ATTN_GROUP // n_blk) == 0
            per_group = ATTN_GROUP // n_blk

            def residue(i, carry, p=p, dil=dil, span=span, n_blk=n_blk, per_group=per_group):
                classes = [i * per_group + g for g in range(per_group)]
                band_blocks(p, dil, [r + n * span for r in classes for n in range(n_blk)],
                            [n > 0 for r in classes for n in range(n_blk)])
                return carry
        n_steps = dil if n_blk >= ATTN_GROUP else dil // (ATTN_GROUP // n_blk)
        if n_steps == 1:
            residue(0, 0)
        else:
            lax.fori_loop(0, n_steps, residue, 0)

    def finish(i, carry):
        rows = pl.ds(pl.multiple_of(i * ROW_TILE, ROW_TILE), ROW_TILE)
        n_pat = len(DILATED_PATTERNS)
        m_p = [m_ref[p, rows, :] for p in range(n_pat)]
        m_all = functools.reduce(jnp.maximum, m_p)
        scale = [jnp.exp2(m - m_all) for m in m_p]
        num = sum(sc * acc_ref[p, rows, :] for p, sc in enumerate(scale))
        den = sum(sc * l_ref[p, rows, :] for p, sc in enumerate(scale))
        o = num / den
        ms = _head_mean_mxu(o * o, averager)
        o_ref[0, rows, :] = o * lax.rsqrt(ms + NORM_EPS) * go_ref[...]
        return carry

    lax.fori_loop(0, seq // ROW_TILE, finish, 0)


def _attention(proj, slopes, gq, gk, go):
    b, s, _ = proj.shape
    blk = lambda off: pl.BlockSpec((1, s, LANES), lambda j, i: (i, 0, off + j))
    vec = lambda: pl.BlockSpec((1, LANES), lambda j, i: (0, 0))
    return pl.pallas_call(
        _attn_kernel,
        grid=(_N_PAIR, b),
        in_specs=[
            pl.BlockSpec(memory_space=pltpu.SMEM),
            blk(_QB), blk(_KB), blk(_VB),
            vec(), vec(),
            pl.BlockSpec((1, LANES), lambda j, i: (0, j)),
        ],
        out_specs=pl.BlockSpec((1, s, LANES), lambda j, i: (i, 0, j)),
        out_shape=jax.ShapeDtypeStruct((b, s, ATTN_WIDTH), F32),
        scratch_shapes=[
            pltpu.VMEM((s, LANES), F32),
            pltpu.VMEM((s, LANES), F32),
            pltpu.VMEM((len(DILATED_PATTERNS), s, LANES), F32),
            pltpu.VMEM((len(DILATED_PATTERNS), s, LANES), F32),
            pltpu.VMEM((len(DILATED_PATTERNS), s, LANES), F32),
            pltpu.VMEM((len(DILATED_PATTERNS), 2 * BLK, 2 * BLK), F32),
        ],
        compiler_params=pltpu.CompilerParams(
            dimension_semantics=("arbitrary", "arbitrary"), vmem_limit_bytes=VMEM_LIMIT),
        name="dilated_attn",
    )(slopes, proj, proj, proj, gq, gk, go)


def _wkv_kernel(pr_ref, pk_ref, pv_ref, pwa_ref, pg_ref,
                mur_ref, muk_ref, muv_ref, muwa_ref, mug_ref,
                w0_ref, a0_ref, kk_ref, ka_ref, rk_ref, lng_ref, lnb_ref,
                wwa_ref, g2_ref, o_ref, h_ref, *slots):
    seq = pr_ref.shape[1]
    n_chunks = seq // BLK
    head0 = _head0_lanes()
    row = lax.broadcasted_iota(jnp.int32, (BLK, BLK), 0)
    col = lax.broadcasted_iota(jnp.int32, (BLK, BLK), 1)
    strict = col < row
    incl = col <= row
    tril_incl = jnp.where(incl, 1.0, 0.0).astype(BF16)
    first_row = lax.broadcasted_iota(jnp.int32, (BLK, LANES), 0) == 0
    srow = lax.broadcasted_iota(jnp.int32, (2 * BLK, LANES), 0) < BLK
    slane = lax.broadcasted_iota(jnp.int32, (2 * BLK, LANES), 1) < HEAD_DIM
    own_head = srow == slane
    zeros_blk = jnp.zeros((BLK, BLK), F32)
    zeros_bf = jnp.zeros((2 * BLK, 2 * BLK), BF16)
    eye_bf = jnp.where(lax.broadcasted_iota(jnp.int32, (2 * BLK, 2 * BLK), 0)
                       == lax.broadcasted_iota(jnp.int32, (2 * BLK, 2 * BLK), 1),
                       1.0, 0.0).astype(BF16)

    h_ref[...] = jnp.zeros(h_ref.shape, F32)

    each = lambda f, *lists: [f(*args) for args in zip(*lists)]
    bf = lambda v: v.astype(BF16)
    pairs = range(_N_PAIR)
    lanes = [slice(j * LANES, (j + 1) * LANES) for j in pairs]
    param = lambda ref: [ref[:, sl] for sl in lanes]

    def produce(c, slot):
        row0 = _aligned(c * BLK, BLK)
        rows = pl.ds(row0, BLK)
        prev8 = pl.ds(_aligned(max(row0 - 8, 0) if isinstance(row0, int)
                               else jnp.maximum(row0 - 8, 0), 8), 8)
        has_prev = jnp.where(c > 0, 1.0, 0.0).astype(F32)

        def shifted(ref, mu_ref, sl):
            p = ref[0, rows, sl]
            last = ref[0, prev8, sl][7:8, :] * has_prev
            prev = jnp.where(first_row, last, pltpu.roll(p, 1, 0))
            return p + (prev - p) * mu_ref[:, sl]

        xr = [shifted(pr_ref, mur_ref, sl) for sl in lanes]
        xk = [shifted(pk_ref, muk_ref, sl) for sl in lanes]
        xv = [shifted(pv_ref, muv_ref, sl) for sl in lanes]
        xwa = shifted(pwa_ref, muwa_ref, slice(None))
        xg = shifted(pg_ref, mug_ref, slice(None))
        yield

        lora_all = _dot(bf(jnp.where(head0, jnp.tanh(xwa), xwa)), wwa_ref[...])
        lora = [lora_all[:, 2 * LANES * j:2 * LANES * (j + 1)] for j in pairs]
        gate_all = _dot(bf(jax.nn.sigmoid(xg)), g2_ref[...])
        gate = [gate_all[:, sl] for sl in lanes]
        yield

        def decay_log(lo, w0):
            zw = -(w0 + lo[:, :LANES])
            softplus = jnp.maximum(zw, 0.0) + jnp.log(1.0 + jnp.exp(-jnp.abs(zw)))
            return -jnp.exp(-softplus - 0.5)
        log_decay = each(decay_log, lora, param(w0_ref))
        a_sig = each(lambda lo, a0: jax.nn.sigmoid(a0 + lo[:, LANES:]), lora, param(a0_ref))

        def unit_key(k, k_k):
            kkv = k * k_k
            return kkv / jnp.maximum(jnp.sqrt(_head_sum(kkv * kkv, head0)), 1e-12)
        kk = each(unit_key, xk, param(kk_ref))
        k2 = each(lambda k, a, k_a: k * (1.0 + (a - 1.0) * k_a), xk, a_sig, param(ka_ref))
        b_vec = each(lambda u, a: u * a, kk, a_sig)
        yield

        def cumulative(ld):
            ld_hi = bf(ld)
            ld_lo = bf(ld - ld_hi.astype(F32))
            cum = _dot(tril_incl, jnp.concatenate([ld_hi, ld_lo], axis=1))
            return cum[:, :LANES] + cum[:, LANES:]
        lw = each(cumulative, log_decay)
        yield
        lw_last = [v[BLK - 1:BLK, :] for v in lw]
        w_inv = each(lambda v: jnp.exp(-v), lw)
        w_tail = each(lambda v, last: jnp.exp(last - v), lw, lw_last)
        w_all = each(jnp.exp, lw_last)

        at = each(lambda u, v, ld: -u * jnp.exp(v - ld), kk, lw, log_decay)
        rt = each(lambda r, v: r * jnp.exp(v), xr, lw)
        bt = each(lambda b, w: b * w, b_vec, w_inv)
        kt = each(lambda k, w: k * w, k2, w_inv)
        bh = each(lambda b, w: b * w, b_vec, w_tail)
        kh = each(lambda k, w: k * w, k2, w_tail)
        yield

        def pair_products(a, r, b, k):
            lhs = jnp.concatenate([jnp.where(head0, a, 0.0), jnp.where(head0, r, 0.0),
                                   jnp.where(head0, 0.0, a), jnp.where(head0, 0.0, r)], axis=0)
            return _nt_dot(bf(lhs), bf(jnp.concatenate([b, k], axis=0)))
        pp = each(pair_products, at, rt, bt, kt)
        yield

        def n_matrix(q):
            aab0 = jnp.where(strict, q[0:BLK, :BLK], 0.0)
            aab1 = jnp.where(strict, q[2 * BLK:3 * BLK, :BLK], 0.0)
            return bf(jnp.concatenate([jnp.concatenate([aab0, zeros_blk], axis=1),
                                       jnp.concatenate([zeros_blk, aab1], axis=1)], axis=0))

        def aak_matrix(q):
            return bf(jnp.concatenate([jnp.where(strict, q[0:BLK, BLK:], 0.0),
                                       jnp.where(strict, q[2 * BLK:3 * BLK, BLK:], 0.0)], axis=0))
        akv = _pairwise_dots(each(aak_matrix, pp), each(bf, xv))

        def x_init(a, av):
            return jnp.concatenate([_stack_heads(a, head0), jnp.where(own_head, av, 0.0)], axis=1)

        def read_matrices(q):
            arb = jnp.concatenate([jnp.where(incl, q[BLK:2 * BLK, :BLK], 0.0),
                                   jnp.where(incl, q[3 * BLK:, :BLK], 0.0)], axis=1)
            ark = jnp.concatenate([jnp.where(incl, q[BLK:2 * BLK, BLK:], 0.0),
                                   jnp.where(incl, q[3 * BLK:, BLK:], 0.0)], axis=1)
            return bf(arb), bf(ark)

        for j in pairs:
            slot["n"][j] = n_matrix(pp[j])
            slot["x"][j] = bf(x_init(at[j], akv[j]))
            slot["arb"][j], slot["ark"][j] = read_matrices(pp[j])
        yield
        r_k = param(rk_ref)
        for j in pairs:
            slot["rt"][j] = rt[j]
            slot["bhs"][j] = bf(_stack_heads(bh[j], head0))
            slot["khs"][j] = bf(_stack_heads(kh[j], head0))
            slot["vs"][j] = bf(_stack_heads(xv[j], head0))
            slot["wall"][j] = jnp.broadcast_to(w_all[j], (8, LANES))
            slot["bonus"][j] = _head_sum(xr[j] * k2[j] * r_k[j], head0) * xv[j]
            slot["gate"][j] = gate[j]

    def consume(c, slot):
        n_chunk = _N_PAIR
        n_bf = [slot["n"][j] for j in range(n_chunk)]
        x_bf = [slot["x"][j] for j in range(n_chunk)]

        def live_rows(m, skip):
            if not skip:
                return m
            return jnp.concatenate([m[skip:BLK], m[BLK + skip:]], axis=0)

        def merge_rows(old, new, skip):
            if not skip:
                return new
            live = BLK - skip
            return jnp.concatenate([old[:skip], new[:live], old[BLK:BLK + skip], new[live:]],
                                   axis=0)

        span = 1
        while span < BLK:
            skip = span if span >= 16 else 0
            x_bf = each(lambda xx, nn: merge_rows(
                xx, bf(_dot(live_rows(nn + eye_bf, skip), xx)), skip), x_bf, n_bf)
            span *= 2
            if span < BLK:
                skip = span if span >= 16 else 0
                n_bf = each(lambda nn: merge_rows(
                    zeros_bf, bf(_dot(live_rows(nn, skip), nn)), skip), n_bf)
            yield
        v_s = [slot["vs"][j] for j in range(n_chunk)]

        def read_side(j, xb, vs):
            z = _dot(slot["arb"][j], xb)
            return slot["rt"][j] + z[:, :LANES], z[:, LANES:] + _dot(slot["ark"][j], vs)
        r_eff_y0 = each(read_side, range(n_chunk), x_bf, v_s)
        yield

        def state_side(j, xb, vs):
            d1 = _tn_dot(slot["bhs"][j], xb)
            d2 = _tn_dot(slot["khs"][j], vs)
            w_all = slot["wall"][j][0:1, :]
            return jnp.where(row == col, w_all, 0.0) + d1[:, :LANES], d1[:, LANES:] + d2
        m_eff_g_eff = each(state_side, range(n_chunk), x_bf, v_s)
        rm_lhs = each(lambda ry, mg: bf(jnp.concatenate([ry[0], mg[0]], axis=0)),
                      r_eff_y0, m_eff_g_eff)
        yield

        rm = _pairwise_dots(rm_lhs, [bf(h_ref[j]) for j in pairs])
        y = each(lambda v, ry: v[:BLK] + ry[1], rm, r_eff_y0)
        for j in pairs:
            h_ref[j] = rm[j][BLK:] + m_eff_g_eff[j][1]
        yield
        mean = each(lambda v: _head_sum(v, head0) * (1.0 / HEAD_DIM), y)
        yc = each(lambda v, m: v - m, y, mean)
        var = each(lambda v: _head_sum(v * v, head0) * (1.0 / HEAD_DIM), yc)
        rows = pl.ds(_aligned(c * BLK, BLK), BLK)
        lnx_g, lnx_b = param(lng_ref), param(lnb_ref)
        for j in pairs:
            yn = yc[j] * lax.rsqrt(var[j] + LNX_EPS) * lnx_g[j] + lnx_b[j]
            o_ref[0, rows, lanes[j]] = (yn + slot["bonus"][j]) * slot["gate"][j]

    def alternate(*stages):
        live = list(stages)
        while live:
            for gen in list(live):
                try:
                    next(gen)
                except StopIteration:
                    live.remove(gen)

    names = ("n", "x", "arb", "ark", "rt", "bhs", "khs", "vs", "wall", "bonus", "gate")
    slot_a = dict(zip(names, slots[:len(names)]))
    slot_b = dict(zip(names, slots[len(names):]))
    assert n_chunks % 2 == 0

    alternate(produce(0, slot_a))

    def chunk_pair(i, carry):
        c = 2 * i
        alternate(consume(c, slot_a), produce(c + 1, slot_b))
        alternate(consume(c + 1, slot_b), produce(jnp.minimum(c + 2, n_chunks - 1), slot_a))
        return carry

    lax.fori_loop(0, n_chunks // 2, chunk_pair, 0)


def _wkv_slot_shapes():
    g = _N_PAIR
    return [
        pltpu.VMEM((g, 2 * BLK, 2 * BLK), BF16),
        pltpu.VMEM((g, 2 * BLK, 2 * LANES), BF16),
        pltpu.VMEM((g, BLK, 2 * BLK), BF16),
        pltpu.VMEM((g, BLK, 2 * BLK), BF16),
        pltpu.VMEM((g, BLK, LANES), F32),
        pltpu.VMEM((g, 2 * BLK, LANES), BF16),
        pltpu.VMEM((g, 2 * BLK, LANES), BF16),
        pltpu.VMEM((g, 2 * BLK, LANES), BF16),
        pltpu.VMEM((g, 8, LANES), F32),
        pltpu.VMEM((g, BLK, LANES), F32),
        pltpu.VMEM((g, BLK, LANES), F32),
    ]


def _rwkv(proj, mu, w0, a0, k_k, k_a, r_k, lnx_g, lnx_b, wwa, g2_bf16):
    b, s, _ = proj.shape
    np_ = _N_PAIR
    w = RWKV_WIDTH
    wide = lambda off: pl.BlockSpec((1, s, w), lambda i: (i, 0, off))
    lane = lambda off: pl.BlockSpec((1, s, LANES), lambda i: (i, 0, off))
    vec_w = lambda off: pl.BlockSpec((1, w), lambda i: (0, off))
    vec_l = lambda off: pl.BlockSpec((1, LANES), lambda i: (0, off))
    full = lambda shape: pl.BlockSpec(shape, lambda i: (0,) * len(shape))
    rw0 = _RW0 * LANES // w
    return pl.pallas_call(
        _wkv_kernel,
        grid=(b,),
        in_specs=[
            wide(rw0), wide(rw0 + 1), wide(rw0 + 2),
            lane(_RW0 + 3 * np_), lane(_RW0 + 3 * np_ + 1),
            vec_w(0), vec_w(1), vec_w(2), vec_l(3 * np_), vec_l(3 * np_ + 1),
            vec_w(0), vec_w(0), vec_w(0), vec_w(0), vec_w(0), vec_w(0), vec_w(0),
            full((LANES, np_ * 2 * LANES)),
            full((GATE_LORA, w)),
        ],
        out_specs=pl.BlockSpec((1, s, w), lambda i: (i, 0, 0)),
        out_shape=jax.ShapeDtypeStruct((b, s, RWKV_WIDTH), F32),
        scratch_shapes=[pltpu.VMEM((np_, LANES, LANES), F32)] + 2 * _wkv_slot_shapes(),
        compiler_params=pltpu.CompilerParams(
            dimension_semantics=("parallel",), vmem_limit_bytes=VMEM_LIMIT),
        name="rwkv7",
    )(proj, proj, proj, proj, proj, mu, mu, mu, mu, mu,
      w0, a0, k_k, k_a, r_k, lnx_g, lnx_b, wwa, g2_bf16)


def _out_ffn_kernel(x_ref, attn_ref, rwkv_ref, woa_ref, wor_ref, g2_ref,
                    wg_ref, wu_ref, wd_ref, o_ref):
    x1 = (x_ref[...] + _dot(attn_ref[...].astype(BF16), woa_ref[...])
          + _dot(rwkv_ref[...].astype(BF16), wor_ref[...]))
    ms = jnp.mean(x1 * x1, axis=-1, keepdims=True)
    xn = (x1 * lax.rsqrt(ms + NORM_EPS) * g2_ref[...]).astype(BF16)
    gate = _dot(xn, wg_ref[...])
    up = _dot(xn, wu_ref[...])
    hidden = (gate * jax.nn.sigmoid(gate) * up).astype(BF16)
    o_ref[...] = x1 + _dot(hidden, wd_ref[...])


def _out_ffn(x2d, attn2d, rwkv2d, woa, wor, g2, wg, wu, wd, tm):
    t, d = x2d.shape
    f = wg.shape[1]
    tok = lambda w: pl.BlockSpec((tm, w), lambda i: (i, 0))
    res = lambda shape: pl.BlockSpec(shape, lambda i: (0, 0), pipeline_mode=pl.Buffered(1))
    return pl.pallas_call(
        _out_ffn_kernel,
        grid=(t // tm,),
        in_specs=[
            tok(d), tok(ATTN_WIDTH), tok(RWKV_WIDTH),
            res((ATTN_WIDTH, d)), res((RWKV_WIDTH, d)),
            pl.BlockSpec((1, d), lambda i: (0, 0)),
            res((d, f)), res((d, f)), res((f, d)),
        ],
        out_specs=tok(d),
        out_shape=jax.ShapeDtypeStruct((t, d), F32),
        compiler_params=pltpu.CompilerParams(
            dimension_semantics=("parallel",), vmem_limit_bytes=VMEM_LIMIT),
        name="out_ffn",
    )(x2d, attn2d, rwkv2d, woa, wor, g2, wg, wu, wd)


def _layer(x, norm1_g, w_in, q_norm_g, k_norm_g, attn_out_g, rwkv_mu, w0, w2, a0, a2, g2,
           k_k, k_a, r_k, lnx_g, lnx_b, w_out, norm2_g, w_gate, w_up, w_down):
    b, s, d = x.shape
    assert s % (BLK * max(dil for _, dil in DILATED_PATTERNS)) == 0
    assert all(window // dil == BLK for window, dil in DILATED_PATTERNS)
    t = b * s
    assert t % IN_PROJ_ROWS == 0 and t % FFN_ROWS == 0
    row = lambda v: v.reshape(1, -1).astype(F32)

    x2d = x.reshape(t, d)
    proj = _in_proj(x2d, row(norm1_g), w_in.astype(BF16), IN_PROJ_ROWS).reshape(b, s, -1)

    slopes = jnp.exp2(-8.0 * jnp.arange(1, ATTN_HEADS + 1, dtype=F32) / ATTN_HEADS)
    pair = lambda v: jnp.tile(v.reshape(1, HEAD_DIM), (1, LANES // HEAD_DIM)).astype(F32)
    attn = _attention(proj, slopes, pair(q_norm_g), pair(k_norm_g), row(attn_out_g))

    w2p = w2.reshape(DECAY_LORA, _N_PAIR, LANES).transpose(1, 0, 2)
    a2p = a2.reshape(ICLR_LORA, _N_PAIR, LANES).transpose(1, 0, 2)
    zero = jnp.zeros_like(w2p)
    wwa = jnp.concatenate([jnp.concatenate([w2p, zero], axis=2),
                           jnp.concatenate([jnp.zeros_like(a2p), a2p], axis=2)], axis=1)
    wwa = wwa.transpose(1, 0, 2).reshape(LANES, _N_PAIR * 2 * LANES).astype(BF16)
    rwkv = _rwkv(proj, row(rwkv_mu), row(w0), row(a0), row(k_k), row(k_a), row(r_k),
                 row(lnx_g), row(lnx_b), wwa, g2.astype(BF16))

    out = _out_ffn(x2d, attn.reshape(t, ATTN_WIDTH), rwkv.reshape(t, RWKV_WIDTH),
                   w_out[:ATTN_WIDTH].astype(BF16), w_out[ATTN_WIDTH:].astype(BF16),
                   row(norm2_g), w_gate.astype(BF16), w_up.astype(BF16), w_down.astype(BF16),
                   FFN_ROWS)
    return out.reshape(b, s, d)


def kernel(x, norm1_g, w_in, q_norm_g, k_norm_g, attn_out_g, rwkv_mu, w0, w2, a0, a2, g2,
           k_k, k_a, r_k, lnx_g, lnx_b, w_out, norm2_g, w_gate, w_up, w_down):
    h = x
    for layer in range(norm1_g.shape[0]):
        h = _layer(h, norm1_g[layer], w_in[layer], q_norm_g[layer], k_norm_g[layer],
                   attn_out_g[layer], rwkv_mu[layer], w0[layer], w2[layer], a0[layer],
                   a2[layer], g2[layer], k_k[layer], k_a[layer], r_k[layer], lnx_g[layer],
                   lnx_b[layer], w_out[layer], norm2_g[layer], w_gate[layer], w_up[layer],
                   w_down[layer])
    return h
```

```python
import functools

import jax
import jax.numpy as jnp
from jax import lax
from jax.experimental import pallas as pl
from jax.experimental.pallas import tpu as pltpu

F32 = jnp.float32
BF16 = jnp.bfloat16

HEAD_DIM = 64
LANES = 128
ATTN_HEADS = 8
RWKV_HEADS = 8
ATTN_WIDTH = ATTN_HEADS * HEAD_DIM
RWKV_WIDTH = RWKV_HEADS * HEAD_DIM
DECAY_LORA = 64
ICLR_LORA = 64
GATE_LORA = 128
DILATED_PATTERNS = ((128, 1), (512, 4), (2048, 16))
BLK = 128
ATTN_GROUP = 8
ROW_TILE = 512
IN_PROJ_ROWS = 1024
FFN_ROWS = 512
NORM_EPS = 1e-6
LNX_EPS = 64e-5
MASKED = -1e30
LOG2_E = 1.4426950408889634
VMEM_LIMIT = 56 * 1024 * 1024

_QB, _KB, _VB = 0, ATTN_WIDTH // LANES, 2 * ATTN_WIDTH // LANES
_RW0 = 3 * ATTN_WIDTH // LANES
_N_PAIR = RWKV_WIDTH // LANES


def _nt_dot(a, b):
    return lax.dot_general(a, b, (((1,), (1,)), ((), ())), preferred_element_type=F32)


def _tn_dot(a, b):
    return lax.dot_general(a, b, (((0,), (0,)), ((), ())), preferred_element_type=F32)


def _dot(a, b):
    return jnp.dot(a, b, preferred_element_type=F32)


def _pairwise_dots(lhs, rhs):
    assert len(lhs) == len(rhs) and len(lhs) % 2 == 0
    out = []
    for j in range(0, len(lhs), 2):
        zero = jnp.zeros_like(rhs[j])
        both = _dot(jnp.concatenate([lhs[j], lhs[j + 1]], axis=1),
                    jnp.concatenate([jnp.concatenate([rhs[j], zero], axis=1),
                                     jnp.concatenate([zero, rhs[j + 1]], axis=1)], axis=0))
        out += [both[:, :LANES], both[:, LANES:]]
    return out


def _aligned(index, multiple):
    if isinstance(index, int):
        return index
    return pl.multiple_of(index, multiple)


def _head0_lanes():
    return lax.broadcasted_iota(jnp.int32, (1, LANES), 1) < HEAD_DIM


def _head_averager():
    i = lax.broadcasted_iota(jnp.int32, (LANES, LANES), 0)
    j = lax.broadcasted_iota(jnp.int32, (LANES, LANES), 1)
    return jnp.where((i < HEAD_DIM) == (j < HEAD_DIM), 1.0 / HEAD_DIM, 0.0).astype(BF16)


def _head_mean_mxu(x, averager):
    return _dot(x.astype(BF16), averager)


def _head_sum(x, head0):
    s0 = jnp.sum(jnp.where(head0, x, 0.0), axis=-1, keepdims=True)
    s1 = jnp.sum(jnp.where(head0, 0.0, x), axis=-1, keepdims=True)
    return jnp.where(head0, s0, s1)


def _stack_heads(x, head0):
    return jnp.concatenate([jnp.where(head0, x, 0.0), jnp.where(head0, 0.0, x)], axis=0)


def _in_proj_kernel(x_ref, g_ref, w_ref, qk_gain_ref, o_ref):
    x = x_ref[...]
    ms = jnp.mean(x * x, axis=-1, keepdims=True)
    xn = (x * lax.rsqrt(ms + NORM_EPS) * g_ref[...]).astype(BF16)
    n_qk = qk_gain_ref.shape[1]
    qk = _dot(xn, w_ref[:, :n_qk])
    o_ref[:, n_qk:] = _dot(xn, w_ref[:, n_qk:])
    head0 = _head0_lanes()
    for j in range(n_qk // LANES):
        cols = slice(j * LANES, (j + 1) * LANES)
        v = qk[:, cols]
        mean_sq = _head_sum(v * v, head0) * (1.0 / HEAD_DIM)
        o_ref[:, cols] = v * lax.rsqrt(mean_sq + NORM_EPS) * qk_gain_ref[:, cols]


def _in_proj(x2d, g, w_bf16, qk_gain, tm):
    t, d = x2d.shape
    n = w_bf16.shape[1]
    return pl.pallas_call(
        _in_proj_kernel,
        grid=(t // tm,),
        in_specs=[
            pl.BlockSpec((tm, d), lambda i: (i, 0)),
            pl.BlockSpec((1, d), lambda i: (0, 0)),
            pl.BlockSpec((d, n), lambda i: (0, 0), pipeline_mode=pl.Buffered(1)),
            pl.BlockSpec((1, qk_gain.shape[1]), lambda i: (0, 0)),
        ],
        out_specs=pl.BlockSpec((tm, n), lambda i: (i, 0)),
        out_shape=jax.ShapeDtypeStruct((t, n), F32),
        compiler_params=pltpu.CompilerParams(
            dimension_semantics=("parallel",), vmem_limit_bytes=VMEM_LIMIT),
        name="in_proj",
    )(x2d, g, w_bf16, qk_gain)


def _attn_kernel(slope_ref, q_ref, k_ref, v_ref, go_ref, o_ref,
                 m_ref, l_ref, acc_ref, bias_ref):
    seq = q_ref.shape[1]
    hp = pl.program_id(0)
    head0 = _head0_lanes()
    averager = _head_averager()
    qn_ref = q_ref.at[0]
    kn_ref = k_ref.at[0]

    @pl.when(pl.program_id(1) == 0)
    def _():
        qi = lax.broadcasted_iota(jnp.int32, (2 * BLK, 2 * BLK), 0)
        ci = lax.broadcasted_iota(jnp.int32, (2 * BLK, 2 * BLK), 1)
        back = jnp.where(qi < BLK, qi, qi - BLK) + BLK - ci
        slope = jnp.where(qi < BLK, slope_ref[2 * hp], slope_ref[2 * hp + 1]) * LOG2_E
        for p, (window, dil) in enumerate(DILATED_PATTERNS):
            steps = window // dil
            valid = (back >= 0) & (back <= steps)
            bias_ref[p] = jnp.where(valid, -slope * (back.astype(F32) * dil), MASKED)

    ones_keys = jnp.ones((2 * BLK, LANES), BF16)

    def band_blocks(p, dil, starts, has_prevs):
        each = lambda f, *lists: [f(*args) for args in zip(*lists)]
        bf = lambda v: v.astype(BF16)
        ds = lambda start: pl.ds(start, BLK, stride=dil) if dil > 1 else pl.ds(start, BLK)
        rowss = [ds(start) for start in starts]

        def keys_values(start, rows, has_prev):
            kb = kn_ref[rows, :]
            vb = v_ref[0, rows, :]
            if has_prev:
                prows = ds(start - BLK * dil)
                kb = jnp.concatenate([kn_ref[prows, :], kb], axis=0)
                vb = jnp.concatenate([v_ref[0, prows, :], vb], axis=0)
            return bf(kb), jnp.concatenate([bf(vb), ones_keys[:vb.shape[0]]], axis=1)
        kv = each(keys_values, starts, rowss, has_prevs)
        qs = each(lambda rows: bf(_stack_heads(qn_ref[rows, :], head0)), rowss)
        s = each(lambda q, kvb, has_prev:
                 _nt_dot(q, kvb[0]) + (bias_ref[p] if has_prev else bias_ref[p, :, BLK:]),
                 qs, kv, has_prevs)
        m_blk = each(lambda v: jnp.max(v, axis=-1, keepdims=True), s)
        pr = each(lambda v, m: bf(jnp.exp2(v - m)), s, m_blk)
        pv = each(lambda v, kvb: _dot(v, kvb[1]), pr, kv)

        def keep(rows, m_b, pv_b):
            m_ref[p, rows, :] = jnp.where(head0, m_b[:BLK], m_b[BLK:])
            acc_ref[p, rows, :] = jnp.where(head0, pv_b[:BLK, :LANES], pv_b[BLK:, :LANES])
            l_ref[p, rows, :] = jnp.where(head0, pv_b[:BLK, LANES:], pv_b[BLK:, LANES:])
        each(keep, rowss, m_blk, pv)

    for p, (window, dil) in enumerate(DILATED_PATTERNS):
        n_blk = seq // (BLK * dil)
        span = BLK * dil
        if n_blk >= ATTN_GROUP:
            assert n_blk % ATTN_GROUP == 0
            groups = n_blk // ATTN_GROUP

            def residue(r, carry, p=p, dil=dil, span=span, groups=groups):
                for grp in range(groups):
                    blocks = range(grp * ATTN_GROUP, (grp + 1) * ATTN_GROUP)
                    band_blocks(p, dil, [r + n * span for n in blocks], [n > 0 for n in blocks])
                return carry
        else:
            assert ATTN_GROUP % n_blk == 0 and dil % (ATTN_GROUP // n_blk) == 0
            per_group = ATTN_GROUP // n_blk

            def residue(i, carry, p=p, dil=dil, span=span, n_blk=n_blk, per_group=per_group):
                classes = [i * per_group + g for g in range(per_group)]
                band_blocks(p, dil, [r + n * span for r in classes for n in range(n_blk)],
                            [n > 0 for r in classes for n in range(n_blk)])
                return carry
        n_steps = dil if n_blk >= ATTN_GROUP else dil // (ATTN_GROUP // n_blk)
        if n_steps == 1:
            residue(0, 0)
        else:
            lax.fori_loop(0, n_steps, residue, 0)

    def finish(i, carry):
        rows = pl.ds(pl.multiple_of(i * ROW_TILE, ROW_TILE), ROW_TILE)
        n_pat = len(DILATED_PATTERNS)
        m_p = [m_ref[p, rows, :] for p in range(n_pat)]
        m_all = functools.reduce(jnp.maximum, m_p)
        scale = [jnp.exp2(m - m_all) for m in m_p]
        num = sum(sc * acc_ref[p, rows, :] for p, sc in enumerate(scale))
        den = sum(sc * l_ref[p, rows, :] for p, sc in enumerate(scale))
        o = num / den
        ms = _head_mean_mxu(o * o, averager)
        o_ref[0, rows, :] = o * lax.rsqrt(ms + NORM_EPS) * go_ref[...]
        return carry

    lax.fori_loop(0, seq // ROW_TILE, finish, 0)


def _attention(proj, slopes, go):
    b, s, _ = proj.shape
    blk = lambda off: pl.BlockSpec((1, s, LANES), lambda j, i: (i, 0, off + j))
    return pl.pallas_call(
        _attn_kernel,
        grid=(_N_PAIR, b),
        in_specs=[
            pl.BlockSpec(memory_space=pltpu.SMEM),
            blk(_QB), blk(_KB), blk(_VB),
            pl.BlockSpec((1, LANES), lambda j, i: (0, j)),
        ],
        out_specs=pl.BlockSpec((1, s, LANES), lambda j, i: (i, 0, j)),
        out_shape=jax.ShapeDtypeStruct((b, s, ATTN_WIDTH), F32),
        scratch_shapes=[
            pltpu.VMEM((len(DILATED_PATTERNS), s, LANES), F32),
            pltpu.VMEM((len(DILATED_PATTERNS), s, LANES), F32),
            pltpu.VMEM((len(DILATED_PATTERNS), s, LANES), F32),
            pltpu.VMEM((len(DILATED_PATTERNS), 2 * BLK, 2 * BLK), F32),
        ],
        compiler_params=pltpu.CompilerParams(
            dimension_semantics=("arbitrary", "arbitrary"), vmem_limit_bytes=VMEM_LIMIT),
        name="dilated_attn",
    )(slopes, proj, proj, proj, go)


def _wkv_kernel(pr_ref, pk_ref, pv_ref, pwa_ref, pg_ref,
                mur_ref, muk_ref, muv_ref, muwa_ref, mug_ref,
                w0_ref, a0_ref, kk_ref, ka_ref, rk_ref, lng_ref, lnb_ref,
                wwa_ref, g2_ref, o_ref, h_ref, *slots):
    seq = pr_ref.shape[1]
    n_chunks = seq // BLK
    head0 = _head0_lanes()
    row = lax.broadcasted_iota(jnp.int32, (BLK, BLK), 0)
    col = lax.broadcasted_iota(jnp.int32, (BLK, BLK), 1)
    strict = col < row
    incl = col <= row
    tril_incl = jnp.where(incl, 1.0, 0.0).astype(BF16)
    first_row = lax.broadcasted_iota(jnp.int32, (BLK, LANES), 0) == 0
    srow = lax.broadcasted_iota(jnp.int32, (2 * BLK, LANES), 0) < BLK
    slane = lax.broadcasted_iota(jnp.int32, (2 * BLK, LANES), 1) < HEAD_DIM
    own_head = srow == slane
    zeros_blk = jnp.zeros((BLK, BLK), F32)
    zeros_bf = jnp.zeros((2 * BLK, 2 * BLK), BF16)
    eye_bf = jnp.where(lax.broadcasted_iota(jnp.int32, (2 * BLK, 2 * BLK), 0)
                       == lax.broadcasted_iota(jnp.int32, (2 * BLK, 2 * BLK), 1),
                       1.0, 0.0).astype(BF16)

    h_ref[...] = jnp.zeros(h_ref.shape, F32)

    each = lambda f, *lists: [f(*args) for args in zip(*lists)]
    bf = lambda v: v.astype(BF16)
    pairs = range(_N_PAIR)
    lanes = [slice(j * LANES, (j + 1) * LANES) for j in pairs]
    param = lambda ref: [ref[:, sl] for sl in lanes]

    def produce(c, slot):
        row0 = _aligned(c * BLK, BLK)
        rows = pl.ds(row0, BLK)
        prev8 = pl.ds(_aligned(max(row0 - 8, 0) if isinstance(row0, int)
                               else jnp.maximum(row0 - 8, 0), 8), 8)
        has_prev = jnp.where(c > 0, 1.0, 0.0).astype(F32)

        def shifted(ref, mu_ref, sl):
            p = ref[0, rows, sl]
            last = ref[0, prev8, sl][7:8, :] * has_prev
            prev = jnp.where(first_row, last, pltpu.roll(p, 1, 0))
            return p + (prev - p) * mu_ref[:, sl]

        xr = [shifted(pr_ref, mur_ref, sl) for sl in lanes]
        xk = [shifted(pk_ref, muk_ref, sl) for sl in lanes]
        xv = [shifted(pv_ref, muv_ref, sl) for sl in lanes]
        xwa = shifted(pwa_ref, muwa_ref, slice(None))
        xg = shifted(pg_ref, mug_ref, slice(None))
        yield

        lora_all = _dot(bf(jnp.where(head0, jnp.tanh(xwa), xwa)), wwa_ref[...])
        lora = [lora_all[:, 2 * LANES * j:2 * LANES * (j + 1)] for j in pairs]
        gate_all = _dot(bf(jax.nn.sigmoid(xg)), g2_ref[...])
        gate = [gate_all[:, sl] for sl in lanes]
        yield

        def decay_log(lo, w0):
            zw = -(w0 + lo[:, :LANES])
            softplus = jnp.maximum(zw, 0.0) + jnp.log(1.0 + jnp.exp(-jnp.abs(zw)))
            return -jnp.exp(-softplus - 0.5)
        log_decay = each(decay_log, lora, param(w0_ref))
        a_sig = each(lambda lo, a0: jax.nn.sigmoid(a0 + lo[:, LANES:]), lora, param(a0_ref))

        def unit_key(k, k_k):
            kkv = k * k_k
            return kkv / jnp.maximum(jnp.sqrt(_head_sum(kkv * kkv, head0)), 1e-12)
        kk = each(unit_key, xk, param(kk_ref))
        k2 = each(lambda k, a, k_a: k * (1.0 + (a - 1.0) * k_a), xk, a_sig, param(ka_ref))
        b_vec = each(lambda u, a: u * a, kk, a_sig)
        yield

        def cumulative(ld):
            ld_hi = bf(ld)
            ld_lo = bf(ld - ld_hi.astype(F32))
            cum = _dot(tril_incl, jnp.concatenate([ld_hi, ld_lo], axis=1))
            return cum[:, :LANES] + cum[:, LANES:]
        lw = each(cumulative, log_decay)
        yield
        lw_last = [v[BLK - 1:BLK, :] for v in lw]
        w_inv = each(lambda v: jnp.exp(-v), lw)
        w_tail = each(lambda v, last: jnp.exp(last - v), lw, lw_last)
        w_all = each(jnp.exp, lw_last)

        at = each(lambda u, v, ld: -u * jnp.exp(v - ld), kk, lw, log_decay)
        rt = each(lambda r, v: r * jnp.exp(v), xr, lw)
        bt = each(lambda b, w: b * w, b_vec, w_inv)
        kt = each(lambda k, w: k * w, k2, w_inv)
        bh = each(lambda b, w: b * w, b_vec, w_tail)
        kh = each(lambda k, w: k * w, k2, w_tail)
        yield

        def pair_products(a, r, b, k):
            lhs = jnp.concatenate([jnp.where(head0, a, 0.0), jnp.where(head0, r, 0.0),
                                   jnp.where(head0, 0.0, a), jnp.where(head0, 0.0, r)], axis=0)
            return _nt_dot(bf(lhs), bf(jnp.concatenate([b, k], axis=0)))
        pp = each(pair_products, at, rt, bt, kt)
        yield

        def n_matrix(q):
            aab0 = jnp.where(strict, q[0:BLK, :BLK], 0.0)
            aab1 = jnp.where(strict, q[2 * BLK:3 * BLK, :BLK], 0.0)
            return bf(jnp.concatenate([jnp.concatenate([aab0, zeros_blk], axis=1),
                                       jnp.concatenate([zeros_blk, aab1], axis=1)], axis=0))

        def aak_matrix(q):
            return bf(jnp.concatenate([jnp.where(strict, q[0:BLK, BLK:], 0.0),
                                       jnp.where(strict, q[2 * BLK:3 * BLK, BLK:], 0.0)], axis=0))
        akv = _pairwise_dots(each(aak_matrix, pp), each(bf, xv))

        def x_init(a, av):
            return jnp.concatenate([_stack_heads(a, head0), jnp.where(own_head, av, 0.0)], axis=1)

        def read_matrices(q):
            arb = jnp.concatenate([jnp.where(incl, q[BLK:2 * BLK, :BLK], 0.0),
                                   jnp.where(incl, q[3 * BLK:, :BLK], 0.0)], axis=1)
            ark = jnp.concatenate([jnp.where(incl, q[BLK:2 * BLK, BLK:], 0.0),
                                   jnp.where(incl, q[3 * BLK:, BLK:], 0.0)], axis=1)
            return bf(arb), bf(ark)

        for j in pairs:
            slot["n"][j] = n_matrix(pp[j])
            slot["x"][j] = bf(x_init(at[j], akv[j]))
            slot["arb"][j], slot["ark"][j] = read_matrices(pp[j])
        yield
        r_k = param(rk_ref)
        for j in pairs:
            slot["rt"][j] = rt[j]
            slot["bhs"][j] = bf(_stack_heads(bh[j], head0))
            slot["khs"][j] = bf(_stack_heads(kh[j], head0))
            slot["vs"][j] = bf(_stack_heads(xv[j], head0))
            slot["wall"][j] = jnp.broadcast_to(w_all[j], (8, LANES))
            slot["bonus"][j] = _head_sum(xr[j] * k2[j] * r_k[j], head0) * xv[j]
            slot["gate"][j] = gate[j]

    def consume(c, slot):
        n_chunk = _N_PAIR
        n_bf = [slot["n"][j] for j in range(n_chunk)]
        x_bf = [slot["x"][j] for j in range(n_chunk)]

        def live_rows(m, skip):
            if not skip:
                return m
            return jnp.concatenate([m[skip:BLK], m[BLK + skip:]], axis=0)

        def merge_rows(old, new, skip):
            if not skip:
                return new
            live = BLK - skip
            return jnp.concatenate([old[:skip], new[:live], old[BLK:BLK + skip], new[live:]],
                                   axis=0)

        span = 1
        while span < BLK:
            skip = span if span >= 16 else 0
            x_bf = each(lambda xx, nn: merge_rows(
                xx, bf(_dot(live_rows(nn + eye_bf, skip), xx)), skip), x_bf, n_bf)
            span *= 2
            if span < BLK:
                skip = span if span >= 16 else 0
                n_bf = each(lambda nn: merge_rows(
                    zeros_bf, bf(_dot(live_rows(nn, skip), nn)), skip), n_bf)
            yield
        v_s = [slot["vs"][j] for j in range(n_chunk)]

        def read_side(j, xb, vs):
            z = _dot(slot["arb"][j], xb)
            return slot["rt"][j] + z[:, :LANES], z[:, LANES:] + _dot(slot["ark"][j], vs)
        r_eff_y0 = each(read_side, range(n_chunk), x_bf, v_s)
        yield

        def state_side(j, xb, vs):
            d1 = _tn_dot(slot["bhs"][j], xb)
            d2 = _tn_dot(slot["khs"][j], vs)
            w_all = slot["wall"][j][0:1, :]
            return jnp.where(row == col, w_all, 0.0) + d1[:, :LANES], d1[:, LANES:] + d2
        m_eff_g_eff = each(state_side, range(n_chunk), x_bf, v_s)
        rm_lhs = each(lambda ry, mg: bf(jnp.concatenate([ry[0], mg[0]], axis=0)),
                      r_eff_y0, m_eff_g_eff)
        yield

        rm = _pairwise_dots(rm_lhs, [bf(h_ref[j]) for j in pairs])
        y = each(lambda v, ry: v[:BLK] + ry[1], rm, r_eff_y0)
        for j in pairs:
            h_ref[j] = rm[j][BLK:] + m_eff_g_eff[j][1]
        yield
        mean = each(lambda v: _head_sum(v, head0) * (1.0 / HEAD_DIM), y)
        yc = each(lambda v, m: v - m, y, mean)
        var = each(lambda v: _head_sum(v * v, head0) * (1.0 / HEAD_DIM), yc)
        rows = pl.ds(_aligned(c * BLK, BLK), BLK)
        lnx_g, lnx_b = param(lng_ref), param(lnb_ref)
        for j in pairs:
            yn = yc[j] * lax.rsqrt(var[j] + LNX_EPS) * lnx_g[j] + lnx_b[j]
            o_ref[0, rows, lanes[j]] = (yn + slot["bonus"][j]) * slot["gate"][j]

    def alternate(*stages):
        live = list(stages)
        while live:
            for gen in list(live):
                try:
                    next(gen)
                except StopIteration:
                    live.remove(gen)

    names = ("n", "x", "arb", "ark", "rt", "bhs", "khs", "vs", "wall", "bonus", "gate")
    slot_a = dict(zip(names, slots[:len(names)]))
    slot_b = dict(zip(names, slots[len(names):]))
    assert n_chunks % 2 == 0

    alternate(produce(0, slot_a))

    def chunk_pair(i, carry):
        c = 2 * i
        alternate(consume(c, slot_a), produce(c + 1, slot_b))
        alternate(consume(c + 1, slot_b), produce(jnp.minimum(c + 2, n_chunks - 1), slot_a))
        return carry

    lax.fori_loop(0, n_chunks // 2, chunk_pair, 0)


def _wkv_slot_shapes():
    g = _N_PAIR
    return [
        pltpu.VMEM((g, 2 * BLK, 2 * BLK), BF16),
        pltpu.VMEM((g, 2 * BLK, 2 * LANES), BF16),
        pltpu.VMEM((g, BLK, 2 * BLK), BF16),
        pltpu.VMEM((g, BLK, 2 * BLK), BF16),
        pltpu.VMEM((g, BLK, LANES), F32),
        pltpu.VMEM((g, 2 * BLK, LANES), BF16),
        pltpu.VMEM((g, 2 * BLK, LANES), BF16),
        pltpu.VMEM((g, 2 * BLK, LANES), BF16),
        pltpu.VMEM((g, 8, LANES), F32),
        pltpu.VMEM((g, BLK, LANES), F32),
        pltpu.VMEM((g, BLK, LANES), F32),
    ]


def _rwkv(proj, mu, w0, a0, k_k, k_a, r_k, lnx_g, lnx_b, wwa, g2_bf16):
    b, s, _ = proj.shape
    np_ = _N_PAIR
    w = RWKV_WIDTH
    wide = lambda off: pl.BlockSpec((1, s, w), lambda i: (i, 0, off))
    lane = lambda off: pl.BlockSpec((1, s, LANES), lambda i: (i, 0, off))
    vec_w = lambda off: pl.BlockSpec((1, w), lambda i: (0, off))
    vec_l = lambda off: pl.BlockSpec((1, LANES), lambda i: (0, off))
    full = lambda shape: pl.BlockSpec(shape, lambda i: (0,) * len(shape))
    rw0 = _RW0 * LANES // w
    return pl.pallas_call(
        _wkv_kernel,
        grid=(b,),
        in_specs=[
            wide(rw0), wide(rw0 + 1), wide(rw0 + 2),
            lane(_RW0 + 3 * np_), lane(_RW0 + 3 * np_ + 1),
            vec_w(0), vec_w(1), vec_w(2), vec_l(3 * np_), vec_l(3 * np_ + 1),
            vec_w(0), vec_w(0), vec_w(0), vec_w(0), vec_w(0), vec_w(0), vec_w(0),
            full((LANES, np_ * 2 * LANES)),
            full((GATE_LORA, w)),
        ],
        out_specs=pl.BlockSpec((1, s, w), lambda i: (i, 0, 0)),
        out_shape=jax.ShapeDtypeStruct((b, s, RWKV_WIDTH), F32),
        scratch_shapes=[pltpu.VMEM((np_, LANES, LANES), F32)] + 2 * _wkv_slot_shapes(),
        compiler_params=pltpu.CompilerParams(
            dimension_semantics=("parallel",), vmem_limit_bytes=VMEM_LIMIT),
        name="rwkv7",
    )(proj, proj, proj, proj, proj, mu, mu, mu, mu, mu,
      w0, a0, k_k, k_a, r_k, lnx_g, lnx_b, wwa, g2_bf16)


def _out_ffn_kernel(x_ref, attn_ref, rwkv_ref, woa_ref, wor_ref, g2_ref,
                    wg_ref, wu_ref, wd_ref, o_ref):
    x1 = (x_ref[...] + _dot(attn_ref[...].astype(BF16), woa_ref[...])
          + _dot(rwkv_ref[...].astype(BF16), wor_ref[...]))
    ms = jnp.mean(x1 * x1, axis=-1, keepdims=True)
    xn = (x1 * lax.rsqrt(ms + NORM_EPS) * g2_ref[...]).astype(BF16)
    gate = _dot(xn, wg_ref[...])
    up = _dot(xn, wu_ref[...])
    hidden = (gate * jax.nn.sigmoid(gate) * up).astype(BF16)
    o_ref[...] = x1 + _dot(hidden, wd_ref[...])


def _out_ffn(x2d, attn2d, rwkv2d, woa, wor, g2, wg, wu, wd, tm):
    t, d = x2d.shape
    f = wg.shape[1]
    tok = lambda w: pl.BlockSpec((tm, w), lambda i: (i, 0))
    res = lambda shape: pl.BlockSpec(shape, lambda i: (0, 0), pipeline_mode=pl.Buffered(1))
    return pl.pallas_call(
        _out_ffn_kernel,
        grid=(t // tm,),
        in_specs=[
            tok(d), tok(ATTN_WIDTH), tok(RWKV_WIDTH),
            res((ATTN_WIDTH, d)), res((RWKV_WIDTH, d)),
            pl.BlockSpec((1, d), lambda i: (0, 0)),
            res((d, f)), res((d, f)), res((f, d)),
        ],
        out_specs=tok(d),
        out_shape=jax.ShapeDtypeStruct((t, d), F32),
        compiler_params=pltpu.CompilerParams(
            dimension_semantics=("parallel",), vmem_limit_bytes=VMEM_LIMIT),
        name="out_ffn",
    )(x2d, attn2d, rwkv2d, woa, wor, g2, wg, wu, wd)


def _layer(x, norm1_g, w_in, q_norm_g, k_norm_g, attn_out_g, rwkv_mu, w0, w2, a0, a2, g2,
           k_k, k_a, r_k, lnx_g, lnx_b, w_out, norm2_g, w_gate, w_up, w_down):
    b, s, d = x.shape
    assert s % (BLK * max(dil for _, dil in DILATED_PATTERNS)) == 0
    assert all(window // dil == BLK for window, dil in DILATED_PATTERNS)
    t = b * s
    assert t % IN_PROJ_ROWS == 0 and t % FFN_ROWS == 0
    row = lambda v: v.reshape(1, -1).astype(F32)

    x2d = x.reshape(t, d)
    per_head = lambda v: jnp.tile(v.reshape(1, HEAD_DIM), (1, ATTN_HEADS)).astype(F32)
    qk_gain = jnp.concatenate([per_head(q_norm_g) * (HEAD_DIM ** -0.5 * LOG2_E),
                               per_head(k_norm_g)], axis=1)
    proj = _in_proj(x2d, row(norm1_g), w_in.astype(BF16), qk_gain, IN_PROJ_ROWS)
    proj = proj.reshape(b, s, -1)

    slopes = jnp.exp2(-8.0 * jnp.arange(1, ATTN_HEADS + 1, dtype=F32) / ATTN_HEADS)
    attn = _attention(proj, slopes, row(attn_out_g))

    w2p = w2.reshape(DECAY_LORA, _N_PAIR, LANES).transpose(1, 0, 2)
    a2p = a2.reshape(ICLR_LORA, _N_PAIR, LANES).transpose(1, 0, 2)
    zero = jnp.zeros_like(w2p)
    wwa = jnp.concatenate([jnp.concatenate([w2p, zero], axis=2),
                           jnp.concatenate([jnp.zeros_like(a2p), a2p], axis=2)], axis=1)
    wwa = wwa.transpose(1, 0, 2).reshape(LANES, _N_PAIR * 2 * LANES).astype(BF16)
    rwkv = _rwkv(proj, row(rwkv_mu), row(w0), row(a0), row(k_k), row(k_a), row(r_k),
                 row(lnx_g), row(lnx_b), wwa, g2.astype(BF16))

    out = _out_ffn(x2d, attn.reshape(t, ATTN_WIDTH), rwkv.reshape(t, RWKV_WIDTH),
                   w_out[:ATTN_WIDTH].astype(BF16), w_out[ATTN_WIDTH:].astype(BF16),
                   row(norm2_g), w_gate.astype(BF16), w_up.astype(BF16), w_down.astype(BF16),
                   FFN_ROWS)
    return out.reshape(b, s, d)


def kernel(x, norm1_g, w_in, q_norm_g, k_norm_g, attn_out_g, rwkv_mu, w0, w2, a0, a2, g2,
           k_k, k_a, r_k, lnx_g, lnx_b, w_out, norm2_g, w_gate, w_up, w_down):
    h = x
    for layer in range(norm1_g.shape[0]):
        h = _layer(h, norm1_g[layer], w_in[layer], q_norm_g[layer], k_norm_g[layer],
                   attn_out_g[layer], rwkv_mu[layer], w0[layer], w2[layer], a0[layer],
                   a2[layer], g2[layer], k_k[layer], k_a[layer], r_k[layer], lnx_g[layer],
                   lnx_b[layer], w_out[layer], norm2_g[layer], w_gate[layer], w_up[layer],
                   w_down[layer])
    return h
```

```python
import functools

import jax
import jax.numpy as jnp
from jax import lax
from jax.experimental import pallas as pl
from jax.experimental.pallas import tpu as pltpu

F32 = jnp.float32
BF16 = jnp.bfloat16

HEAD_DIM = 64
LANES = 128
ATTN_HEADS = 8
RWKV_HEADS = 8
ATTN_WIDTH = ATTN_HEADS * HEAD_DIM
RWKV_WIDTH = RWKV_HEADS * HEAD_DIM
DECAY_LORA = 64
ICLR_LORA = 64
GATE_LORA = 128
DILATED_PATTERNS = ((128, 1), (512, 4), (2048, 16))
RESIDUES = max(dil for _, dil in DILATED_PATTERNS)
BLK = 128
ATTN_GROUP = 8
ROW_TILE = 512
IN_PROJ_ROWS = 1024
FFN_ROWS = 512
NORM_EPS = 1e-6
LNX_EPS = 64e-5
MASKED = -1e30
LOG2_E = 1.4426950408889634
VMEM_LIMIT = 56 * 1024 * 1024

_QB, _KB, _VB = 0, ATTN_WIDTH // LANES, 2 * ATTN_WIDTH // LANES
_N_PAIR = RWKV_WIDTH // LANES


def _nt_dot(a, b):
    return lax.dot_general(a, b, (((1,), (1,)), ((), ())), preferred_element_type=F32)


def _tn_dot(a, b):
    return lax.dot_general(a, b, (((0,), (0,)), ((), ())), preferred_element_type=F32)


def _dot(a, b):
    return jnp.dot(a, b, preferred_element_type=F32)


def _pairwise_dots(lhs, rhs):
    assert len(lhs) == len(rhs) and len(lhs) % 2 == 0
    out = []
    for j in range(0, len(lhs), 2):
        zero = jnp.zeros_like(rhs[j])
        both = _dot(jnp.concatenate([lhs[j], lhs[j + 1]], axis=1),
                    jnp.concatenate([jnp.concatenate([rhs[j], zero], axis=1),
                                     jnp.concatenate([zero, rhs[j + 1]], axis=1)], axis=0))
        out += [both[:, :LANES], both[:, LANES:]]
    return out


def _aligned(index, multiple):
    if isinstance(index, int):
        return index
    return pl.multiple_of(index, multiple)


def _head0_lanes():
    return lax.broadcasted_iota(jnp.int32, (1, LANES), 1) < HEAD_DIM


def _head_averager():
    i = lax.broadcasted_iota(jnp.int32, (LANES, LANES), 0)
    j = lax.broadcasted_iota(jnp.int32, (LANES, LANES), 1)
    return jnp.where((i < HEAD_DIM) == (j < HEAD_DIM), 1.0 / HEAD_DIM, 0.0).astype(BF16)


def _head_mean_mxu(x, averager):
    return _dot(x.astype(BF16), averager)


def _head_sum(x, head0):
    s0 = jnp.sum(jnp.where(head0, x, 0.0), axis=-1, keepdims=True)
    s1 = jnp.sum(jnp.where(head0, 0.0, x), axis=-1, keepdims=True)
    return jnp.where(head0, s0, s1)


def _stack_heads(x, head0):
    return jnp.concatenate([jnp.where(head0, x, 0.0), jnp.where(head0, 0.0, x)], axis=0)


def _in_proj_kernel(x_ref, *refs):
    class_refs = refs[:RESIDUES]
    g_ref, w_ref, qk_gain_ref, rw_ref, qkv_ref = refs[RESIDUES:]

    def normed(x):
        ms = jnp.mean(x * x, axis=-1, keepdims=True)
        return (x * lax.rsqrt(ms + NORM_EPS) * g_ref[...]).astype(BF16)

    n_qk = qk_gain_ref.shape[1]
    n_attn = qkv_ref.shape[3]
    per_class = qkv_ref.shape[2]
    rw_ref[...] = _dot(normed(x_ref[...]), w_ref[:, n_attn:])
    xg = normed(jnp.concatenate([ref[0] for ref in class_refs], axis=0))
    qk = _dot(xg, w_ref[:, :n_qk])
    val = _dot(xg, w_ref[:, n_qk:n_attn])
    head0 = _head0_lanes()
    for j in range(n_qk // LANES):
        cols = slice(j * LANES, (j + 1) * LANES)
        v = qk[:, cols]
        mean_sq = _head_sum(v * v, head0) * (1.0 / HEAD_DIM)
        v = v * lax.rsqrt(mean_sq + NORM_EPS) * qk_gain_ref[:, cols]
        for r in range(RESIDUES):
            qkv_ref[0, r, :, cols] = v[r * per_class:(r + 1) * per_class]
    for r in range(RESIDUES):
        qkv_ref[0, r, :, n_qk:] = val[r * per_class:(r + 1) * per_class]


def _in_proj(x3d, g, w_bf16, qk_gain, tm):
    b, s, d = x3d.shape
    n = w_bf16.shape[1]
    n_attn = 3 * ATTN_WIDTH
    assert s % tm == 0 and tm % (8 * RESIDUES) == 0
    tiles = s // tm
    by_class = x3d.reshape(b, s // RESIDUES, RESIDUES * d)
    class_spec = lambda r: pl.BlockSpec((1, tm // RESIDUES, d),
                                        lambda i: (i // tiles, i % tiles, r))
    return pl.pallas_call(
        _in_proj_kernel,
        grid=(b * tiles,),
        in_specs=[pl.BlockSpec((tm, d), lambda i: (i, 0))]
        + [class_spec(r) for r in range(RESIDUES)]
        + [
            pl.BlockSpec((1, d), lambda i: (0, 0)),
            pl.BlockSpec((d, n), lambda i: (0, 0), pipeline_mode=pl.Buffered(1)),
            pl.BlockSpec((1, qk_gain.shape[1]), lambda i: (0, 0)),
        ],
        out_specs=[
            pl.BlockSpec((tm, n - n_attn), lambda i: (i, 0)),
            pl.BlockSpec((1, RESIDUES, tm // RESIDUES, n_attn),
                         lambda i: (i // tiles, 0, i % tiles, 0)),
        ],
        out_shape=[
            jax.ShapeDtypeStruct((b * s, n - n_attn), F32),
            jax.ShapeDtypeStruct((b, RESIDUES, s // RESIDUES, n_attn), F32),
        ],
        compiler_params=pltpu.CompilerParams(
            dimension_semantics=("parallel",), vmem_limit_bytes=VMEM_LIMIT),
        name="in_proj",
    )(x3d.reshape(b * s, d), *([by_class] * RESIDUES), g, w_bf16, qk_gain)


def _attn_kernel(slope_ref, q_ref, k_ref, v_ref, go_ref, o_ref,
                 m_ref, l_ref, acc_ref, bias_ref):
    seq = q_ref.shape[1]
    per_class = seq // RESIDUES
    hp = pl.program_id(0)
    head0 = _head0_lanes()
    averager = _head_averager()
    def block_runs(dil, cls, n):
        fold = RESIDUES // dil
        length = BLK // fold
        return [(_aligned((dil * m + cls) * per_class + n * length, 8), length)
                for m in range(fold)]

    def member_of(u, dil):
        fold = RESIDUES // dil
        length = BLK // fold
        return fold * (u & (length - 1)) + (u >> (length.bit_length() - 1))

    def load_rows(ref, runs):
        return jnp.concatenate([ref[0, pl.ds(start, length), :] for start, length in runs], axis=0)

    def store_rows(ref, p, runs, value):
        offset = 0
        for start, length in runs:
            ref[p, pl.ds(start, length), :] = value[offset:offset + length]
            offset += length

    @pl.when(pl.program_id(1) == 0)
    def _():
        qi = lax.broadcasted_iota(jnp.int32, (2 * BLK, 2 * BLK), 0)
        ci = lax.broadcasted_iota(jnp.int32, (2 * BLK, 2 * BLK), 1)
        slope = jnp.where(qi < BLK, slope_ref[2 * hp], slope_ref[2 * hp + 1]) * LOG2_E
        for p, (window, dil) in enumerate(DILATED_PATTERNS):
            steps = window // dil
            back = ((BLK + member_of(qi & (BLK - 1), dil))
                    - ((ci & BLK) + member_of(ci & (BLK - 1), dil)))
            valid = (back >= 0) & (back <= steps)
            bias_ref[p] = jnp.where(valid, -slope * (back.astype(F32) * dil), MASKED)

    ones_keys = jnp.ones((2 * BLK, LANES), BF16)

    def band_blocks(p, dil, blocks):
        each = lambda f, *lists: [f(*args) for args in zip(*lists)]
        bf = lambda v: v.astype(BF16)
        runs = [block_runs(dil, cls, n) for cls, n in blocks]
        has_prevs = [n > 0 for _, n in blocks]

        def keys_values(block, own, has_prev):
            kb = load_rows(k_ref, own)
            vb = load_rows(v_ref, own)
            if has_prev:
                prev = block_runs(dil, block[0], block[1] - 1)
                kb = jnp.concatenate([load_rows(k_ref, prev), kb], axis=0)
                vb = jnp.concatenate([load_rows(v_ref, prev), vb], axis=0)
            return bf(kb), jnp.concatenate([bf(vb), ones_keys[:vb.shape[0]]], axis=1)
        kv = each(keys_values, blocks, runs, has_prevs)
        qs = each(lambda own: bf(_stack_heads(load_rows(q_ref, own), head0)), runs)
        s = each(lambda q, kvb, has_prev:
                 _nt_dot(q, kvb[0]) + (bias_ref[p] if has_prev else bias_ref[p, :, BLK:]),
                 qs, kv, has_prevs)
        m_blk = each(lambda v: jnp.max(v, axis=-1, keepdims=True), s)
        pr = each(lambda v, m: bf(jnp.exp2(v - m)), s, m_blk)
        pv = each(lambda v, kvb: _dot(v, kvb[1]), pr, kv)

        def keep(own, m_b, pv_b):
            store_rows(m_ref, p, own, jnp.where(head0, m_b[:BLK], m_b[BLK:]))
            store_rows(acc_ref, p, own,
                       jnp.where(head0, pv_b[:BLK, :LANES], pv_b[BLK:, :LANES]))
            store_rows(l_ref, p, own, jnp.where(head0, pv_b[:BLK, LANES:], pv_b[BLK:, LANES:]))
        each(keep, runs, m_blk, pv)

    for p, (window, dil) in enumerate(DILATED_PATTERNS):
        n_blk = seq // (BLK * dil)
        if n_blk >= ATTN_GROUP:
            assert n_blk % ATTN_GROUP == 0
            groups = n_blk // ATTN_GROUP

            def residue(r, carry, p=p, dil=dil, groups=groups):
                for grp in range(groups):
                    band_blocks(p, dil, [(r, n) for n in range(grp * ATTN_GROUP,
                                                                (grp + 1) * ATTN_GROUP)])
                return carry
        else:
            assert ATTN_GROUP % n_blk == 0 and dil % (ATTN_GROUP // n_blk) == 0
            per_group = ATTN_GROUP // n_blk

            def residue(i, carry, p=p, dil=dil, n_blk=n_blk, per_group=per_group):
                classes = [i * per_group + g for g in range(per_group)]
                band_blocks(p, dil, [(r, n) for r in classes for n in range(n_blk)])
                return carry
        n_steps = dil if n_blk >= ATTN_GROUP else dil // (ATTN_GROUP // n_blk)
        if n_steps == 1:
            residue(0, 0)
        else:
            lax.fori_loop(0, n_steps, residue, 0)

    def finish(i, carry):
        rows = pl.ds(pl.multiple_of(i * ROW_TILE, ROW_TILE), ROW_TILE)
        n_pat = len(DILATED_PATTERNS)
        m_p = [m_ref[p, rows, :] for p in range(n_pat)]
        m_all = functools.reduce(jnp.maximum, m_p)
        scale = [jnp.exp2(m - m_all) for m in m_p]
        num = sum(sc * acc_ref[p, rows, :] for p, sc in enumerate(scale))
        den = sum(sc * l_ref[p, rows, :] for p, sc in enumerate(scale))
        o = num / den
        ms = _head_mean_mxu(o * o, averager)
        o = o * lax.rsqrt(ms + NORM_EPS) * go_ref[...]
        for c in range(ROW_TILE // per_class):
            cls = i * (ROW_TILE // per_class) + c
            o_ref[0, pl.ds(cls, per_class, stride=RESIDUES), :] = (
                o[c * per_class:(c + 1) * per_class])
        return carry

    assert ROW_TILE % per_class == 0
    lax.fori_loop(0, seq // ROW_TILE, finish, 0)


def _attention(qkv, slopes, go):
    b, s, _ = qkv.shape
    assert BLK & (BLK - 1) == 0 and s % (RESIDUES * BLK) == 0
    proj = qkv
    blk = lambda off: pl.BlockSpec((1, s, LANES), lambda j, i: (i, 0, off + j))
    return pl.pallas_call(
        _attn_kernel,
        grid=(_N_PAIR, b),
        in_specs=[
            pl.BlockSpec(memory_space=pltpu.SMEM),
            blk(_QB), blk(_KB), blk(_VB),
            pl.BlockSpec((1, LANES), lambda j, i: (0, j)),
        ],
        out_specs=pl.BlockSpec((1, s, LANES), lambda j, i: (i, 0, j)),
        out_shape=jax.ShapeDtypeStruct((b, s, ATTN_WIDTH), F32),
        scratch_shapes=[
            pltpu.VMEM((len(DILATED_PATTERNS), s, LANES), F32),
            pltpu.VMEM((len(DILATED_PATTERNS), s, LANES), F32),
            pltpu.VMEM((len(DILATED_PATTERNS), s, LANES), F32),
            pltpu.VMEM((len(DILATED_PATTERNS), 2 * BLK, 2 * BLK), F32),
        ],
        compiler_params=pltpu.CompilerParams(
            dimension_semantics=("arbitrary", "arbitrary"), vmem_limit_bytes=VMEM_LIMIT),
        name="dilated_attn",
    )(slopes, proj, proj, proj, go)


def _wkv_kernel(pr_ref, pk_ref, pv_ref, pwa_ref, pg_ref,
                mur_ref, muk_ref, muv_ref, muwa_ref, mug_ref,
                w0_ref, a0_ref, kk_ref, ka_ref, rk_ref, lng_ref, lnb_ref,
                wwa_ref, g2_ref, o_ref, h_ref, *slots):
    seq = pr_ref.shape[1]
    n_chunks = seq // BLK
    head0 = _head0_lanes()
    row = lax.broadcasted_iota(jnp.int32, (BLK, BLK), 0)
    col = lax.broadcasted_iota(jnp.int32, (BLK, BLK), 1)
    strict = col < row
    incl = col <= row
    tril_incl = jnp.where(incl, 1.0, 0.0).astype(BF16)
    first_row = lax.broadcasted_iota(jnp.int32, (BLK, LANES), 0) == 0
    srow = lax.broadcasted_iota(jnp.int32, (2 * BLK, LANES), 0) < BLK
    slane = lax.broadcasted_iota(jnp.int32, (2 * BLK, LANES), 1) < HEAD_DIM
    own_head = srow == slane
    zeros_blk = jnp.zeros((BLK, BLK), F32)
    zeros_bf = jnp.zeros((2 * BLK, 2 * BLK), BF16)
    eye_bf = jnp.where(lax.broadcasted_iota(jnp.int32, (2 * BLK, 2 * BLK), 0)
                       == lax.broadcasted_iota(jnp.int32, (2 * BLK, 2 * BLK), 1),
                       1.0, 0.0).astype(BF16)

    h_ref[...] = jnp.zeros(h_ref.shape, F32)

    each = lambda f, *lists: [f(*args) for args in zip(*lists)]
    bf = lambda v: v.astype(BF16)
    pairs = range(_N_PAIR)
    lanes = [slice(j * LANES, (j + 1) * LANES) for j in pairs]
    param = lambda ref: [ref[:, sl] for sl in lanes]

    def produce(c, slot):
        row0 = _aligned(c * BLK, BLK)
        rows = pl.ds(row0, BLK)
        prev8 = pl.ds(_aligned(max(row0 - 8, 0) if isinstance(row0, int)
                               else jnp.maximum(row0 - 8, 0), 8), 8)
        has_prev = jnp.where(c > 0, 1.0, 0.0).astype(F32)

        def shifted(ref, mu_ref, sl):
            p = ref[0, rows, sl]
            last = ref[0, prev8, sl][7:8, :] * has_prev
            prev = jnp.where(first_row, last, pltpu.roll(p, 1, 0))
            return p + (prev - p) * mu_ref[:, sl]

        xr = [shifted(pr_ref, mur_ref, sl) for sl in lanes]
        xk = [shifted(pk_ref, muk_ref, sl) for sl in lanes]
        xv = [shifted(pv_ref, muv_ref, sl) for sl in lanes]
        xwa = shifted(pwa_ref, muwa_ref, slice(None))
        xg = shifted(pg_ref, mug_ref, slice(None))
        yield

        lora_all = _dot(bf(jnp.where(head0, jnp.tanh(xwa), xwa)), wwa_ref[...])
        lora = [lora_all[:, 2 * LANES * j:2 * LANES * (j + 1)] for j in pairs]
        gate_all = _dot(bf(jax.nn.sigmoid(xg)), g2_ref[...])
        gate = [gate_all[:, sl] for sl in lanes]
        yield

        def decay_log(lo, w0):
            zw = -(w0 + lo[:, :LANES])
            softplus = jnp.maximum(zw, 0.0) + jnp.log(1.0 + jnp.exp(-jnp.abs(zw)))
            return -jnp.exp(-softplus - 0.5)
        log_decay = each(decay_log, lora, param(w0_ref))
        a_sig = each(lambda lo, a0: jax.nn.sigmoid(a0 + lo[:, LANES:]), lora, param(a0_ref))

        def unit_key(k, k_k):
            kkv = k * k_k
            return kkv / jnp.maximum(jnp.sqrt(_head_sum(kkv * kkv, head0)), 1e-12)
        kk = each(unit_key, xk, param(kk_ref))
        k2 = each(lambda k, a, k_a: k * (1.0 + (a - 1.0) * k_a), xk, a_sig, param(ka_ref))
        b_vec = each(lambda u, a: u * a, kk, a_sig)
        yield

        def cumulative(ld):
            ld_hi = bf(ld)
            ld_lo = bf(ld - ld_hi.astype(F32))
            cum = _dot(tril_incl, jnp.concatenate([ld_hi, ld_lo], axis=1))
            return cum[:, :LANES] + cum[:, LANES:]
        lw = each(cumulative, log_decay)
        yield
        lw_last = [v[BLK - 1:BLK, :] for v in lw]
        w_inv = each(lambda v: jnp.exp(-v), lw)
        w_tail = each(lambda v, last: jnp.exp(last - v), lw, lw_last)
        w_all = each(jnp.exp, lw_last)

        at = each(lambda u, v, ld: -u * jnp.exp(v - ld), kk, lw, log_decay)
        rt = each(lambda r, v: r * jnp.exp(v), xr, lw)
        bt = each(lambda b, w: b * w, b_vec, w_inv)
        kt = each(lambda k, w: k * w, k2, w_inv)
        bh = each(lambda b, w: b * w, b_vec, w_tail)
        kh = each(lambda k, w: k * w, k2, w_tail)
        yield

        def pair_products(a, r, b, k):
            lhs = jnp.concatenate([jnp.where(head0, a, 0.0), jnp.where(head0, r, 0.0),
                                   jnp.where(head0, 0.0, a), jnp.where(head0, 0.0, r)], axis=0)
            return _nt_dot(bf(lhs), bf(jnp.concatenate([b, k], axis=0)))
        pp = each(pair_products, at, rt, bt, kt)
        yield

        def n_matrix(q):
            aab0 = jnp.where(strict, q[0:BLK, :BLK], 0.0)
            aab1 = jnp.where(strict, q[2 * BLK:3 * BLK, :BLK], 0.0)
            return bf(jnp.concatenate([jnp.concatenate([aab0, zeros_blk], axis=1),
                                       jnp.concatenate([zeros_blk, aab1], axis=1)], axis=0))

        def aak_matrix(q):
            return bf(jnp.concatenate([jnp.where(strict, q[0:BLK, BLK:], 0.0),
                                       jnp.where(strict, q[2 * BLK:3 * BLK, BLK:], 0.0)], axis=0))
        akv = _pairwise_dots(each(aak_matrix, pp), each(bf, xv))

        def x_init(a, av):
            return jnp.concatenate([_stack_heads(a, head0), jnp.where(own_head, av, 0.0)], axis=1)

        def read_matrices(q):
            arb = jnp.concatenate([jnp.where(incl, q[BLK:2 * BLK, :BLK], 0.0),
                                   jnp.where(incl, q[3 * BLK:, :BLK], 0.0)], axis=1)
            ark = jnp.concatenate([jnp.where(incl, q[BLK:2 * BLK, BLK:], 0.0),
                                   jnp.where(incl, q[3 * BLK:, BLK:], 0.0)], axis=1)
            return bf(arb), bf(ark)

        for j in pairs:
            slot["n"][j] = n_matrix(pp[j])
            slot["x"][j] = bf(x_init(at[j], akv[j]))
            slot["arb"][j], slot["ark"][j] = read_matrices(pp[j])
        yield
        r_k = param(rk_ref)
        for j in pairs:
            slot["rt"][j] = rt[j]
            slot["bhs"][j] = bf(_stack_heads(bh[j], head0))
            slot["khs"][j] = bf(_stack_heads(kh[j], head0))
            slot["vs"][j] = bf(_stack_heads(xv[j], head0))
            slot["wall"][j] = jnp.broadcast_to(w_all[j], (8, LANES))
            slot["bonus"][j] = _head_sum(xr[j] * k2[j] * r_k[j], head0) * xv[j]
            slot["gate"][j] = gate[j]

    def consume(c, slot):
        n_chunk = _N_PAIR
        n_bf = [slot["n"][j] for j in range(n_chunk)]
        x_bf = [slot["x"][j] for j in range(n_chunk)]

        def live_rows(m, skip):
            if not skip:
                return m
            return jnp.concatenate([m[skip:BLK], m[BLK + skip:]], axis=0)

        def merge_rows(old, new, skip):
            if not skip:
                return new
            live = BLK - skip
            return jnp.concatenate([old[:skip], new[:live], old[BLK:BLK + skip], new[live:]],
                                   axis=0)

        span = 1
        while span < BLK:
            skip = span if span >= 16 else 0
            x_bf = each(lambda xx, nn: merge_rows(
                xx, bf(_dot(live_rows(nn + eye_bf, skip), xx)), skip), x_bf, n_bf)
            span *= 2
            if span < BLK:
                skip = span if span >= 16 else 0
                n_bf = each(lambda nn: merge_rows(
                    zeros_bf, bf(_dot(live_rows(nn, skip), nn)), skip), n_bf)
            yield
        v_s = [slot["vs"][j] for j in range(n_chunk)]

        def read_side(j, xb, vs):
            z = _dot(slot["arb"][j], xb)
            return slot["rt"][j] + z[:, :LANES], z[:, LANES:] + _dot(slot["ark"][j], vs)
        r_eff_y0 = each(read_side, range(n_chunk), x_bf, v_s)
        yield

        def state_side(j, xb, vs):
            d1 = _tn_dot(slot["bhs"][j], xb)
            d2 = _tn_dot(slot["khs"][j], vs)
            w_all = slot["wall"][j][0:1, :]
            return jnp.where(row == col, w_all, 0.0) + d1[:, :LANES], d1[:, LANES:] + d2
        m_eff_g_eff = each(state_side, range(n_chunk), x_bf, v_s)
        rm_lhs = each(lambda ry, mg: bf(jnp.concatenate([ry[0], mg[0]], axis=0)),
                      r_eff_y0, m_eff_g_eff)
        yield

        rm = _pairwise_dots(rm_lhs, [bf(h_ref[j]) for j in pairs])
        y = each(lambda v, ry: v[:BLK] + ry[1], rm, r_eff_y0)
        for j in pairs:
            h_ref[j] = rm[j][BLK:] + m_eff_g_eff[j][1]
        yield
        mean = each(lambda v: _head_sum(v, head0) * (1.0 / HEAD_DIM), y)
        yc = each(lambda v, m: v - m, y, mean)
        var = each(lambda v: _head_sum(v * v, head0) * (1.0 / HEAD_DIM), yc)
        rows = pl.ds(_aligned(c * BLK, BLK), BLK)
        lnx_g, lnx_b = param(lng_ref), param(lnb_ref)
        for j in pairs:
            yn = yc[j] * lax.rsqrt(var[j] + LNX_EPS) * lnx_g[j] + lnx_b[j]
            o_ref[0, rows, lanes[j]] = (yn + slot["bonus"][j]) * slot["gate"][j]

    def alternate(*stages):
        live = list(stages)
        while live:
            for gen in list(live):
                try:
                    next(gen)
                except StopIteration:
                    live.remove(gen)

    names = ("n", "x", "arb", "ark", "rt", "bhs", "khs", "vs", "wall", "bonus", "gate")
    slot_a = dict(zip(names, slots[:len(names)]))
    slot_b = dict(zip(names, slots[len(names):]))
    assert n_chunks % 2 == 0

    alternate(produce(0, slot_a))

    def chunk_pair(i, carry):
        c = 2 * i
        alternate(consume(c, slot_a), produce(c + 1, slot_b))
        alternate(consume(c + 1, slot_b), produce(jnp.minimum(c + 2, n_chunks - 1), slot_a))
        return carry

    lax.fori_loop(0, n_chunks // 2, chunk_pair, 0)


def _wkv_slot_shapes():
    g = _N_PAIR
    return [
        pltpu.VMEM((g, 2 * BLK, 2 * BLK), BF16),
        pltpu.VMEM((g, 2 * BLK, 2 * LANES), BF16),
        pltpu.VMEM((g, BLK, 2 * BLK), BF16),
        pltpu.VMEM((g, BLK, 2 * BLK), BF16),
        pltpu.VMEM((g, BLK, LANES), F32),
        pltpu.VMEM((g, 2 * BLK, LANES), BF16),
        pltpu.VMEM((g, 2 * BLK, LANES), BF16),
        pltpu.VMEM((g, 2 * BLK, LANES), BF16),
        pltpu.VMEM((g, 8, LANES), F32),
        pltpu.VMEM((g, BLK, LANES), F32),
        pltpu.VMEM((g, BLK, LANES), F32),
    ]


def _rwkv(proj, mu, w0, a0, k_k, k_a, r_k, lnx_g, lnx_b, wwa, g2_bf16):
    b, s, _ = proj.shape
    np_ = _N_PAIR
    w = RWKV_WIDTH
    wide = lambda off: pl.BlockSpec((1, s, w), lambda i: (i, 0, off))
    lane = lambda off: pl.BlockSpec((1, s, LANES), lambda i: (i, 0, off))
    vec_w = lambda off: pl.BlockSpec((1, w), lambda i: (0, off))
    vec_l = lambda off: pl.BlockSpec((1, LANES), lambda i: (0, off))
    full = lambda shape: pl.BlockSpec(shape, lambda i: (0,) * len(shape))
    return pl.pallas_call(
        _wkv_kernel,
        grid=(b,),
        in_specs=[
            wide(0), wide(1), wide(2),
            lane(3 * np_), lane(3 * np_ + 1),
            vec_w(0), vec_w(1), vec_w(2), vec_l(3 * np_), vec_l(3 * np_ + 1),
            vec_w(0), vec_w(0), vec_w(0), vec_w(0), vec_w(0), vec_w(0), vec_w(0),
            full((LANES, np_ * 2 * LANES)),
            full((GATE_LORA, w)),
        ],
        out_specs=pl.BlockSpec((1, s, w), lambda i: (i, 0, 0)),
        out_shape=jax.ShapeDtypeStruct((b, s, RWKV_WIDTH), F32),
        scratch_shapes=[pltpu.VMEM((np_, LANES, LANES), F32)] + 2 * _wkv_slot_shapes(),
        compiler_params=pltpu.CompilerParams(
            dimension_semantics=("parallel",), vmem_limit_bytes=VMEM_LIMIT),
        name="rwkv7",
    )(proj, proj, proj, proj, proj, mu, mu, mu, mu, mu,
      w0, a0, k_k, k_a, r_k, lnx_g, lnx_b, wwa, g2_bf16)


def _out_ffn_kernel(x_ref, attn_ref, rwkv_ref, woa_ref, wor_ref, g2_ref,
                    wg_ref, wu_ref, wd_ref, o_ref):
    x1 = (x_ref[...] + _dot(attn_ref[...].astype(BF16), woa_ref[...])
          + _dot(rwkv_ref[...].astype(BF16), wor_ref[...]))
    ms = jnp.mean(x1 * x1, axis=-1, keepdims=True)
    xn = (x1 * lax.rsqrt(ms + NORM_EPS) * g2_ref[...]).astype(BF16)
    gate = _dot(xn, wg_ref[...])
    up = _dot(xn, wu_ref[...])
    hidden = (gate * jax.nn.sigmoid(gate) * up).astype(BF16)
    o_ref[...] = x1 + _dot(hidden, wd_ref[...])


def _out_ffn(x2d, attn2d, rwkv2d, woa, wor, g2, wg, wu, wd, tm):
    t, d = x2d.shape
    f = wg.shape[1]
    tok = lambda w: pl.BlockSpec((tm, w), lambda i: (i, 0))
    res = lambda shape: pl.BlockSpec(shape, lambda i: (0, 0), pipeline_mode=pl.Buffered(1))
    return pl.pallas_call(
        _out_ffn_kernel,
        grid=(t // tm,),
        in_specs=[
            tok(d), tok(ATTN_WIDTH), tok(RWKV_WIDTH),
            res((ATTN_WIDTH, d)), res((RWKV_WIDTH, d)),
            pl.BlockSpec((1, d), lambda i: (0, 0)),
            res((d, f)), res((d, f)), res((f, d)),
        ],
        out_specs=tok(d),
        out_shape=jax.ShapeDtypeStruct((t, d), F32),
        compiler_params=pltpu.CompilerParams(
            dimension_semantics=("parallel",), vmem_limit_bytes=VMEM_LIMIT),
        name="out_ffn",
    )(x2d, attn2d, rwkv2d, woa, wor, g2, wg, wu, wd)


def _layer(x, norm1_g, w_in, q_norm_g, k_norm_g, attn_out_g, rwkv_mu, w0, w2, a0, a2, g2,
           k_k, k_a, r_k, lnx_g, lnx_b, w_out, norm2_g, w_gate, w_up, w_down):
    b, s, d = x.shape
    assert s % (BLK * max(dil for _, dil in DILATED_PATTERNS)) == 0
    assert all(window // dil == BLK for window, dil in DILATED_PATTERNS)
    t = b * s
    assert t % IN_PROJ_ROWS == 0 and t % FFN_ROWS == 0
    row = lambda v: v.reshape(1, -1).astype(F32)

    x2d = x.reshape(t, d)
    per_head = lambda v: jnp.tile(v.reshape(1, HEAD_DIM), (1, ATTN_HEADS)).astype(F32)
    qk_gain = jnp.concatenate([per_head(q_norm_g) * (HEAD_DIM ** -0.5 * LOG2_E),
                               per_head(k_norm_g)], axis=1)
    proj, qkv = _in_proj(x, row(norm1_g), w_in.astype(BF16), qk_gain, IN_PROJ_ROWS)
    proj = proj.reshape(b, s, -1)

    slopes = jnp.exp2(-8.0 * jnp.arange(1, ATTN_HEADS + 1, dtype=F32) / ATTN_HEADS)
    attn = _attention(qkv.reshape(b, s, -1), slopes, row(attn_out_g))

    w2p = w2.reshape(DECAY_LORA, _N_PAIR, LANES).transpose(1, 0, 2)
    a2p = a2.reshape(ICLR_LORA, _N_PAIR, LANES).transpose(1, 0, 2)
    zero = jnp.zeros_like(w2p)
    wwa = jnp.concatenate([jnp.concatenate([w2p, zero], axis=2),
                           jnp.concatenate([jnp.zeros_like(a2p), a2p], axis=2)], axis=1)
    wwa = wwa.transpose(1, 0, 2).reshape(LANES, _N_PAIR * 2 * LANES).astype(BF16)
    rwkv = _rwkv(proj, row(rwkv_mu), row(w0), row(a0), row(k_k), row(k_a), row(r_k),
                 row(lnx_g), row(lnx_b), wwa, g2.astype(BF16))

    out = _out_ffn(x2d, attn.reshape(t, ATTN_WIDTH), rwkv.reshape(t, RWKV_WIDTH),
                   w_out[:ATTN_WIDTH].astype(BF16), w_out[ATTN_WIDTH:].astype(BF16),
                   row(norm2_g), w_gate.astype(BF16), w_up.astype(BF16), w_down.astype(BF16),
                   FFN_ROWS)
    return out.reshape(b, s, d)


def kernel(x, norm1_g, w_in, q_norm_g, k_norm_g, attn_out_g, rwkv_mu, w0, w2, a0, a2, g2,
           k_k, k_a, r_k, lnx_g, lnx_b, w_out, norm2_g, w_gate, w_up, w_down):
    h = x
    for layer in range(norm1_g.shape[0]):
        h = _layer(h, norm1_g[layer], w_in[layer], q_norm_g[layer], k_norm_g[layer],
                   attn_out_g[layer], rwkv_mu[layer], w0[layer], w2[layer], a0[layer],
                   a2[layer], g2[layer], k_k[layer], k_a[layer], r_k[layer], lnx_g[layer],
                   lnx_b[layer], w_out[layer], norm2_g[layer], w_gate[layer], w_up[layer],
                   w_down[layer])
    return h
```

```python
import functools

import jax
import jax.numpy as jnp
from jax import lax
from jax.experimental import pallas as pl
from jax.experimental.pallas import tpu as pltpu

F32 = jnp.float32
BF16 = jnp.bfloat16

HEAD_DIM = 64
LANES = 128
ATTN_HEADS = 8
RWKV_HEADS = 8
ATTN_WIDTH = ATTN_HEADS * HEAD_DIM
RWKV_WIDTH = RWKV_HEADS * HEAD_DIM
DECAY_LORA = 64
ICLR_LORA = 64
GATE_LORA = 128
DILATED_PATTERNS = ((128, 1), (512, 4), (2048, 16))
RESIDUES = max(dil for _, dil in DILATED_PATTERNS)
BLK = 128
ATTN_GROUP = 8
ROW_TILE = 512
IN_PROJ_ROWS = 512
STAGE_PITCH = 24
FFN_ROWS = 512
NORM_EPS = 1e-6
LNX_EPS = 64e-5
MASKED = -1e30
LOG2_E = 1.4426950408889634
VMEM_LIMIT = 56 * 1024 * 1024

_QB, _KB, _VB = 0, ATTN_WIDTH // LANES, 2 * ATTN_WIDTH // LANES
_N_PAIR = RWKV_WIDTH // LANES


def _nt_dot(a, b):
    return lax.dot_general(a, b, (((1,), (1,)), ((), ())), preferred_element_type=F32)


def _tn_dot(a, b):
    return lax.dot_general(a, b, (((0,), (0,)), ((), ())), preferred_element_type=F32)


def _dot(a, b):
    return jnp.dot(a, b, preferred_element_type=F32)


def _pairwise_dots(lhs, rhs):
    assert len(lhs) == len(rhs) and len(lhs) % 2 == 0
    out = []
    for j in range(0, len(lhs), 2):
        zero = jnp.zeros_like(rhs[j])
        both = _dot(jnp.concatenate([lhs[j], lhs[j + 1]], axis=1),
                    jnp.concatenate([jnp.concatenate([rhs[j], zero], axis=1),
                                     jnp.concatenate([zero, rhs[j + 1]], axis=1)], axis=0))
        out += [both[:, :LANES], both[:, LANES:]]
    return out


def _aligned(index, multiple):
    if isinstance(index, int):
        return index
    return pl.multiple_of(index, multiple)


def _head0_lanes():
    return lax.broadcasted_iota(jnp.int32, (1, LANES), 1) < HEAD_DIM


def _head_averager():
    i = lax.broadcasted_iota(jnp.int32, (LANES, LANES), 0)
    j = lax.broadcasted_iota(jnp.int32, (LANES, LANES), 1)
    return jnp.where((i < HEAD_DIM) == (j < HEAD_DIM), 1.0 / HEAD_DIM, 0.0).astype(BF16)


def _head_mean_mxu(x, averager):
    return _dot(x.astype(BF16), averager)


def _head_sum(x, head0):
    s0 = jnp.sum(jnp.where(head0, x, 0.0), axis=-1, keepdims=True)
    s1 = jnp.sum(jnp.where(head0, 0.0, x), axis=-1, keepdims=True)
    return jnp.where(head0, s0, s1)


def _stack_heads(x, head0):
    return jnp.concatenate([jnp.where(head0, x, 0.0), jnp.where(head0, 0.0, x)], axis=0)


def _in_proj_kernel(x_ref, g_ref, w_ref, qk_gain_ref, rw_ref, qkv_ref, stage_ref):
    x = x_ref[...]
    ms = jnp.mean(x * x, axis=-1, keepdims=True)
    xn = (x * lax.rsqrt(ms + NORM_EPS) * g_ref[...]).astype(BF16)
    n_qk = qk_gain_ref.shape[1]
    n_attn = qkv_ref.shape[3]
    per_class = qkv_ref.shape[2]
    qk = _dot(xn, w_ref[:, :n_qk])
    val = _dot(xn, w_ref[:, n_qk:n_attn])
    head0 = _head0_lanes()
    for j in range(n_attn // LANES):
        cols = slice(j * LANES, (j + 1) * LANES)
        if j < n_qk // LANES:
            v = qk[:, cols]
            mean_sq = _head_sum(v * v, head0) * (1.0 / HEAD_DIM)
            v = v * lax.rsqrt(mean_sq + NORM_EPS) * qk_gain_ref[:, cols]
        else:
            v = val[:, j * LANES - n_qk:(j + 1) * LANES - n_qk]
        for i in range(per_class):
            stage_ref[j, pl.ds(i * STAGE_PITCH, RESIDUES), :] = v[i * RESIDUES:(i + 1) * RESIDUES]
        for r in range(RESIDUES):
            qkv_ref[0, r, :, cols] = stage_ref[j, pl.ds(r, per_class, stride=STAGE_PITCH), :]
    rw_ref[...] = _dot(xn, w_ref[:, n_attn:])


def _in_proj(x3d, g, w_bf16, qk_gain, tm):
    b, s, d = x3d.shape
    n = w_bf16.shape[1]
    n_attn = 3 * ATTN_WIDTH
    assert s % tm == 0 and tm % (8 * RESIDUES) == 0
    tiles = s // tm
    return pl.pallas_call(
        _in_proj_kernel,
        grid=(b * tiles,),
        in_specs=[
            pl.BlockSpec((tm, d), lambda i: (i, 0)),
            pl.BlockSpec((1, d), lambda i: (0, 0)),
            pl.BlockSpec((d, n), lambda i: (0, 0), pipeline_mode=pl.Buffered(1)),
            pl.BlockSpec((1, qk_gain.shape[1]), lambda i: (0, 0)),
        ],
        out_specs=[
            pl.BlockSpec((tm, n - n_attn), lambda i: (i, 0)),
            pl.BlockSpec((1, RESIDUES, tm // RESIDUES, n_attn),
                         lambda i: (i // tiles, 0, i % tiles, 0)),
        ],
        out_shape=[
            jax.ShapeDtypeStruct((b * s, n - n_attn), F32),
            jax.ShapeDtypeStruct((b, RESIDUES, s // RESIDUES, n_attn), F32),
        ],
        scratch_shapes=[
            pltpu.VMEM((n_attn // LANES, tm // RESIDUES * STAGE_PITCH, LANES), F32)],
        compiler_params=pltpu.CompilerParams(
            dimension_semantics=("parallel",), vmem_limit_bytes=VMEM_LIMIT),
        name="in_proj",
    )(x3d.reshape(b * s, d), g, w_bf16, qk_gain)


def _attn_kernel(slope_ref, q_ref, k_ref, v_ref, go_ref, o_ref,
                 m_ref, l_ref, acc_ref, bias_ref):
    seq = q_ref.shape[1]
    per_class = seq // RESIDUES
    hp = pl.program_id(0)
    head0 = _head0_lanes()
    averager = _head_averager()
    def block_runs(dil, cls, n):
        fold = RESIDUES // dil
        length = BLK // fold
        return [(_aligned((dil * m + cls) * per_class + n * length, 8), length)
                for m in range(fold)]

    def member_of(u, dil):
        fold = RESIDUES // dil
        length = BLK // fold
        return fold * (u & (length - 1)) + (u >> (length.bit_length() - 1))

    def load_rows(ref, runs):
        return jnp.concatenate([ref[0, pl.ds(start, length), :] for start, length in runs], axis=0)

    def store_rows(ref, p, runs, value):
        offset = 0
        for start, length in runs:
            ref[p, pl.ds(start, length), :] = value[offset:offset + length]
            offset += length

    @pl.when(pl.program_id(1) == 0)
    def _():
        qi = lax.broadcasted_iota(jnp.int32, (2 * BLK, 2 * BLK), 0)
        ci = lax.broadcasted_iota(jnp.int32, (2 * BLK, 2 * BLK), 1)
        slope = jnp.where(qi < BLK, slope_ref[2 * hp], slope_ref[2 * hp + 1]) * LOG2_E
        for p, (window, dil) in enumerate(DILATED_PATTERNS):
            steps = window // dil
            back = ((BLK + member_of(qi & (BLK - 1), dil))
                    - ((ci & BLK) + member_of(ci & (BLK - 1), dil)))
            valid = (back >= 0) & (back <= steps)
            bias_ref[p] = jnp.where(valid, -slope * (back.astype(F32) * dil), MASKED)

    ones_keys = jnp.ones((2 * BLK, LANES), BF16)

    def band_blocks(p, dil, blocks):
        each = lambda f, *lists: [f(*args) for args in zip(*lists)]
        bf = lambda v: v.astype(BF16)
        runs = [block_runs(dil, cls, n) for cls, n in blocks]
        has_prevs = [n > 0 for _, n in blocks]

        def keys_values(block, own, has_prev):
            kb = load_rows(k_ref, own)
            vb = load_rows(v_ref, own)
            if has_prev:
                prev = block_runs(dil, block[0], block[1] - 1)
                kb = jnp.concatenate([load_rows(k_ref, prev), kb], axis=0)
                vb = jnp.concatenate([load_rows(v_ref, prev), vb], axis=0)
            return bf(kb), jnp.concatenate([bf(vb), ones_keys[:vb.shape[0]]], axis=1)
        kv = each(keys_values, blocks, runs, has_prevs)
        qs = each(lambda own: bf(_stack_heads(load_rows(q_ref, own), head0)), runs)
        s = each(lambda q, kvb, has_prev:
                 _nt_dot(q, kvb[0]) + (bias_ref[p] if has_prev else bias_ref[p, :, BLK:]),
                 qs, kv, has_prevs)
        m_blk = each(lambda v: jnp.max(v, axis=-1, keepdims=True), s)
        pr = each(lambda v, m: bf(jnp.exp2(v - m)), s, m_blk)
        pv = each(lambda v, kvb: _dot(v, kvb[1]), pr, kv)

        def keep(own, m_b, pv_b):
            store_rows(m_ref, p, own, jnp.where(head0, m_b[:BLK], m_b[BLK:]))
            store_rows(acc_ref, p, own,
                       jnp.where(head0, pv_b[:BLK, :LANES], pv_b[BLK:, :LANES]))
            store_rows(l_ref, p, own, jnp.where(head0, pv_b[:BLK, LANES:], pv_b[BLK:, LANES:]))
        each(keep, runs, m_blk, pv)

    for p, (window, dil) in enumerate(DILATED_PATTERNS):
        n_blk = seq // (BLK * dil)
        if n_blk >= ATTN_GROUP:
            assert n_blk % ATTN_GROUP == 0
            groups = n_blk // ATTN_GROUP

            def residue(r, carry, p=p, dil=dil, groups=groups):
                for grp in range(groups):
                    band_blocks(p, dil, [(r, n) for n in range(grp * ATTN_GROUP,
                                                                (grp + 1) * ATTN_GROUP)])
                return carry
        else:
            assert ATTN_GROUP % n_blk == 0 and dil % (ATTN_GROUP // n_blk) == 0
            per_group = ATTN_GROUP // n_blk

            def residue(i, carry, p=p, dil=dil, n_blk=n_blk, per_group=per_group):
                classes = [i * per_group + g for g in range(per_group)]
                band_blocks(p, dil, [(r, n) for r in classes for n in range(n_blk)])
                return carry
        n_steps = dil if n_blk >= ATTN_GROUP else dil // (ATTN_GROUP // n_blk)
        if n_steps == 1:
            residue(0, 0)
        else:
            lax.fori_loop(0, n_steps, residue, 0)

    def finish(i, carry):
        rows = pl.ds(pl.multiple_of(i * ROW_TILE, ROW_TILE), ROW_TILE)
        n_pat = len(DILATED_PATTERNS)
        m_p = [m_ref[p, rows, :] for p in range(n_pat)]
        m_all = functools.reduce(jnp.maximum, m_p)
        scale = [jnp.exp2(m - m_all) for m in m_p]
        num = sum(sc * acc_ref[p, rows, :] for p, sc in enumerate(scale))
        den = sum(sc * l_ref[p, rows, :] for p, sc in enumerate(scale))
        o = num / den
        ms = _head_mean_mxu(o * o, averager)
        o = o * lax.rsqrt(ms + NORM_EPS) * go_ref[...]
        for c in range(ROW_TILE // per_class):
            cls = i * (ROW_TILE // per_class) + c
            o_ref[0, pl.ds(cls, per_class, stride=RESIDUES), :] = (
                o[c * per_class:(c + 1) * per_class])
        return carry

    assert ROW_TILE % per_class == 0
    lax.fori_loop(0, seq // ROW_TILE, finish, 0)


def _attention(qkv, slopes, go):
    b, s, _ = qkv.shape
    assert BLK & (BLK - 1) == 0 and s % (RESIDUES * BLK) == 0
    proj = qkv
    blk = lambda off: pl.BlockSpec((1, s, LANES), lambda j, i: (i, 0, off + j))
    return pl.pallas_call(
        _attn_kernel,
        grid=(_N_PAIR, b),
        in_specs=[
            pl.BlockSpec(memory_space=pltpu.SMEM),
            blk(_QB), blk(_KB), blk(_VB),
            pl.BlockSpec((1, LANES), lambda j, i: (0, j)),
        ],
        out_specs=pl.BlockSpec((1, s, LANES), lambda j, i: (i, 0, j)),
        out_shape=jax.ShapeDtypeStruct((b, s, ATTN_WIDTH), F32),
        scratch_shapes=[
            pltpu.VMEM((len(DILATED_PATTERNS), s, LANES), F32),
            pltpu.VMEM((len(DILATED_PATTERNS), s, LANES), F32),
            pltpu.VMEM((len(DILATED_PATTERNS), s, LANES), F32),
            pltpu.VMEM((len(DILATED_PATTERNS), 2 * BLK, 2 * BLK), F32),
        ],
        compiler_params=pltpu.CompilerParams(
            dimension_semantics=("arbitrary", "arbitrary"), vmem_limit_bytes=VMEM_LIMIT),
        name="dilated_attn",
    )(slopes, proj, proj, proj, go)


def _wkv_kernel(pr_ref, pk_ref, pv_ref, pwa_ref, pg_ref,
                mur_ref, muk_ref, muv_ref, muwa_ref, mug_ref,
                w0_ref, a0_ref, kk_ref, ka_ref, rk_ref, lng_ref, lnb_ref,
                wwa_ref, g2_ref, o_ref, h_ref, *slots):
    seq = pr_ref.shape[1]
    n_chunks = seq // BLK
    head0 = _head0_lanes()
    row = lax.broadcasted_iota(jnp.int32, (BLK, BLK), 0)
    col = lax.broadcasted_iota(jnp.int32, (BLK, BLK), 1)
    strict = col < row
    incl = col <= row
    tril_incl = jnp.where(incl, 1.0, 0.0).astype(BF16)
    first_row = lax.broadcasted_iota(jnp.int32, (BLK, LANES), 0) == 0
    srow = lax.broadcasted_iota(jnp.int32, (2 * BLK, LANES), 0) < BLK
    slane = lax.broadcasted_iota(jnp.int32, (2 * BLK, LANES), 1) < HEAD_DIM
    own_head = srow == slane
    zeros_blk = jnp.zeros((BLK, BLK), F32)
    zeros_bf = jnp.zeros((2 * BLK, 2 * BLK), BF16)
    eye_bf = jnp.where(lax.broadcasted_iota(jnp.int32, (2 * BLK, 2 * BLK), 0)
                       == lax.broadcasted_iota(jnp.int32, (2 * BLK, 2 * BLK), 1),
                       1.0, 0.0).astype(BF16)

    h_ref[...] = jnp.zeros(h_ref.shape, F32)

    each = lambda f, *lists: [f(*args) for args in zip(*lists)]
    bf = lambda v: v.astype(BF16)
    pairs = range(_N_PAIR)
    lanes = [slice(j * LANES, (j + 1) * LANES) for j in pairs]
    param = lambda ref: [ref[:, sl] for sl in lanes]

    def produce(c, slot):
        row0 = _aligned(c * BLK, BLK)
        rows = pl.ds(row0, BLK)
        prev8 = pl.ds(_aligned(max(row0 - 8, 0) if isinstance(row0, int)
                               else jnp.maximum(row0 - 8, 0), 8), 8)
        has_prev = jnp.where(c > 0, 1.0, 0.0).astype(F32)

        def shifted(ref, mu_ref, sl):
            p = ref[0, rows, sl]
            last = ref[0, prev8, sl][7:8, :] * has_prev
            prev = jnp.where(first_row, last, pltpu.roll(p, 1, 0))
            return p + (prev - p) * mu_ref[:, sl]

        xr = [shifted(pr_ref, mur_ref, sl) for sl in lanes]
        xk = [shifted(pk_ref, muk_ref, sl) for sl in lanes]
        xv = [shifted(pv_ref, muv_ref, sl) for sl in lanes]
        xwa = shifted(pwa_ref, muwa_ref, slice(None))
        xg = shifted(pg_ref, mug_ref, slice(None))
        yield

        lora_all = _dot(bf(jnp.where(head0, jnp.tanh(xwa), xwa)), wwa_ref[...])
        lora = [lora_all[:, 2 * LANES * j:2 * LANES * (j + 1)] for j in pairs]
        gate_all = _dot(bf(jax.nn.sigmoid(xg)), g2_ref[...])
        gate = [gate_all[:, sl] for sl in lanes]
        yield

        def decay_log(lo, w0):
            zw = -(w0 + lo[:, :LANES])
            softplus = jnp.maximum(zw, 0.0) + jnp.log(1.0 + jnp.exp(-jnp.abs(zw)))
            return -jnp.exp(-softplus - 0.5)
        log_decay = each(decay_log, lora, param(w0_ref))
        a_sig = each(lambda lo, a0: jax.nn.sigmoid(a0 + lo[:, LANES:]), lora, param(a0_ref))

        def unit_key(k, k_k):
            kkv = k * k_k
            return kkv / jnp.maximum(jnp.sqrt(_head_sum(kkv * kkv, head0)), 1e-12)
        kk = each(unit_key, xk, param(kk_ref))
        k2 = each(lambda k, a, k_a: k * (1.0 + (a - 1.0) * k_a), xk, a_sig, param(ka_ref))
        b_vec = each(lambda u, a: u * a, kk, a_sig)
        yield

        def cumulative(ld):
            ld_hi = bf(ld)
            ld_lo = bf(ld - ld_hi.astype(F32))
            cum = _dot(tril_incl, jnp.concatenate([ld_hi, ld_lo], axis=1))
            return cum[:, :LANES] + cum[:, LANES:]
        lw = each(cumulative, log_decay)
        yield
        lw_last = [v[BLK - 1:BLK, :] for v in lw]
        w_inv = each(lambda v: jnp.exp(-v), lw)
        w_tail = each(lambda v, last: jnp.exp(last - v), lw, lw_last)
        w_all = each(jnp.exp, lw_last)

        at = each(lambda u, v, ld: -u * jnp.exp(v - ld), kk, lw, log_decay)
        rt = each(lambda r, v: r * jnp.exp(v), xr, lw)
        bt = each(lambda b, w: b * w, b_vec, w_inv)
        kt = each(lambda k, w: k * w, k2, w_inv)
        bh = each(lambda b, w: b * w, b_vec, w_tail)
        kh = each(lambda k, w: k * w, k2, w_tail)
        yield

        def pair_products(a, r, b, k):
            lhs = jnp.concatenate([jnp.where(head0, a, 0.0), jnp.where(head0, r, 0.0),
                                   jnp.where(head0, 0.0, a), jnp.where(head0, 0.0, r)], axis=0)
            return _nt_dot(bf(lhs), bf(jnp.concatenate([b, k], axis=0)))
        pp = each(pair_products, at, rt, bt, kt)
        yield

        def n_matrix(q):
            aab0 = jnp.where(strict, q[0:BLK, :BLK], 0.0)
            aab1 = jnp.where(strict, q[2 * BLK:3 * BLK, :BLK], 0.0)
            return bf(jnp.concatenate([jnp.concatenate([aab0, zeros_blk], axis=1),
                                       jnp.concatenate([zeros_blk, aab1], axis=1)], axis=0))

        def aak_matrix(q):
            return bf(jnp.concatenate([jnp.where(strict, q[0:BLK, BLK:], 0.0),
                                       jnp.where(strict, q[2 * BLK:3 * BLK, BLK:], 0.0)], axis=0))
        akv = _pairwise_dots(each(aak_matrix, pp), each(bf, xv))

        def x_init(a, av):
            return jnp.concatenate([_stack_heads(a, head0), jnp.where(own_head, av, 0.0)], axis=1)

        def read_matrices(q):
            arb = jnp.concatenate([jnp.where(incl, q[BLK:2 * BLK, :BLK], 0.0),
                                   jnp.where(incl, q[3 * BLK:, :BLK], 0.0)], axis=1)
            ark = jnp.concatenate([jnp.where(incl, q[BLK:2 * BLK, BLK:], 0.0),
                                   jnp.where(incl, q[3 * BLK:, BLK:], 0.0)], axis=1)
            return bf(arb), bf(ark)

        for j in pairs:
            slot["n"][j] = n_matrix(pp[j])
            slot["x"][j] = bf(x_init(at[j], akv[j]))
            slot["arb"][j], slot["ark"][j] = read_matrices(pp[j])
        yield
        r_k = param(rk_ref)
        for j in pairs:
            slot["rt"][j] = rt[j]
            slot["bhs"][j] = bf(_stack_heads(bh[j], head0))
            slot["khs"][j] = bf(_stack_heads(kh[j], head0))
            slot["vs"][j] = bf(_stack_heads(xv[j], head0))
            slot["wall"][j] = jnp.broadcast_to(w_all[j], (8, LANES))
            slot["bonus"][j] = _head_sum(xr[j] * k2[j] * r_k[j], head0) * xv[j]
            slot["gate"][j] = gate[j]

    def consume(c, slot):
        n_chunk = _N_PAIR
        n_bf = [slot["n"][j] for j in range(n_chunk)]
        x_bf = [slot["x"][j] for j in range(n_chunk)]

        def live_rows(m, skip):
            if not skip:
                return m
            return jnp.concatenate([m[skip:BLK], m[BLK + skip:]], axis=0)

        def merge_rows(old, new, skip):
            if not skip:
                return new
            live = BLK - skip
            return jnp.concatenate([old[:skip], new[:live], old[BLK:BLK + skip], new[live:]],
                                   axis=0)

        span = 1
        while span < BLK:
            skip = span if span >= 16 else 0
            x_bf = each(lambda xx, nn: merge_rows(
                xx, bf(_dot(live_rows(nn + eye_bf, skip), xx)), skip), x_bf, n_bf)
            span *= 2
            if span < BLK:
                skip = span if span >= 16 else 0
                n_bf = each(lambda nn: merge_rows(
                    zeros_bf, bf(_dot(live_rows(nn, skip), nn)), skip), n_bf)
            yield
        v_s = [slot["vs"][j] for j in range(n_chunk)]

        def read_side(j, xb, vs):
            z = _dot(slot["arb"][j], xb)
            return slot["rt"][j] + z[:, :LANES], z[:, LANES:] + _dot(slot["ark"][j], vs)
        r_eff_y0 = each(read_side, range(n_chunk), x_bf, v_s)
        yield

        def state_side(j, xb, vs):
            d1 = _tn_dot(slot["bhs"][j], xb)
            d2 = _tn_dot(slot["khs"][j], vs)
            w_all = slot["wall"][j][0:1, :]
            return jnp.where(row == col, w_all, 0.0) + d1[:, :LANES], d1[:, LANES:] + d2
        m_eff_g_eff = each(state_side, range(n_chunk), x_bf, v_s)
        rm_lhs = each(lambda ry, mg: bf(jnp.concatenate([ry[0], mg[0]], axis=0)),
                      r_eff_y0, m_eff_g_eff)
        yield

        rm = _pairwise_dots(rm_lhs, [bf(h_ref[j]) for j in pairs])
        y = each(lambda v, ry: v[:BLK] + ry[1], rm, r_eff_y0)
        for j in pairs:
            h_ref[j] = rm[j][BLK:] + m_eff_g_eff[j][1]
        yield
        mean = each(lambda v: _head_sum(v, head0) * (1.0 / HEAD_DIM), y)
        yc = each(lambda v, m: v - m, y, mean)
        var = each(lambda v: _head_sum(v * v, head0) * (1.0 / HEAD_DIM), yc)
        rows = pl.ds(_aligned(c * BLK, BLK), BLK)
        lnx_g, lnx_b = param(lng_ref), param(lnb_ref)
        for j in pairs:
            yn = yc[j] * lax.rsqrt(var[j] + LNX_EPS) * lnx_g[j] + lnx_b[j]
            o_ref[0, rows, lanes[j]] = (yn + slot["bonus"][j]) * slot["gate"][j]

    def alternate(*stages):
        live = list(stages)
        while live:
            for gen in list(live):
                try:
                    next(gen)
                except StopIteration:
                    live.remove(gen)

    names = ("n", "x", "arb", "ark", "rt", "bhs", "khs", "vs", "wall", "bonus", "gate")
    slot_a = dict(zip(names, slots[:len(names)]))
    slot_b = dict(zip(names, slots[len(names):]))
    assert n_chunks % 2 == 0

    alternate(produce(0, slot_a))

    def chunk_pair(i, carry):
        c = 2 * i
        alternate(consume(c, slot_a), produce(c + 1, slot_b))
        alternate(consume(c + 1, slot_b), produce(jnp.minimum(c + 2, n_chunks - 1), slot_a))
        return carry

    lax.fori_loop(0, n_chunks // 2, chunk_pair, 0)


def _wkv_slot_shapes():
    g = _N_PAIR
    return [
        pltpu.VMEM((g, 2 * BLK, 2 * BLK), BF16),
        pltpu.VMEM((g, 2 * BLK, 2 * LANES), BF16),
        pltpu.VMEM((g, BLK, 2 * BLK), BF16),
        pltpu.VMEM((g, BLK, 2 * BLK), BF16),
        pltpu.VMEM((g, BLK, LANES), F32),
        pltpu.VMEM((g, 2 * BLK, LANES), BF16),
        pltpu.VMEM((g, 2 * BLK, LANES), BF16),
        pltpu.VMEM((g, 2 * BLK, LANES), BF16),
        pltpu.VMEM((g, 8, LANES), F32),
        pltpu.VMEM((g, BLK, LANES), F32),
        pltpu.VMEM((g, BLK, LANES), F32),
    ]


def _rwkv(proj, mu, w0, a0, k_k, k_a, r_k, lnx_g, lnx_b, wwa, g2_bf16):
    b, s, _ = proj.shape
    np_ = _N_PAIR
    w = RWKV_WIDTH
    wide = lambda off: pl.BlockSpec((1, s, w), lambda i: (i, 0, off))
    lane = lambda off: pl.BlockSpec((1, s, LANES), lambda i: (i, 0, off))
    vec_w = lambda off: pl.BlockSpec((1, w), lambda i: (0, off))
    vec_l = lambda off: pl.BlockSpec((1, LANES), lambda i: (0, off))
    full = lambda shape: pl.BlockSpec(shape, lambda i: (0,) * len(shape))
    return pl.pallas_call(
        _wkv_kernel,
        grid=(b,),
        in_specs=[
            wide(0), wide(1), wide(2),
            lane(3 * np_), lane(3 * np_ + 1),
            vec_w(0), vec_w(1), vec_w(2), vec_l(3 * np_), vec_l(3 * np_ + 1),
            vec_w(0), vec_w(0), vec_w(0), vec_w(0), vec_w(0), vec_w(0), vec_w(0),
            full((LANES, np_ * 2 * LANES)),
            full((GATE_LORA, w)),
        ],
        out_specs=pl.BlockSpec((1, s, w), lambda i: (i, 0, 0)),
        out_shape=jax.ShapeDtypeStruct((b, s, RWKV_WIDTH), F32),
        scratch_shapes=[pltpu.VMEM((np_, LANES, LANES), F32)] + 2 * _wkv_slot_shapes(),
        compiler_params=pltpu.CompilerParams(
            dimension_semantics=("parallel",), vmem_limit_bytes=VMEM_LIMIT),
        name="rwkv7",
    )(proj, proj, proj, proj, proj, mu, mu, mu, mu, mu,
      w0, a0, k_k, k_a, r_k, lnx_g, lnx_b, wwa, g2_bf16)


def _out_ffn_kernel(x_ref, attn_ref, rwkv_ref, woa_ref, wor_ref, g2_ref,
                    wg_ref, wu_ref, wd_ref, o_ref):
    x1 = (x_ref[...] + _dot(attn_ref[...].astype(BF16), woa_ref[...])
          + _dot(rwkv_ref[...].astype(BF16), wor_ref[...]))
    ms = jnp.mean(x1 * x1, axis=-1, keepdims=True)
    xn = (x1 * lax.rsqrt(ms + NORM_EPS) * g2_ref[...]).astype(BF16)
    gate = _dot(xn, wg_ref[...])
    up = _dot(xn, wu_ref[...])
    hidden = (gate * jax.nn.sigmoid(gate) * up).astype(BF16)
    o_ref[...] = x1 + _dot(hidden, wd_ref[...])


def _out_ffn(x2d, attn2d, rwkv2d, woa, wor, g2, wg, wu, wd, tm):
    t, d = x2d.shape
    f = wg.shape[1]
    tok = lambda w: pl.BlockSpec((tm, w), lambda i: (i, 0))
    res = lambda shape: pl.BlockSpec(shape, lambda i: (0, 0), pipeline_mode=pl.Buffered(1))
    return pl.pallas_call(
        _out_ffn_kernel,
        grid=(t // tm,),
        in_specs=[
            tok(d), tok(ATTN_WIDTH), tok(RWKV_WIDTH),
            res((ATTN_WIDTH, d)), res((RWKV_WIDTH, d)),
            pl.BlockSpec((1, d), lambda i: (0, 0)),
            res((d, f)), res((d, f)), res((f, d)),
        ],
        out_specs=tok(d),
        out_shape=jax.ShapeDtypeStruct((t, d), F32),
        compiler_params=pltpu.CompilerParams(
            dimension_semantics=("parallel",), vmem_limit_bytes=VMEM_LIMIT),
        name="out_ffn",
    )(x2d, attn2d, rwkv2d, woa, wor, g2, wg, wu, wd)


def _layer(x, norm1_g, w_in, q_norm_g, k_norm_g, attn_out_g, rwkv_mu, w0, w2, a0, a2, g2,
           k_k, k_a, r_k, lnx_g, lnx_b, w_out, norm2_g, w_gate, w_up, w_down):
    b, s, d = x.shape
    assert s % (BLK * max(dil for _, dil in DILATED_PATTERNS)) == 0
    assert all(window // dil == BLK for window, dil in DILATED_PATTERNS)
    t = b * s
    assert t % IN_PROJ_ROWS == 0 and t % FFN_ROWS == 0
    row = lambda v: v.reshape(1, -1).astype(F32)

    x2d = x.reshape(t, d)
    per_head = lambda v: jnp.tile(v.reshape(1, HEAD_DIM), (1, ATTN_HEADS)).astype(F32)
    qk_gain = jnp.concatenate([per_head(q_norm_g) * (HEAD_DIM ** -0.5 * LOG2_E),
                               per_head(k_norm_g)], axis=1)
    proj, qkv = _in_proj(x, row(norm1_g), w_in.astype(BF16), qk_gain, IN_PROJ_ROWS)
    proj = proj.reshape(b, s, -1)

    slopes = jnp.exp2(-8.0 * jnp.arange(1, ATTN_HEADS + 1, dtype=F32) / ATTN_HEADS)
    attn = _attention(qkv.reshape(b, s, -1), slopes, row(attn_out_g))

    w2p = w2.reshape(DECAY_LORA, _N_PAIR, LANES).transpose(1, 0, 2)
    a2p = a2.reshape(ICLR_LORA, _N_PAIR, LANES).transpose(1, 0, 2)
    zero = jnp.zeros_like(w2p)
    wwa = jnp.concatenate([jnp.concatenate([w2p, zero], axis=2),
                           jnp.concatenate([jnp.zeros_like(a2p), a2p], axis=2)], axis=1)
    wwa = wwa.transpose(1, 0, 2).reshape(LANES, _N_PAIR * 2 * LANES).astype(BF16)
    rwkv = _rwkv(proj, row(rwkv_mu), row(w0), row(a0), row(k_k), row(k_a), row(r_k),
                 row(lnx_g), row(lnx_b), wwa, g2.astype(BF16))

    out = _out_ffn(x2d, attn.reshape(t, ATTN_WIDTH), rwkv.reshape(t, RWKV_WIDTH),
                   w_out[:ATTN_WIDTH].astype(BF16), w_out[ATTN_WIDTH:].astype(BF16),
                   row(norm2_g), w_gate.astype(BF16), w_up.astype(BF16), w_down.astype(BF16),
                   FFN_ROWS)
    return out.reshape(b, s, d)


def kernel(x, norm1_g, w_in, q_norm_g, k_norm_g, attn_out_g, rwkv_mu, w0, w2, a0, a2, g2,
           k_k, k_a, r_k, lnx_g, lnx_b, w_out, norm2_g, w_gate, w_up, w_down):
    h = x
    for layer in range(norm1_g.shape[0]):
        h = _layer(h, norm1_g[layer], w_in[layer], q_norm_g[layer], k_norm_g[layer],
                   attn_out_g[layer], rwkv_mu[layer], w0[layer], w2[layer], a0[layer],
                   a2[layer], g2[layer], k_k[layer], k_a[layer], r_k[layer], lnx_g[layer],
                   lnx_b[layer], w_out[layer], norm2_g[layer], w_gate[layer], w_up[layer],
                   w_down[layer])
    return h
```

```python
import functools

import jax
import jax.numpy as jnp
from jax import lax
from jax.experimental import pallas as pl
from jax.experimental.pallas import tpu as pltpu

F32 = jnp.float32
BF16 = jnp.bfloat16

HEAD_DIM = 64
LANES = 128
ATTN_HEADS = 8
RWKV_HEADS = 8
ATTN_WIDTH = ATTN_HEADS * HEAD_DIM
RWKV_WIDTH = RWKV_HEADS * HEAD_DIM
DECAY_LORA = 64
ICLR_LORA = 64
GATE_LORA = 128
DILATED_PATTERNS = ((128, 1), (512, 4), (2048, 16))
RESIDUES = max(dil for _, dil in DILATED_PATTERNS)
BLK = 128
ATTN_GROUP = 8
IN_PROJ_ROWS = 512
STAGE_PITCH = 24
FFN_ROWS = 512
NORM_EPS = 1e-6
LNX_EPS = 64e-5
MASKED = -1e30
LOG2_E = 1.4426950408889634
VMEM_LIMIT = 56 * 1024 * 1024

_QB, _KB, _VB = 0, ATTN_WIDTH // LANES, 2 * ATTN_WIDTH // LANES
_N_PAIR = RWKV_WIDTH // LANES


def _nt_dot(a, b):
    return lax.dot_general(a, b, (((1,), (1,)), ((), ())), preferred_element_type=F32)


def _tn_dot(a, b):
    return lax.dot_general(a, b, (((0,), (0,)), ((), ())), preferred_element_type=F32)


def _dot(a, b):
    return jnp.dot(a, b, preferred_element_type=F32)


def _pairwise_dots(lhs, rhs):
    assert len(lhs) == len(rhs) and len(lhs) % 2 == 0
    out = []
    for j in range(0, len(lhs), 2):
        zero = jnp.zeros_like(rhs[j])
        both = _dot(jnp.concatenate([lhs[j], lhs[j + 1]], axis=1),
                    jnp.concatenate([jnp.concatenate([rhs[j], zero], axis=1),
                                     jnp.concatenate([zero, rhs[j + 1]], axis=1)], axis=0))
        out += [both[:, :LANES], both[:, LANES:]]
    return out


def _aligned(index, multiple):
    if isinstance(index, int):
        return index
    return pl.multiple_of(index, multiple)


def _head0_lanes():
    return lax.broadcasted_iota(jnp.int32, (1, LANES), 1) < HEAD_DIM


def _head_averager():
    i = lax.broadcasted_iota(jnp.int32, (LANES, LANES), 0)
    j = lax.broadcasted_iota(jnp.int32, (LANES, LANES), 1)
    return jnp.where((i < HEAD_DIM) == (j < HEAD_DIM), 1.0 / HEAD_DIM, 0.0).astype(BF16)


def _head_mean_mxu(x, averager):
    return _dot(x.astype(BF16), averager)


def _head_sum(x, head0):
    s0 = jnp.sum(jnp.where(head0, x, 0.0), axis=-1, keepdims=True)
    s1 = jnp.sum(jnp.where(head0, 0.0, x), axis=-1, keepdims=True)
    return jnp.where(head0, s0, s1)


def _stack_heads(x, head0):
    return jnp.concatenate([jnp.where(head0, x, 0.0), jnp.where(head0, 0.0, x)], axis=0)


def _in_proj_kernel(x_ref, g_ref, w_ref, qk_gain_ref, rw_ref, qkv_ref, stage_ref):
    x = x_ref[...]
    ms = jnp.mean(x * x, axis=-1, keepdims=True)
    xn = (x * lax.rsqrt(ms + NORM_EPS) * g_ref[...]).astype(BF16)
    n_qk = qk_gain_ref.shape[1]
    n_attn = qkv_ref.shape[3]
    per_class = qkv_ref.shape[2]
    qk = _dot(xn, w_ref[:, :n_qk])
    val = _dot(xn, w_ref[:, n_qk:n_attn])
    head0 = _head0_lanes()
    for j in range(n_attn // LANES):
        cols = slice(j * LANES, (j + 1) * LANES)
        if j < n_qk // LANES:
            v = qk[:, cols]
            mean_sq = _head_sum(v * v, head0) * (1.0 / HEAD_DIM)
            v = v * lax.rsqrt(mean_sq + NORM_EPS) * qk_gain_ref[:, cols]
        else:
            v = val[:, j * LANES - n_qk:(j + 1) * LANES - n_qk]
        for i in range(per_class):
            stage_ref[j, pl.ds(i * STAGE_PITCH, RESIDUES), :] = v[i * RESIDUES:(i + 1) * RESIDUES]
        for r in range(RESIDUES):
            qkv_ref[0, r, :, cols] = stage_ref[j, pl.ds(r, per_class, stride=STAGE_PITCH), :]
    rw_ref[...] = _dot(xn, w_ref[:, n_attn:])


def _in_proj(x3d, g, w_bf16, qk_gain, tm):
    b, s, d = x3d.shape
    n = w_bf16.shape[1]
    n_attn = 3 * ATTN_WIDTH
    assert s % tm == 0 and tm % (8 * RESIDUES) == 0
    tiles = s // tm
    return pl.pallas_call(
        _in_proj_kernel,
        grid=(b * tiles,),
        in_specs=[
            pl.BlockSpec((tm, d), lambda i: (i, 0)),
            pl.BlockSpec((1, d), lambda i: (0, 0)),
            pl.BlockSpec((d, n), lambda i: (0, 0), pipeline_mode=pl.Buffered(1)),
            pl.BlockSpec((1, qk_gain.shape[1]), lambda i: (0, 0)),
        ],
        out_specs=[
            pl.BlockSpec((tm, n - n_attn), lambda i: (i, 0)),
            pl.BlockSpec((1, RESIDUES, tm // RESIDUES, n_attn),
                         lambda i: (i // tiles, 0, i % tiles, 0)),
        ],
        out_shape=[
            jax.ShapeDtypeStruct((b * s, n - n_attn), F32),
            jax.ShapeDtypeStruct((b, RESIDUES, s // RESIDUES, n_attn), F32),
        ],
        scratch_shapes=[
            pltpu.VMEM((n_attn // LANES, tm // RESIDUES * STAGE_PITCH, LANES), F32)],
        compiler_params=pltpu.CompilerParams(
            dimension_semantics=("parallel",), vmem_limit_bytes=VMEM_LIMIT),
        name="in_proj",
    )(x3d.reshape(b * s, d), g, w_bf16, qk_gain)


def _attn_kernel(slope_ref, q_ref, k_ref, v_ref, go_ref, o_ref,
                 m_ref, l_ref, acc_ref, bias_ref):
    seq = q_ref.shape[1]
    per_class = seq // RESIDUES
    hp = pl.program_id(0)
    head0 = _head0_lanes()
    averager = _head_averager()
    def block_runs(dil, cls, n):
        fold = RESIDUES // dil
        length = BLK // fold
        return [(_aligned((dil * m + cls) * per_class + n * length, 8), length)
                for m in range(fold)]

    def member_of(u, dil):
        fold = RESIDUES // dil
        length = BLK // fold
        return fold * (u & (length - 1)) + (u >> (length.bit_length() - 1))

    def load_rows(ref, runs):
        return jnp.concatenate([ref[0, pl.ds(start, length), :] for start, length in runs], axis=0)

    def store_rows(ref, p, runs, value):
        offset = 0
        for start, length in runs:
            ref[p, pl.ds(start, length), :] = value[offset:offset + length]
            offset += length

    @pl.when(pl.program_id(1) == 0)
    def _():
        qi = lax.broadcasted_iota(jnp.int32, (2 * BLK, 2 * BLK), 0)
        ci = lax.broadcasted_iota(jnp.int32, (2 * BLK, 2 * BLK), 1)
        slope = jnp.where(qi < BLK, slope_ref[2 * hp], slope_ref[2 * hp + 1]) * LOG2_E
        for p, (window, dil) in enumerate(DILATED_PATTERNS):
            steps = window // dil
            back = ((BLK + member_of(qi & (BLK - 1), dil))
                    - ((ci & BLK) + member_of(ci & (BLK - 1), dil)))
            valid = (back >= 0) & (back <= steps)
            bias_ref[p] = jnp.where(valid, -slope * (back.astype(F32) * dil), MASKED)

    ones_keys = jnp.ones((2 * BLK, LANES), BF16)

    def band_blocks(p, dil, blocks):
        each = lambda f, *lists: [f(*args) for args in zip(*lists)]
        bf = lambda v: v.astype(BF16)
        runs = [block_runs(dil, cls, n) for cls, n in blocks]
        has_prevs = [n > 0 for _, n in blocks]

        def keys_values(block, own, has_prev):
            kb = load_rows(k_ref, own)
            vb = load_rows(v_ref, own)
            if has_prev:
                prev = block_runs(dil, block[0], block[1] - 1)
                kb = jnp.concatenate([load_rows(k_ref, prev), kb], axis=0)
                vb = jnp.concatenate([load_rows(v_ref, prev), vb], axis=0)
            return bf(kb), jnp.concatenate([bf(vb), ones_keys[:vb.shape[0]]], axis=1)
        kv = each(keys_values, blocks, runs, has_prevs)
        qs = each(lambda own: bf(_stack_heads(load_rows(q_ref, own), head0)), runs)
        s = each(lambda q, kvb, has_prev:
                 _nt_dot(q, kvb[0]) + (bias_ref[p] if has_prev else bias_ref[p, :, BLK:]),
                 qs, kv, has_prevs)
        m_blk = each(lambda v: jnp.max(v, axis=-1, keepdims=True), s)
        pr = each(lambda v, m: bf(jnp.exp2(v - m)), s, m_blk)
        pv = each(lambda v, kvb: _dot(v, kvb[1]), pr, kv)

        m_pair = each(lambda m_b: jnp.where(head0, m_b[:BLK], m_b[BLK:]), m_blk)
        acc_pair = each(lambda v: jnp.where(head0, v[:BLK, :LANES], v[BLK:, :LANES]), pv)
        l_pair = each(lambda v: jnp.where(head0, v[:BLK, LANES:], v[BLK:, LANES:]), pv)
        if p + 1 < len(DILATED_PATTERNS):
            def keep(own, m_b, acc_b, l_b):
                store_rows(m_ref, p, own, m_b)
                store_rows(acc_ref, p, own, acc_b)
                store_rows(l_ref, p, own, l_b)
            each(keep, runs, m_pair, acc_pair, l_pair)
            return

        assert dil == RESIDUES and all(len(own) == 1 for own in runs)
        rows = [pl.ds(own[0][0], BLK) for own in runs]
        m_prev = [[m_ref[q, r, :] for q in range(p)] for r in rows]
        m_all = each(lambda mine, others: functools.reduce(jnp.maximum, others, mine),
                     m_pair, m_prev)
        w_mine = each(lambda mine, top: jnp.exp2(mine - top), m_pair, m_all)
        w_prev = each(lambda others, top: [jnp.exp2(m - top) for m in others], m_prev, m_all)
        num = each(lambda r, wm, a, ws: wm * a + sum(w * acc_ref[q, r, :] for q, w in enumerate(ws)),
                   rows, w_mine, acc_pair, w_prev)
        den = each(lambda r, wm, l, ws: wm * l + sum(w * l_ref[q, r, :] for q, w in enumerate(ws)),
                   rows, w_mine, l_pair, w_prev)
        o = each(lambda a, b: a / b, num, den)
        ms = each(lambda v: _head_mean_mxu(v * v, averager), o)
        for (cls, _), v, mean_sq in zip(blocks, o, ms):
            o_ref[0, pl.ds(cls, per_class, stride=RESIDUES), :] = (
                v * lax.rsqrt(mean_sq + NORM_EPS) * go_ref[...])

    for p, (window, dil) in enumerate(DILATED_PATTERNS):
        n_blk = seq // (BLK * dil)
        group = ATTN_GROUP
        if n_blk >= group:
            assert n_blk % group == 0
            groups = n_blk // group

            def residue(r, carry, p=p, dil=dil, groups=groups, group=group):
                for grp in range(groups):
                    band_blocks(p, dil, [(r, n) for n in range(grp * group, (grp + 1) * group)])
                return carry
        else:
            assert group % n_blk == 0 and dil % (group // n_blk) == 0
            per_group = group // n_blk

            def residue(i, carry, p=p, dil=dil, n_blk=n_blk, per_group=per_group):
                classes = [i * per_group + g for g in range(per_group)]
                band_blocks(p, dil, [(r, n) for r in classes for n in range(n_blk)])
                return carry
        n_steps = dil if n_blk >= group else dil // (group // n_blk)
        if n_steps == 1:
            residue(0, 0)
        else:
            lax.fori_loop(0, n_steps, residue, 0)


def _attention(qkv, slopes, go):
    b, s, _ = qkv.shape
    assert BLK & (BLK - 1) == 0 and s % (RESIDUES * BLK) == 0
    proj = qkv
    blk = lambda off: pl.BlockSpec((1, s, LANES), lambda j, i: (i, 0, off + j))
    return pl.pallas_call(
        _attn_kernel,
        grid=(_N_PAIR, b),
        in_specs=[
            pl.BlockSpec(memory_space=pltpu.SMEM),
            blk(_QB), blk(_KB), blk(_VB),
            pl.BlockSpec((1, LANES), lambda j, i: (0, j)),
        ],
        out_specs=pl.BlockSpec((1, s, LANES), lambda j, i: (i, 0, j)),
        out_shape=jax.ShapeDtypeStruct((b, s, ATTN_WIDTH), F32),
        scratch_shapes=[
            pltpu.VMEM((len(DILATED_PATTERNS) - 1, s, LANES), F32),
            pltpu.VMEM((len(DILATED_PATTERNS) - 1, s, LANES), F32),
            pltpu.VMEM((len(DILATED_PATTERNS) - 1, s, LANES), F32),
            pltpu.VMEM((len(DILATED_PATTERNS), 2 * BLK, 2 * BLK), F32),
        ],
        compiler_params=pltpu.CompilerParams(
            dimension_semantics=("arbitrary", "arbitrary"), vmem_limit_bytes=VMEM_LIMIT),
        name="dilated_attn",
    )(slopes, proj, proj, proj, go)


def _wkv_kernel(pr_ref, pk_ref, pv_ref, pwa_ref, pg_ref,
                mur_ref, muk_ref, muv_ref, muwa_ref, mug_ref,
                w0_ref, a0_ref, kk_ref, ka_ref, rk_ref, lng_ref, lnb_ref,
                wwa_ref, g2_ref, o_ref, h_ref, *slots):
    seq = pr_ref.shape[1]
    n_chunks = seq // BLK
    head0 = _head0_lanes()
    row = lax.broadcasted_iota(jnp.int32, (BLK, BLK), 0)
    col = lax.broadcasted_iota(jnp.int32, (BLK, BLK), 1)
    strict = col < row
    incl = col <= row
    tril_incl = jnp.where(incl, 1.0, 0.0).astype(BF16)
    first_row = lax.broadcasted_iota(jnp.int32, (BLK, LANES), 0) == 0
    srow = lax.broadcasted_iota(jnp.int32, (2 * BLK, LANES), 0) < BLK
    slane = lax.broadcasted_iota(jnp.int32, (2 * BLK, LANES), 1) < HEAD_DIM
    own_head = srow == slane
    zeros_blk = jnp.zeros((BLK, BLK), F32)
    zeros_bf = jnp.zeros((2 * BLK, 2 * BLK), BF16)
    eye_bf = jnp.where(lax.broadcasted_iota(jnp.int32, (2 * BLK, 2 * BLK), 0)
                       == lax.broadcasted_iota(jnp.int32, (2 * BLK, 2 * BLK), 1),
                       1.0, 0.0).astype(BF16)

    h_ref[...] = jnp.zeros(h_ref.shape, F32)

    each = lambda f, *lists: [f(*args) for args in zip(*lists)]
    bf = lambda v: v.astype(BF16)
    pairs = range(_N_PAIR)
    lanes = [slice(j * LANES, (j + 1) * LANES) for j in pairs]
    param = lambda ref: [ref[:, sl] for sl in lanes]

    def produce(c, slot):
        row0 = _aligned(c * BLK, BLK)
        rows = pl.ds(row0, BLK)
        prev8 = pl.ds(_aligned(max(row0 - 8, 0) if isinstance(row0, int)
                               else jnp.maximum(row0 - 8, 0), 8), 8)
        has_prev = jnp.where(c > 0, 1.0, 0.0).astype(F32)

        def shifted(ref, mu_ref, sl):
            p = ref[0, rows, sl]
            last = ref[0, prev8, sl][7:8, :] * has_prev
            prev = jnp.where(first_row, last, pltpu.roll(p, 1, 0))
            return p + (prev - p) * mu_ref[:, sl]

        xr = [shifted(pr_ref, mur_ref, sl) for sl in lanes]
        xk = [shifted(pk_ref, muk_ref, sl) for sl in lanes]
        xv = [shifted(pv_ref, muv_ref, sl) for sl in lanes]
        xwa = shifted(pwa_ref, muwa_ref, slice(None))
        xg = shifted(pg_ref, mug_ref, slice(None))
        yield

        lora_all = _dot(bf(jnp.where(head0, jnp.tanh(xwa), xwa)), wwa_ref[...])
        lora = [lora_all[:, 2 * LANES * j:2 * LANES * (j + 1)] for j in pairs]
        gate_all = _dot(bf(jax.nn.sigmoid(xg)), g2_ref[...])
        gate = [gate_all[:, sl] for sl in lanes]
        yield

        def decay_log(lo, w0):
            zw = -(w0 + lo[:, :LANES])
            softplus = jnp.maximum(zw, 0.0) + jnp.log(1.0 + jnp.exp(-jnp.abs(zw)))
            return -jnp.exp(-softplus - 0.5)
        log_decay = each(decay_log, lora, param(w0_ref))
        a_sig = each(lambda lo, a0: jax.nn.sigmoid(a0 + lo[:, LANES:]), lora, param(a0_ref))

        def unit_key(k, k_k):
            kkv = k * k_k
            return kkv / jnp.maximum(jnp.sqrt(_head_sum(kkv * kkv, head0)), 1e-12)
        kk = each(unit_key, xk, param(kk_ref))
        k2 = each(lambda k, a, k_a: k * (1.0 + (a - 1.0) * k_a), xk, a_sig, param(ka_ref))
        b_vec = each(lambda u, a: u * a, kk, a_sig)
        yield

        def cumulative(ld):
            ld_hi = bf(ld)
            ld_lo = bf(ld - ld_hi.astype(F32))
            cum = _dot(tril_incl, jnp.concatenate([ld_hi, ld_lo], axis=1))
            return cum[:, :LANES] + cum[:, LANES:]
        lw = each(cumulative, log_decay)
        yield
        lw_last = [v[BLK - 1:BLK, :] for v in lw]
        w_inv = each(lambda v: jnp.exp(-v), lw)
        w_tail = each(lambda v, last: jnp.exp(last - v), lw, lw_last)
        w_all = each(jnp.exp, lw_last)

        at = each(lambda u, v, ld: -u * jnp.exp(v - ld), kk, lw, log_decay)
        rt = each(lambda r, v: r * jnp.exp(v), xr, lw)
        bt = each(lambda b, w: b * w, b_vec, w_inv)
        kt = each(lambda k, w: k * w, k2, w_inv)
        bh = each(lambda b, w: b * w, b_vec, w_tail)
        kh = each(lambda k, w: k * w, k2, w_tail)
        yield

        def pair_products(a, r, b, k):
            lhs = jnp.concatenate([jnp.where(head0, a, 0.0), jnp.where(head0, r, 0.0),
                                   jnp.where(head0, 0.0, a), jnp.where(head0, 0.0, r)], axis=0)
            return _nt_dot(bf(lhs), bf(jnp.concatenate([b, k], axis=0)))
        pp = each(pair_products, at, rt, bt, kt)
        yield

        def n_matrix(q):
            aab0 = jnp.where(strict, q[0:BLK, :BLK], 0.0)
            aab1 = jnp.where(strict, q[2 * BLK:3 * BLK, :BLK], 0.0)
            return bf(jnp.concatenate([jnp.concatenate([aab0, zeros_blk], axis=1),
                                       jnp.concatenate([zeros_blk, aab1], axis=1)], axis=0))

        def aak_matrix(q):
            return bf(jnp.concatenate([jnp.where(strict, q[0:BLK, BLK:], 0.0),
                                       jnp.where(strict, q[2 * BLK:3 * BLK, BLK:], 0.0)], axis=0))
        akv = _pairwise_dots(each(aak_matrix, pp), each(bf, xv))

        def x_init(a, av):
            return jnp.concatenate([_stack_heads(a, head0), jnp.where(own_head, av, 0.0)], axis=1)

        def read_matrices(q):
            arb = jnp.concatenate([jnp.where(incl, q[BLK:2 * BLK, :BLK], 0.0),
                                   jnp.where(incl, q[3 * BLK:, :BLK], 0.0)], axis=1)
            ark = jnp.concatenate([jnp.where(incl, q[BLK:2 * BLK, BLK:], 0.0),
                                   jnp.where(incl, q[3 * BLK:, BLK:], 0.0)], axis=1)
            return bf(arb), bf(ark)

        for j in pairs:
            slot["n"][j] = n_matrix(pp[j])
            slot["x"][j] = bf(x_init(at[j], akv[j]))
            slot["arb"][j], slot["ark"][j] = read_matrices(pp[j])
        yield
        r_k = param(rk_ref)
        for j in pairs:
            slot["rt"][j] = rt[j]
            slot["bhs"][j] = bf(_stack_heads(bh[j], head0))
            slot["khs"][j] = bf(_stack_heads(kh[j], head0))
            slot["vs"][j] = bf(_stack_heads(xv[j], head0))
            slot["wall"][j] = jnp.broadcast_to(w_all[j], (8, LANES))
            slot["bonus"][j] = _head_sum(xr[j] * k2[j] * r_k[j], head0) * xv[j]
            slot["gate"][j] = gate[j]

    def consume(c, slot):
        n_chunk = _N_PAIR
        n_bf = [slot["n"][j] for j in range(n_chunk)]
        x_bf = [slot["x"][j] for j in range(n_chunk)]

        def live_rows(m, skip):
            if not skip:
                return m
            return jnp.concatenate([m[skip:BLK], m[BLK + skip:]], axis=0)

        def merge_rows(old, new, skip):
            if not skip:
                return new
            live = BLK - skip
            return jnp.concatenate([old[:skip], new[:live], old[BLK:BLK + skip], new[live:]],
                                   axis=0)

        span = 1
        while span < BLK:
            skip = span if span >= 16 else 0
            x_bf = each(lambda xx, nn: merge_rows(
                xx, bf(_dot(live_rows(nn + eye_bf, skip), xx)), skip), x_bf, n_bf)
            span *= 2
            if span < BLK:
                skip = span if span >= 16 else 0
                n_bf = each(lambda nn: merge_rows(
                    zeros_bf, bf(_dot(live_rows(nn, skip), nn)), skip), n_bf)
            yield
        v_s = [slot["vs"][j] for j in range(n_chunk)]

        def read_side(j, xb, vs):
            z = _dot(slot["arb"][j], xb)
            return slot["rt"][j] + z[:, :LANES], z[:, LANES:] + _dot(slot["ark"][j], vs)
        r_eff_y0 = each(read_side, range(n_chunk), x_bf, v_s)
        yield

        def state_side(j, xb, vs):
            d1 = _tn_dot(slot["bhs"][j], xb)
            d2 = _tn_dot(slot["khs"][j], vs)
            w_all = slot["wall"][j][0:1, :]
            return jnp.where(row == col, w_all, 0.0) + d1[:, :LANES], d1[:, LANES:] + d2
        m_eff_g_eff = each(state_side, range(n_chunk), x_bf, v_s)
        rm_lhs = each(lambda ry, mg: bf(jnp.concatenate([ry[0], mg[0]], axis=0)),
                      r_eff_y0, m_eff_g_eff)
        yield

        rm = _pairwise_dots(rm_lhs, [bf(h_ref[j]) for j in pairs])
        y = each(lambda v, ry: v[:BLK] + ry[1], rm, r_eff_y0)
        for j in pairs:
            h_ref[j] = rm[j][BLK:] + m_eff_g_eff[j][1]
        yield
        mean = each(lambda v: _head_sum(v, head0) * (1.0 / HEAD_DIM), y)
        yc = each(lambda v, m: v - m, y, mean)
        var = each(lambda v: _head_sum(v * v, head0) * (1.0 / HEAD_DIM), yc)
        rows = pl.ds(_aligned(c * BLK, BLK), BLK)
        lnx_g, lnx_b = param(lng_ref), param(lnb_ref)
        for j in pairs:
            yn = yc[j] * lax.rsqrt(var[j] + LNX_EPS) * lnx_g[j] + lnx_b[j]
            o_ref[0, rows, lanes[j]] = (yn + slot["bonus"][j]) * slot["gate"][j]

    def alternate(*stages):
        live = list(stages)
        while live:
            for gen in list(live):
                try:
                    next(gen)
                except StopIteration:
                    live.remove(gen)

    names = ("n", "x", "arb", "ark", "rt", "bhs", "khs", "vs", "wall", "bonus", "gate")
    slot_a = dict(zip(names, slots[:len(names)]))
    slot_b = dict(zip(names, slots[len(names):]))
    assert n_chunks % 2 == 0

    alternate(produce(0, slot_a))

    def chunk_pair(i, carry):
        c = 2 * i
        alternate(consume(c, slot_a), produce(c + 1, slot_b))
        alternate(consume(c + 1, slot_b), produce(jnp.minimum(c + 2, n_chunks - 1), slot_a))
        return carry

    lax.fori_loop(0, n_chunks // 2, chunk_pair, 0)


def _wkv_slot_shapes():
    g = _N_PAIR
    return [
        pltpu.VMEM((g, 2 * BLK, 2 * BLK), BF16),
        pltpu.VMEM((g, 2 * BLK, 2 * LANES), BF16),
        pltpu.VMEM((g, BLK, 2 * BLK), BF16),
        pltpu.VMEM((g, BLK, 2 * BLK), BF16),
        pltpu.VMEM((g, BLK, LANES), F32),
        pltpu.VMEM((g, 2 * BLK, LANES), BF16),
        pltpu.VMEM((g, 2 * BLK, LANES), BF16),
        pltpu.VMEM((g, 2 * BLK, LANES), BF16),
        pltpu.VMEM((g, 8, LANES), F32),
        pltpu.VMEM((g, BLK, LANES), F32),
        pltpu.VMEM((g, BLK, LANES), F32),
    ]


def _rwkv(proj, mu, w0, a0, k_k, k_a, r_k, lnx_g, lnx_b, wwa, g2_bf16):
    b, s, _ = proj.shape
    np_ = _N_PAIR
    w = RWKV_WIDTH
    wide = lambda off: pl.BlockSpec((1, s, w), lambda i: (i, 0, off))
    lane = lambda off: pl.BlockSpec((1, s, LANES), lambda i: (i, 0, off))
    vec_w = lambda off: pl.BlockSpec((1, w), lambda i: (0, off))
    vec_l = lambda off: pl.BlockSpec((1, LANES), lambda i: (0, off))
    full = lambda shape: pl.BlockSpec(shape, lambda i: (0,) * len(shape))
    return pl.pallas_call(
        _wkv_kernel,
        grid=(b,),
        in_specs=[
            wide(0), wide(1), wide(2),
            lane(3 * np_), lane(3 * np_ + 1),
            vec_w(0), vec_w(1), vec_w(2), vec_l(3 * np_), vec_l(3 * np_ + 1),
            vec_w(0), vec_w(0), vec_w(0), vec_w(0), vec_w(0), vec_w(0), vec_w(0),
            full((LANES, np_ * 2 * LANES)),
            full((GATE_LORA, w)),
        ],
        out_specs=pl.BlockSpec((1, s, w), lambda i: (i, 0, 0)),
        out_shape=jax.ShapeDtypeStruct((b, s, RWKV_WIDTH), F32),
        scratch_shapes=[pltpu.VMEM((np_, LANES, LANES), F32)] + 2 * _wkv_slot_shapes(),
        compiler_params=pltpu.CompilerParams(
            dimension_semantics=("parallel",), vmem_limit_bytes=VMEM_LIMIT),
        name="rwkv7",
    )(proj, proj, proj, proj, proj, mu, mu, mu, mu, mu,
      w0, a0, k_k, k_a, r_k, lnx_g, lnx_b, wwa, g2_bf16)


def _out_ffn_kernel(x_ref, attn_ref, rwkv_ref, woa_ref, wor_ref, g2_ref,
                    wg_ref, wu_ref, wd_ref, o_ref):
    x1 = (x_ref[...] + _dot(attn_ref[...].astype(BF16), woa_ref[...])
          + _dot(rwkv_ref[...].astype(BF16), wor_ref[...]))
    ms = jnp.mean(x1 * x1, axis=-1, keepdims=True)
    xn = (x1 * lax.rsqrt(ms + NORM_EPS) * g2_ref[...]).astype(BF16)
    gate = _dot(xn, wg_ref[...])
    up = _dot(xn, wu_ref[...])
    hidden = (gate * jax.nn.sigmoid(gate) * up).astype(BF16)
    o_ref[...] = x1 + _dot(hidden, wd_ref[...])


def _out_ffn(x2d, attn2d, rwkv2d, woa, wor, g2, wg, wu, wd, tm):
    t, d = x2d.shape
    f = wg.shape[1]
    tok = lambda w: pl.BlockSpec((tm, w), lambda i: (i, 0))
    res = lambda shape: pl.BlockSpec(shape, lambda i: (0, 0), pipeline_mode=pl.Buffered(1))
    return pl.pallas_call(
        _out_ffn_kernel,
        grid=(t // tm,),
        in_specs=[
            tok(d), tok(ATTN_WIDTH), tok(RWKV_WIDTH),
            res((ATTN_WIDTH, d)), res((RWKV_WIDTH, d)),
            pl.BlockSpec((1, d), lambda i: (0, 0)),
            res((d, f)), res((d, f)), res((f, d)),
        ],
        out_specs=tok(d),
        out_shape=jax.ShapeDtypeStruct((t, d), F32),
        compiler_params=pltpu.CompilerParams(
            dimension_semantics=("parallel",), vmem_limit_bytes=VMEM_LIMIT),
        name="out_ffn",
    )(x2d, attn2d, rwkv2d, woa, wor, g2, wg, wu, wd)


def _layer(x, norm1_g, w_in, q_norm_g, k_norm_g, attn_out_g, rwkv_mu, w0, w2, a0, a2, g2,
           k_k, k_a, r_k, lnx_g, lnx_b, w_out, norm2_g, w_gate, w_up, w_down):
    b, s, d = x.shape
    assert s % (BLK * max(dil for _, dil in DILATED_PATTERNS)) == 0
    assert all(window // dil == BLK for window, dil in DILATED_PATTERNS)
    t = b * s
    assert t % IN_PROJ_ROWS == 0 and t % FFN_ROWS == 0
    row = lambda v: v.reshape(1, -1).astype(F32)

    x2d = x.reshape(t, d)
    per_head = lambda v: jnp.tile(v.reshape(1, HEAD_DIM), (1, ATTN_HEADS)).astype(F32)
    qk_gain = jnp.concatenate([per_head(q_norm_g) * (HEAD_DIM ** -0.5 * LOG2_E),
                               per_head(k_norm_g)], axis=1)
    proj, qkv = _in_proj(x, row(norm1_g), w_in.astype(BF16), qk_gain, IN_PROJ_ROWS)
    proj = proj.reshape(b, s, -1)

    slopes = jnp.exp2(-8.0 * jnp.arange(1, ATTN_HEADS + 1, dtype=F32) / ATTN_HEADS)
    attn = _attention(qkv.reshape(b, s, -1), slopes, row(attn_out_g))

    w2p = w2.reshape(DECAY_LORA, _N_PAIR, LANES).transpose(1, 0, 2)
    a2p = a2.reshape(ICLR_LORA, _N_PAIR, LANES).transpose(1, 0, 2)
    zero = jnp.zeros_like(w2p)
    wwa = jnp.concatenate([jnp.concatenate([w2p, zero], axis=2),
                           jnp.concatenate([jnp.zeros_like(a2p), a2p], axis=2)], axis=1)
    wwa = wwa.transpose(1, 0, 2).reshape(LANES, _N_PAIR * 2 * LANES).astype(BF16)
    rwkv = _rwkv(proj, row(rwkv_mu), row(w0), row(a0), row(k_k), row(k_a), row(r_k),
                 row(lnx_g), row(lnx_b), wwa, g2.astype(BF16))

    out = _out_ffn(x2d, attn.reshape(t, ATTN_WIDTH), rwkv.reshape(t, RWKV_WIDTH),
                   w_out[:ATTN_WIDTH].astype(BF16), w_out[ATTN_WIDTH:].astype(BF16),
                   row(norm2_g), w_gate.astype(BF16), w_up.astype(BF16), w_down.astype(BF16),
                   FFN_ROWS)
    return out.reshape(b, s, d)


def kernel(x, norm1_g, w_in, q_norm_g, k_norm_g, attn_out_g, rwkv_mu, w0, w2, a0, a2, g2,
           k_k, k_a, r_k, lnx_g, lnx_b, w_out, norm2_g, w_gate, w_up, w_down):
    h = x
    for layer in range(norm1_g.shape[0]):
        h = _layer(h, norm1_g[layer], w_in[layer], q_norm_g[layer], k_norm_g[layer],
                   attn_out_g[layer], rwkv_mu[layer], w0[layer], w2[layer], a0[layer],
                   a2[layer], g2[layer], k_k[layer], k_a[layer], r_k[layer], lnx_g[layer],
                   lnx_b[layer], w_out[layer], norm2_g[layer], w_gate[layer], w_up[layer],
                   w_down[layer])
    return h
```

```python
import functools

import jax
import jax.numpy as jnp
from jax import lax
from jax.experimental import pallas as pl
from jax.experimental.pallas import tpu as pltpu

F32 = jnp.float32
BF16 = jnp.bfloat16

HEAD_DIM = 64
LANES = 128
ATTN_HEADS = 8
RWKV_HEADS = 8
ATTN_WIDTH = ATTN_HEADS * HEAD_DIM
RWKV_WIDTH = RWKV_HEADS * HEAD_DIM
DECAY_LORA = 64
ICLR_LORA = 64
GATE_LORA = 128
DILATED_PATTERNS = ((128, 1), (512, 4), (2048, 16))
RESIDUES = max(dil for _, dil in DILATED_PATTERNS)
BLK = 128
ATTN_GROUP = 8
IN_PROJ_ROWS = 512
FFN_ROWS = 512
NORM_EPS = 1e-6
LNX_EPS = 64e-5
MASKED = -1e30
LOG2_E = 1.4426950408889634
VMEM_LIMIT = 56 * 1024 * 1024

_QB, _KB, _VB = 0, ATTN_WIDTH // LANES, 2 * ATTN_WIDTH // LANES
_N_PAIR = RWKV_WIDTH // LANES


def _nt_dot(a, b):
    return lax.dot_general(a, b, (((1,), (1,)), ((), ())), preferred_element_type=F32)


def _tn_dot(a, b):
    return lax.dot_general(a, b, (((0,), (0,)), ((), ())), preferred_element_type=F32)


def _dot(a, b):
    return jnp.dot(a, b, preferred_element_type=F32)


def _pairwise_dots(lhs, rhs):
    assert len(lhs) == len(rhs) and len(lhs) % 2 == 0
    out = []
    for j in range(0, len(lhs), 2):
        zero = jnp.zeros_like(rhs[j])
        both = _dot(jnp.concatenate([lhs[j], lhs[j + 1]], axis=1),
                    jnp.concatenate([jnp.concatenate([rhs[j], zero], axis=1),
                                     jnp.concatenate([zero, rhs[j + 1]], axis=1)], axis=0))
        out += [both[:, :LANES], both[:, LANES:]]
    return out


def _aligned(index, multiple):
    if isinstance(index, int):
        return index
    return pl.multiple_of(index, multiple)


def _head0_lanes():
    return lax.broadcasted_iota(jnp.int32, (1, LANES), 1) < HEAD_DIM


def _head_averager():
    i = lax.broadcasted_iota(jnp.int32, (LANES, LANES), 0)
    j = lax.broadcasted_iota(jnp.int32, (LANES, LANES), 1)
    return jnp.where((i < HEAD_DIM) == (j < HEAD_DIM), 1.0 / HEAD_DIM, 0.0).astype(BF16)


def _head_mean_mxu(x, averager):
    return _dot(x.astype(BF16), averager)


def _head_sum(x, head0):
    s0 = jnp.sum(jnp.where(head0, x, 0.0), axis=-1, keepdims=True)
    s1 = jnp.sum(jnp.where(head0, 0.0, x), axis=-1, keepdims=True)
    return jnp.where(head0, s0, s1)


def _stack_heads(x, head0):
    return jnp.concatenate([jnp.where(head0, x, 0.0), jnp.where(head0, 0.0, x)], axis=0)


def _window_row_token(u):
    per_residue = BLK // RESIDUES
    return RESIDUES * (u & (per_residue - 1)) + (u >> (per_residue.bit_length() - 1))


def _in_proj_kernel(x_ref, g_ref, w_ref, qk_gain_ref, rw_ref, qkv_ref):
    x = x_ref[...]
    ms = jnp.mean(x * x, axis=-1, keepdims=True)
    xn = (x * lax.rsqrt(ms + NORM_EPS) * g_ref[...]).astype(BF16)
    n_qk = qk_gain_ref.shape[1]
    n_attn = qkv_ref.shape[1]
    qk = _dot(xn, w_ref[:, :n_qk])
    val = _dot(xn, w_ref[:, n_qk:n_attn])
    rw_ref[...] = _dot(xn, w_ref[:, n_attn:])
    head0 = _head0_lanes()
    pieces = []
    for j in range(n_qk // LANES):
        cols = slice(j * LANES, (j + 1) * LANES)
        v = qk[:, cols]
        mean_sq = _head_sum(v * v, head0) * (1.0 / HEAD_DIM)
        pieces.append((v * lax.rsqrt(mean_sq + NORM_EPS) * qk_gain_ref[:, cols]).astype(BF16))
    qkv = jnp.concatenate(pieces + [val.astype(BF16)], axis=1)
    u = lax.broadcasted_iota(jnp.int32, (BLK, BLK), 0)
    t = lax.broadcasted_iota(jnp.int32, (BLK, BLK), 1)
    regroup = jnp.where(t == _window_row_token(u), 1.0, 0.0).astype(BF16)
    for w in range(x.shape[0] // BLK):
        rows = slice(w * BLK, (w + 1) * BLK)
        qkv_ref[rows, :] = _dot(regroup, qkv[rows, :])


def _in_proj(x3d, g, w_bf16, qk_gain, tm):
    b, s, d = x3d.shape
    n = w_bf16.shape[1]
    n_attn = 3 * ATTN_WIDTH
    assert s % tm == 0 and tm % BLK == 0
    return pl.pallas_call(
        _in_proj_kernel,
        grid=(b * s // tm,),
        in_specs=[
            pl.BlockSpec((tm, d), lambda i: (i, 0)),
            pl.BlockSpec((1, d), lambda i: (0, 0)),
            pl.BlockSpec((d, n), lambda i: (0, 0), pipeline_mode=pl.Buffered(1)),
            pl.BlockSpec((1, qk_gain.shape[1]), lambda i: (0, 0)),
        ],
        out_specs=[
            pl.BlockSpec((tm, n - n_attn), lambda i: (i, 0)),
            pl.BlockSpec((tm, n_attn), lambda i: (i, 0)),
        ],
        out_shape=[
            jax.ShapeDtypeStruct((b * s, n - n_attn), F32),
            jax.ShapeDtypeStruct((b * s, n_attn), F32),
        ],
        compiler_params=pltpu.CompilerParams(
            dimension_semantics=("parallel",), vmem_limit_bytes=VMEM_LIMIT),
        name="in_proj",
    )(x3d.reshape(b * s, d), g, w_bf16, qk_gain)


def _attn_kernel(slope_ref, q_ref, k_ref, v_ref, go_ref, o_ref,
                 m_ref, l_ref, acc_ref, bias_ref):
    seq = q_ref.shape[1]
    hp = pl.program_id(0)
    head0 = _head0_lanes()
    averager = _head_averager()
    per_residue = BLK // RESIDUES

    def block_runs(dil, cls, n):
        if dil == 1:
            return [(_aligned(n * BLK, BLK), BLK)]
        return [(_aligned((n * dil + wl) * BLK + (dil * m + cls) * per_residue, per_residue),
                 per_residue) for wl in range(dil) for m in range(RESIDUES // dil)]

    def member_of(u, dil):
        per_window = BLK // dil
        in_window = u & (per_window - 1)
        return (per_window * (u >> (per_window.bit_length() - 1))
                + (RESIDUES // dil) * (u & (per_residue - 1))
                + (in_window >> (per_residue.bit_length() - 1)))

    def load_rows(ref, runs):
        return jnp.concatenate([ref[0, pl.ds(start, length), :] for start, length in runs], axis=0)

    def store_rows(ref, p, runs, value):
        offset = 0
        for start, length in runs:
            ref[p, pl.ds(start, length), :] = value[offset:offset + length]
            offset += length

    @pl.when(pl.program_id(1) == 0)
    def _():
        qi = lax.broadcasted_iota(jnp.int32, (2 * BLK, 2 * BLK), 0)
        ci = lax.broadcasted_iota(jnp.int32, (2 * BLK, 2 * BLK), 1)
        slope = jnp.where(qi < BLK, slope_ref[2 * hp], slope_ref[2 * hp + 1]) * LOG2_E
        for p, (window, dil) in enumerate(DILATED_PATTERNS):
            steps = window // dil
            back = ((BLK + member_of(qi & (BLK - 1), dil))
                    - ((ci & BLK) + member_of(ci & (BLK - 1), dil)))
            valid = (back >= 0) & (back <= steps)
            bias_ref[p] = jnp.where(valid, -slope * (back.astype(F32) * dil), MASKED)

    ones_keys = jnp.ones((2 * BLK, LANES), BF16)

    def band_blocks(p, dil, blocks):
        each = lambda f, *lists: [f(*args) for args in zip(*lists)]
        bf = lambda v: v.astype(BF16)
        runs = [block_runs(dil, cls, n) for cls, n in blocks]
        has_prevs = [n > 0 for _, n in blocks]

        def keys_values(block, own, has_prev):
            kb = load_rows(k_ref, own)
            vb = load_rows(v_ref, own)
            if has_prev:
                prev = block_runs(dil, block[0], block[1] - 1)
                kb = jnp.concatenate([load_rows(k_ref, prev), kb], axis=0)
                vb = jnp.concatenate([load_rows(v_ref, prev), vb], axis=0)
            return bf(kb), jnp.concatenate([bf(vb), ones_keys[:vb.shape[0]]], axis=1)
        kv = each(keys_values, blocks, runs, has_prevs)
        qs = each(lambda own: bf(_stack_heads(load_rows(q_ref, own), head0)), runs)
        s = each(lambda q, kvb, has_prev:
                 _nt_dot(q, kvb[0]) + (bias_ref[p] if has_prev else bias_ref[p, :, BLK:]),
                 qs, kv, has_prevs)
        m_blk = each(lambda v: jnp.max(v, axis=-1, keepdims=True), s)
        pr = each(lambda v, m: bf(jnp.exp2(v - m)), s, m_blk)
        pv = each(lambda v, kvb: _dot(v, kvb[1]), pr, kv)

        m_pair = each(lambda m_b: jnp.where(head0, m_b[:BLK], m_b[BLK:]), m_blk)
        acc_pair = each(lambda v: jnp.where(head0, v[:BLK, :LANES], v[BLK:, :LANES]), pv)
        l_pair = each(lambda v: jnp.where(head0, v[:BLK, LANES:], v[BLK:, LANES:]), pv)
        if p + 1 < len(DILATED_PATTERNS):
            def keep(own, m_b, acc_b, l_b):
                store_rows(m_ref, p, own, m_b)
                store_rows(acc_ref, p, own, acc_b)
                store_rows(l_ref, p, own, l_b)
            each(keep, runs, m_pair, acc_pair, l_pair)
            return

        assert dil == RESIDUES and seq == BLK * RESIDUES
        part = lambda ref, q, own: jnp.concatenate(
            [ref[q, pl.ds(start, length), :] for start, length in own], axis=0)
        m_prev = [[part(m_ref, q, own) for q in range(p)] for own in runs]
        m_all = each(lambda mine, others: functools.reduce(jnp.maximum, others, mine),
                     m_pair, m_prev)
        w_mine = each(lambda mine, top: jnp.exp2(mine - top), m_pair, m_all)
        w_prev = each(lambda others, top: [jnp.exp2(m - top) for m in others], m_prev, m_all)
        num = each(lambda own, wm, a, ws:
                   wm * a + sum(w * part(acc_ref, q, own) for q, w in enumerate(ws)),
                   runs, w_mine, acc_pair, w_prev)
        den = each(lambda own, wm, l, ws:
                   wm * l + sum(w * part(l_ref, q, own) for q, w in enumerate(ws)),
                   runs, w_mine, l_pair, w_prev)
        o = each(lambda a, b: a / b, num, den)
        ms = each(lambda v: _head_mean_mxu(v * v, averager), o)
        for (cls, _), v, mean_sq in zip(blocks, o, ms):
            o_ref[0, pl.ds(cls, BLK, stride=RESIDUES), :] = (
                v * lax.rsqrt(mean_sq + NORM_EPS) * go_ref[...])

    for p, (window, dil) in enumerate(DILATED_PATTERNS):
        n_blk = seq // (BLK * dil)
        group = ATTN_GROUP
        if n_blk >= group:
            assert n_blk % group == 0
            groups = n_blk // group

            def residue(r, carry, p=p, dil=dil, groups=groups, group=group):
                for grp in range(groups):
                    band_blocks(p, dil, [(r, n) for n in range(grp * group, (grp + 1) * group)])
                return carry
        else:
            assert group % n_blk == 0 and dil % (group // n_blk) == 0
            per_group = group // n_blk

            def residue(i, carry, p=p, dil=dil, n_blk=n_blk, per_group=per_group):
                classes = [i * per_group + g for g in range(per_group)]
                band_blocks(p, dil, [(r, n) for r in classes for n in range(n_blk)])
                return carry
        n_steps = dil if n_blk >= group else dil // (group // n_blk)
        if n_steps == 1:
            residue(0, 0)
        else:
            lax.fori_loop(0, n_steps, residue, 0)


def _attention(qkv, slopes, go):
    b, s, _ = qkv.shape
    assert BLK & (BLK - 1) == 0 and s % (RESIDUES * BLK) == 0
    proj = qkv
    blk = lambda off: pl.BlockSpec((1, s, LANES), lambda j, i: (i, 0, off + j))
    return pl.pallas_call(
        _attn_kernel,
        grid=(_N_PAIR, b),
        in_specs=[
            pl.BlockSpec(memory_space=pltpu.SMEM),
            blk(_QB), blk(_KB), blk(_VB),
            pl.BlockSpec((1, LANES), lambda j, i: (0, j)),
        ],
        out_specs=pl.BlockSpec((1, s, LANES), lambda j, i: (i, 0, j)),
        out_shape=jax.ShapeDtypeStruct((b, s, ATTN_WIDTH), F32),
        scratch_shapes=[
            pltpu.VMEM((len(DILATED_PATTERNS) - 1, s, LANES), F32),
            pltpu.VMEM((len(DILATED_PATTERNS) - 1, s, LANES), F32),
            pltpu.VMEM((len(DILATED_PATTERNS) - 1, s, LANES), F32),
            pltpu.VMEM((len(DILATED_PATTERNS), 2 * BLK, 2 * BLK), F32),
        ],
        compiler_params=pltpu.CompilerParams(
            dimension_semantics=("arbitrary", "arbitrary"), vmem_limit_bytes=VMEM_LIMIT),
        name="dilated_attn",
    )(slopes, proj, proj, proj, go)


def _wkv_kernel(pr_ref, pk_ref, pv_ref, pwa_ref, pg_ref,
                mur_ref, muk_ref, muv_ref, muwa_ref, mug_ref,
                w0_ref, a0_ref, kk_ref, ka_ref, rk_ref, lng_ref, lnb_ref,
                wwa_ref, g2_ref, o_ref, h_ref, *slots):
    seq = pr_ref.shape[1]
    n_chunks = seq // BLK
    head0 = _head0_lanes()
    row = lax.broadcasted_iota(jnp.int32, (BLK, BLK), 0)
    col = lax.broadcasted_iota(jnp.int32, (BLK, BLK), 1)
    strict = col < row
    incl = col <= row
    tril_incl = jnp.where(incl, 1.0, 0.0).astype(BF16)
    first_row = lax.broadcasted_iota(jnp.int32, (BLK, LANES), 0) == 0
    srow = lax.broadcasted_iota(jnp.int32, (2 * BLK, LANES), 0) < BLK
    slane = lax.broadcasted_iota(jnp.int32, (2 * BLK, LANES), 1) < HEAD_DIM
    own_head = srow == slane
    zeros_blk = jnp.zeros((BLK, BLK), F32)
    zeros_bf = jnp.zeros((2 * BLK, 2 * BLK), BF16)
    eye_bf = jnp.where(lax.broadcasted_iota(jnp.int32, (2 * BLK, 2 * BLK), 0)
                       == lax.broadcasted_iota(jnp.int32, (2 * BLK, 2 * BLK), 1),
                       1.0, 0.0).astype(BF16)

    h_ref[...] = jnp.zeros(h_ref.shape, F32)

    each = lambda f, *lists: [f(*args) for args in zip(*lists)]
    bf = lambda v: v.astype(BF16)
    pairs = range(_N_PAIR)
    lanes = [slice(j * LANES, (j + 1) * LANES) for j in pairs]
    param = lambda ref: [ref[:, sl] for sl in lanes]

    def produce(c, slot):
        row0 = _aligned(c * BLK, BLK)
        rows = pl.ds(row0, BLK)
        prev8 = pl.ds(_aligned(max(row0 - 8, 0) if isinstance(row0, int)
                               else jnp.maximum(row0 - 8, 0), 8), 8)
        has_prev = jnp.where(c > 0, 1.0, 0.0).astype(F32)

        def shifted(ref, mu_ref, sl):
            p = ref[0, rows, sl]
            last = ref[0, prev8, sl][7:8, :] * has_prev
            prev = jnp.where(first_row, last, pltpu.roll(p, 1, 0))
            return p + (prev - p) * mu_ref[:, sl]

        xr = [shifted(pr_ref, mur_ref, sl) for sl in lanes]
        xk = [shifted(pk_ref, muk_ref, sl) for sl in lanes]
        xv = [shifted(pv_ref, muv_ref, sl) for sl in lanes]
        xwa = shifted(pwa_ref, muwa_ref, slice(None))
        xg = shifted(pg_ref, mug_ref, slice(None))
        yield

        lora_all = _dot(bf(jnp.where(head0, jnp.tanh(xwa), xwa)), wwa_ref[...])
        lora = [lora_all[:, 2 * LANES * j:2 * LANES * (j + 1)] for j in pairs]
        gate_all = _dot(bf(jax.nn.sigmoid(xg)), g2_ref[...])
        gate = [gate_all[:, sl] for sl in lanes]
        yield

        def decay_log(lo, w0):
            zw = -(w0 + lo[:, :LANES])
            softplus = jnp.maximum(zw, 0.0) + jnp.log(1.0 + jnp.exp(-jnp.abs(zw)))
            return -jnp.exp(-softplus - 0.5)
        log_decay = each(decay_log, lora, param(w0_ref))
        a_sig = each(lambda lo, a0: jax.nn.sigmoid(a0 + lo[:, LANES:]), lora, param(a0_ref))

        def unit_key(k, k_k):
            kkv = k * k_k
            return kkv / jnp.maximum(jnp.sqrt(_head_sum(kkv * kkv, head0)), 1e-12)
        kk = each(unit_key, xk, param(kk_ref))
        k2 = each(lambda k, a, k_a: k * (1.0 + (a - 1.0) * k_a), xk, a_sig, param(ka_ref))
        b_vec = each(lambda u, a: u * a, kk, a_sig)
        yield

        def cumulative(ld):
            ld_hi = bf(ld)
            ld_lo = bf(ld - ld_hi.astype(F32))
            cum = _dot(tril_incl, jnp.concatenate([ld_hi, ld_lo], axis=1))
            return cum[:, :LANES] + cum[:, LANES:]
        lw = each(cumulative, log_decay)
        yield
        lw_last = [v[BLK - 1:BLK, :] for v in lw]
        w_inv = each(lambda v: jnp.exp(-v), lw)
        w_tail = each(lambda v, last: jnp.exp(last - v), lw, lw_last)
        w_all = each(jnp.exp, lw_last)

        at = each(lambda u, v, ld: -u * jnp.exp(v - ld), kk, lw, log_decay)
        rt = each(lambda r, v: r * jnp.exp(v), xr, lw)
        bt = each(lambda b, w: b * w, b_vec, w_inv)
        kt = each(lambda k, w: k * w, k2, w_inv)
        bh = each(lambda b, w: b * w, b_vec, w_tail)
        kh = each(lambda k, w: k * w, k2, w_tail)
        yield

        def pair_products(a, r, b, k):
            lhs = jnp.concatenate([jnp.where(head0, a, 0.0), jnp.where(head0, r, 0.0),
                                   jnp.where(head0, 0.0, a), jnp.where(head0, 0.0, r)], axis=0)
            return _nt_dot(bf(lhs), bf(jnp.concatenate([b, k], axis=0)))
        pp = each(pair_products, at, rt, bt, kt)
        yield

        def n_matrix(q):
            aab0 = jnp.where(strict, q[0:BLK, :BLK], 0.0)
            aab1 = jnp.where(strict, q[2 * BLK:3 * BLK, :BLK], 0.0)
            return bf(jnp.concatenate([jnp.concatenate([aab0, zeros_blk], axis=1),
                                       jnp.concatenate([zeros_blk, aab1], axis=1)], axis=0))

        def aak_matrix(q):
            return bf(jnp.concatenate([jnp.where(strict, q[0:BLK, BLK:], 0.0),
                                       jnp.where(strict, q[2 * BLK:3 * BLK, BLK:], 0.0)], axis=0))
        akv = _pairwise_dots(each(aak_matrix, pp), each(bf, xv))

        def x_init(a, av):
            return jnp.concatenate([_stack_heads(a, head0), jnp.where(own_head, av, 0.0)], axis=1)

        def read_matrices(q):
            arb = jnp.concatenate([jnp.where(incl, q[BLK:2 * BLK, :BLK], 0.0),
                                   jnp.where(incl, q[3 * BLK:, :BLK], 0.0)], axis=1)
            ark = jnp.concatenate([jnp.where(incl, q[BLK:2 * BLK, BLK:], 0.0),
                                   jnp.where(incl, q[3 * BLK:, BLK:], 0.0)], axis=1)
            return bf(arb), bf(ark)

        for j in pairs:
            slot["n"][j] = n_matrix(pp[j])
            slot["x"][j] = bf(x_init(at[j], akv[j]))
            slot["arb"][j], slot["ark"][j] = read_matrices(pp[j])
        yield
        r_k = param(rk_ref)
        for j in pairs:
            slot["rt"][j] = rt[j]
            slot["bhs"][j] = bf(_stack_heads(bh[j], head0))
            slot["khs"][j] = bf(_stack_heads(kh[j], head0))
            slot["vs"][j] = bf(_stack_heads(xv[j], head0))
            slot["wall"][j] = jnp.broadcast_to(w_all[j], (8, LANES))
            slot["bonus"][j] = _head_sum(xr[j] * k2[j] * r_k[j], head0) * xv[j]
            slot["gate"][j] = gate[j]

    def consume(c, slot):
        n_chunk = _N_PAIR
        n_bf = [slot["n"][j] for j in range(n_chunk)]
        x_bf = [slot["x"][j] for j in range(n_chunk)]

        def live_rows(m, skip):
            if not skip:
                return m
            return jnp.concatenate([m[skip:BLK], m[BLK + skip:]], axis=0)

        def merge_rows(old, new, skip):
            if not skip:
                return new
            live = BLK - skip
            return jnp.concatenate([old[:skip], new[:live], old[BLK:BLK + skip], new[live:]],
                                   axis=0)

        span = 1
        while span < BLK:
            skip = span if span >= 16 else 0
            x_bf = each(lambda xx, nn: merge_rows(
                xx, bf(_dot(live_rows(nn + eye_bf, skip), xx)), skip), x_bf, n_bf)
            span *= 2
            if span < BLK:
                skip = span if span >= 16 else 0
                n_bf = each(lambda nn: merge_rows(
                    zeros_bf, bf(_dot(live_rows(nn, skip), nn)), skip), n_bf)
            yield
        v_s = [slot["vs"][j] for j in range(n_chunk)]

        def read_side(j, xb, vs):
            z = _dot(slot["arb"][j], xb)
            return slot["rt"][j] + z[:, :LANES], z[:, LANES:] + _dot(slot["ark"][j], vs)
        r_eff_y0 = each(read_side, range(n_chunk), x_bf, v_s)
        yield

        def state_side(j, xb, vs):
            d1 = _tn_dot(slot["bhs"][j], xb)
            d2 = _tn_dot(slot["khs"][j], vs)
            w_all = slot["wall"][j][0:1, :]
            return jnp.where(row == col, w_all, 0.0) + d1[:, :LANES], d1[:, LANES:] + d2
        m_eff_g_eff = each(state_side, range(n_chunk), x_bf, v_s)
        rm_lhs = each(lambda ry, mg: bf(jnp.concatenate([ry[0], mg[0]], axis=0)),
                      r_eff_y0, m_eff_g_eff)
        yield

        rm = _pairwise_dots(rm_lhs, [bf(h_ref[j]) for j in pairs])
        y = each(lambda v, ry: v[:BLK] + ry[1], rm, r_eff_y0)
        for j in pairs:
            h_ref[j] = rm[j][BLK:] + m_eff_g_eff[j][1]
        yield
        mean = each(lambda v: _head_sum(v, head0) * (1.0 / HEAD_DIM), y)
        yc = each(lambda v, m: v - m, y, mean)
        var = each(lambda v: _head_sum(v * v, head0) * (1.0 / HEAD_DIM), yc)
        rows = pl.ds(_aligned(c * BLK, BLK), BLK)
        lnx_g, lnx_b = param(lng_ref), param(lnb_ref)
        for j in pairs:
            yn = yc[j] * lax.rsqrt(var[j] + LNX_EPS) * lnx_g[j] + lnx_b[j]
            o_ref[0, rows, lanes[j]] = (yn + slot["bonus"][j]) * slot["gate"][j]

    def alternate(*stages):
        live = list(stages)
        while live:
            for gen in list(live):
                try:
                    next(gen)
                except StopIteration:
                    live.remove(gen)

    names = ("n", "x", "arb", "ark", "rt", "bhs", "khs", "vs", "wall", "bonus", "gate")
    slot_a = dict(zip(names, slots[:len(names)]))
    slot_b = dict(zip(names, slots[len(names):]))
    assert n_chunks % 2 == 0

    alternate(produce(0, slot_a))

    def chunk_pair(i, carry):
        c = 2 * i
        alternate(consume(c, slot_a), produce(c + 1, slot_b))
        alternate(consume(c + 1, slot_b), produce(jnp.minimum(c + 2, n_chunks - 1), slot_a))
        return carry

    lax.fori_loop(0, n_chunks // 2, chunk_pair, 0)


def _wkv_slot_shapes():
    g = _N_PAIR
    return [
        pltpu.VMEM((g, 2 * BLK, 2 * BLK), BF16),
        pltpu.VMEM((g, 2 * BLK, 2 * LANES), BF16),
        pltpu.VMEM((g, BLK, 2 * BLK), BF16),
        pltpu.VMEM((g, BLK, 2 * BLK), BF16),
        pltpu.VMEM((g, BLK, LANES), F32),
        pltpu.VMEM((g, 2 * BLK, LANES), BF16),
        pltpu.VMEM((g, 2 * BLK, LANES), BF16),
        pltpu.VMEM((g, 2 * BLK, LANES), BF16),
        pltpu.VMEM((g, 8, LANES), F32),
        pltpu.VMEM((g, BLK, LANES), F32),
        pltpu.VMEM((g, BLK, LANES), F32),
    ]


def _rwkv(proj, mu, w0, a0, k_k, k_a, r_k, lnx_g, lnx_b, wwa, g2_bf16):
    b, s, _ = proj.shape
    np_ = _N_PAIR
    w = RWKV_WIDTH
    wide = lambda off: pl.BlockSpec((1, s, w), lambda i: (i, 0, off))
    lane = lambda off: pl.BlockSpec((1, s, LANES), lambda i: (i, 0, off))
    vec_w = lambda off: pl.BlockSpec((1, w), lambda i: (0, off))
    vec_l = lambda off: pl.BlockSpec((1, LANES), lambda i: (0, off))
    full = lambda shape: pl.BlockSpec(shape, lambda i: (0,) * len(shape))
    return pl.pallas_call(
        _wkv_kernel,
        grid=(b,),
        in_specs=[
            wide(0), wide(1), wide(2),
            lane(3 * np_), lane(3 * np_ + 1),
            vec_w(0), vec_w(1), vec_w(2), vec_l(3 * np_), vec_l(3 * np_ + 1),
            vec_w(0), vec_w(0), vec_w(0), vec_w(0), vec_w(0), vec_w(0), vec_w(0),
            full((LANES, np_ * 2 * LANES)),
            full((GATE_LORA, w)),
        ],
        out_specs=pl.BlockSpec((1, s, w), lambda i: (i, 0, 0)),
        out_shape=jax.ShapeDtypeStruct((b, s, RWKV_WIDTH), F32),
        scratch_shapes=[pltpu.VMEM((np_, LANES, LANES), F32)] + 2 * _wkv_slot_shapes(),
        compiler_params=pltpu.CompilerParams(
            dimension_semantics=("parallel",), vmem_limit_bytes=VMEM_LIMIT),
        name="rwkv7",
    )(proj, proj, proj, proj, proj, mu, mu, mu, mu, mu,
      w0, a0, k_k, k_a, r_k, lnx_g, lnx_b, wwa, g2_bf16)


def _out_ffn_kernel(x_ref, attn_ref, rwkv_ref, woa_ref, wor_ref, g2_ref,
                    wg_ref, wu_ref, wd_ref, o_ref):
    x1 = (x_ref[...] + _dot(attn_ref[...].astype(BF16), woa_ref[...])
          + _dot(rwkv_ref[...].astype(BF16), wor_ref[...]))
    ms = jnp.mean(x1 * x1, axis=-1, keepdims=True)
    xn = (x1 * lax.rsqrt(ms + NORM_EPS) * g2_ref[...]).astype(BF16)
    gate = _dot(xn, wg_ref[...])
    up = _dot(xn, wu_ref[...])
    hidden = (gate * jax.nn.sigmoid(gate) * up).astype(BF16)
    o_ref[...] = x1 + _dot(hidden, wd_ref[...])


def _out_ffn(x2d, attn2d, rwkv2d, woa, wor, g2, wg, wu, wd, tm):
    t, d = x2d.shape
    f = wg.shape[1]
    tok = lambda w: pl.BlockSpec((tm, w), lambda i: (i, 0))
    res = lambda shape: pl.BlockSpec(shape, lambda i: (0, 0), pipeline_mode=pl.Buffered(1))
    return pl.pallas_call(
        _out_ffn_kernel,
        grid=(t // tm,),
        in_specs=[
            tok(d), tok(ATTN_WIDTH), tok(RWKV_WIDTH),
            res((ATTN_WIDTH, d)), res((RWKV_WIDTH, d)),
            pl.BlockSpec((1, d), lambda i: (0, 0)),
            res((d, f)), res((d, f)), res((f, d)),
        ],
        out_specs=tok(d),
        out_shape=jax.ShapeDtypeStruct((t, d), F32),
        compiler_params=pltpu.CompilerParams(
            dimension_semantics=("parallel",), vmem_limit_bytes=VMEM_LIMIT),
        name="out_ffn",
    )(x2d, attn2d, rwkv2d, woa, wor, g2, wg, wu, wd)


def _layer(x, norm1_g, w_in, q_norm_g, k_norm_g, attn_out_g, rwkv_mu, w0, w2, a0, a2, g2,
           k_k, k_a, r_k, lnx_g, lnx_b, w_out, norm2_g, w_gate, w_up, w_down):
    b, s, d = x.shape
    assert s % (BLK * max(dil for _, dil in DILATED_PATTERNS)) == 0
    assert all(window // dil == BLK for window, dil in DILATED_PATTERNS)
    t = b * s
    assert t % IN_PROJ_ROWS == 0 and t % FFN_ROWS == 0
    row = lambda v: v.reshape(1, -1).astype(F32)

    x2d = x.reshape(t, d)
    per_head = lambda v: jnp.tile(v.reshape(1, HEAD_DIM), (1, ATTN_HEADS)).astype(F32)
    qk_gain = jnp.concatenate([per_head(q_norm_g) * (HEAD_DIM ** -0.5 * LOG2_E),
                               per_head(k_norm_g)], axis=1)
    proj, qkv = _in_proj(x, row(norm1_g), w_in.astype(BF16), qk_gain, IN_PROJ_ROWS)
    proj = proj.reshape(b, s, -1)

    slopes = jnp.exp2(-8.0 * jnp.arange(1, ATTN_HEADS + 1, dtype=F32) / ATTN_HEADS)
    attn = _attention(qkv.reshape(b, s, -1), slopes, row(attn_out_g))

    w2p = w2.reshape(DECAY_LORA, _N_PAIR, LANES).transpose(1, 0, 2)
    a2p = a2.reshape(ICLR_LORA, _N_PAIR, LANES).transpose(1, 0, 2)
    zero = jnp.zeros_like(w2p)
    wwa = jnp.concatenate([jnp.concatenate([w2p, zero], axis=2),
                           jnp.concatenate([jnp.zeros_like(a2p), a2p], axis=2)], axis=1)
    wwa = wwa.transpose(1, 0, 2).reshape(LANES, _N_PAIR * 2 * LANES).astype(BF16)
    rwkv = _rwkv(proj, row(rwkv_mu), row(w0), row(a0), row(k_k), row(k_a), row(r_k),
                 row(lnx_g), row(lnx_b), wwa, g2.astype(BF16))

    out = _out_ffn(x2d, attn.reshape(t, ATTN_WIDTH), rwkv.reshape(t, RWKV_WIDTH),
                   w_out[:ATTN_WIDTH].astype(BF16), w_out[ATTN_WIDTH:].astype(BF16),
                   row(norm2_g), w_gate.astype(BF16), w_up.astype(BF16), w_down.astype(BF16),
                   FFN_ROWS)
    return out.reshape(b, s, d)


def kernel(x, norm1_g, w_in, q_norm_g, k_norm_g, attn_out_g, rwkv_mu, w0, w2, a0, a2, g2,
           k_k, k_a, r_k, lnx_g, lnx_b, w_out, norm2_g, w_gate, w_up, w_down):
    h = x
    for layer in range(norm1_g.shape[0]):
        h = _layer(h, norm1_g[layer], w_in[layer], q_norm_g[layer], k_norm_g[layer],
                   attn_out_g[layer], rwkv_mu[layer], w0[layer], w2[layer], a0[layer],
                   a2[layer], g2[layer], k_k[layer], k_a[layer], r_k[layer], lnx_g[layer],
                   lnx_b[layer], w_out[layer], norm2_g[layer], w_gate[layer], w_up[layer],
                   w_down[layer])
    return h
```

```python
import functools

import jax
import jax.numpy as jnp
from jax import lax
from jax.experimental import pallas as pl
from jax.experimental.pallas import tpu as pltpu

F32 = jnp.float32
BF16 = jnp.bfloat16

HEAD_DIM = 64
LANES = 128
ATTN_HEADS = 8
RWKV_HEADS = 8
ATTN_WIDTH = ATTN_HEADS * HEAD_DIM
RWKV_WIDTH = RWKV_HEADS * HEAD_DIM
DECAY_LORA = 64
ICLR_LORA = 64
GATE_LORA = 128
DILATED_PATTERNS = ((128, 1), (512, 4), (2048, 16))
RESIDUES = max(dil for _, dil in DILATED_PATTERNS)
BLK = 128
ATTN_GROUP = 8
IN_PROJ_ROWS = 1024
FFN_ROWS = 512
NORM_EPS = 1e-6
LNX_EPS = 64e-5
MASKED = -1e30
LOG2_E = 1.4426950408889634
VMEM_LIMIT = 56 * 1024 * 1024

_QB, _KB, _VB = 0, ATTN_WIDTH // LANES, 2 * ATTN_WIDTH // LANES
_N_PAIR = RWKV_WIDTH // LANES


def _nt_dot(a, b):
    return lax.dot_general(a, b, (((1,), (1,)), ((), ())), preferred_element_type=F32)


def _tn_dot(a, b):
    return lax.dot_general(a, b, (((0,), (0,)), ((), ())), preferred_element_type=F32)


def _dot(a, b):
    return jnp.dot(a, b, preferred_element_type=F32)


def _pairwise_dots(lhs, rhs):
    assert len(lhs) == len(rhs) and len(lhs) % 2 == 0
    out = []
    for j in range(0, len(lhs), 2):
        zero = jnp.zeros_like(rhs[j])
        both = _dot(jnp.concatenate([lhs[j], lhs[j + 1]], axis=1),
                    jnp.concatenate([jnp.concatenate([rhs[j], zero], axis=1),
                                     jnp.concatenate([zero, rhs[j + 1]], axis=1)], axis=0))
        out += [both[:, :LANES], both[:, LANES:]]
    return out


def _aligned(index, multiple):
    if isinstance(index, int):
        return index
    return pl.multiple_of(index, multiple)


def _head0_lanes():
    return lax.broadcasted_iota(jnp.int32, (1, LANES), 1) < HEAD_DIM


def _head_averager():
    i = lax.broadcasted_iota(jnp.int32, (LANES, LANES), 0)
    j = lax.broadcasted_iota(jnp.int32, (LANES, LANES), 1)
    return jnp.where((i < HEAD_DIM) == (j < HEAD_DIM), 1.0 / HEAD_DIM, 0.0).astype(BF16)


def _head_mean_mxu(x, averager):
    return _dot(x.astype(BF16), averager)


def _head_sum(x, head0):
    s0 = jnp.sum(jnp.where(head0, x, 0.0), axis=-1, keepdims=True)
    s1 = jnp.sum(jnp.where(head0, 0.0, x), axis=-1, keepdims=True)
    return jnp.where(head0, s0, s1)


def _stack_heads(x, head0):
    return jnp.concatenate([jnp.where(head0, x, 0.0), jnp.where(head0, 0.0, x)], axis=0)


def _window_row_token(u):
    per_residue = BLK // RESIDUES
    return RESIDUES * (u & (per_residue - 1)) + (u >> (per_residue.bit_length() - 1))


def _in_proj_kernel(x_ref, g_ref, w_ref, qk_gain_ref, rw_ref, qkv_ref):
    x = x_ref[...]
    ms = jnp.mean(x * x, axis=-1, keepdims=True)
    xn = (x * lax.rsqrt(ms + NORM_EPS) * g_ref[...]).astype(BF16)
    n_qk = qk_gain_ref.shape[1]
    n_attn = qkv_ref.shape[1]
    rw_ref[...] = _dot(xn, w_ref[:, n_attn:])
    u = lax.broadcasted_iota(jnp.int32, (BLK, BLK), 0)
    t = lax.broadcasted_iota(jnp.int32, (BLK, BLK), 1)
    regroup = jnp.where(t == _window_row_token(u), 1.0, 0.0).astype(BF16)
    xg = jnp.concatenate([_dot(regroup, xn[w * BLK:(w + 1) * BLK]).astype(BF16)
                          for w in range(x.shape[0] // BLK)], axis=0)
    qk = _dot(xg, w_ref[:, :n_qk])
    qkv_ref[:, n_qk:] = _dot(xg, w_ref[:, n_qk:n_attn])
    head0 = _head0_lanes()
    for j in range(n_qk // LANES):
        cols = slice(j * LANES, (j + 1) * LANES)
        v = qk[:, cols]
        mean_sq = _head_sum(v * v, head0) * (1.0 / HEAD_DIM)
        qkv_ref[:, cols] = v * lax.rsqrt(mean_sq + NORM_EPS) * qk_gain_ref[:, cols]


def _in_proj(x3d, g, w_bf16, qk_gain, tm):
    b, s, d = x3d.shape
    n = w_bf16.shape[1]
    n_attn = 3 * ATTN_WIDTH
    assert s % tm == 0 and tm % BLK == 0
    return pl.pallas_call(
        _in_proj_kernel,
        grid=(b * s // tm,),
        in_specs=[
            pl.BlockSpec((tm, d), lambda i: (i, 0)),
            pl.BlockSpec((1, d), lambda i: (0, 0)),
            pl.BlockSpec((d, n), lambda i: (0, 0), pipeline_mode=pl.Buffered(1)),
            pl.BlockSpec((1, qk_gain.shape[1]), lambda i: (0, 0)),
        ],
        out_specs=[
            pl.BlockSpec((tm, n - n_attn), lambda i: (i, 0)),
            pl.BlockSpec((tm, n_attn), lambda i: (i, 0)),
        ],
        out_shape=[
            jax.ShapeDtypeStruct((b * s, n - n_attn), F32),
            jax.ShapeDtypeStruct((b * s, n_attn), F32),
        ],
        compiler_params=pltpu.CompilerParams(
            dimension_semantics=("parallel",), vmem_limit_bytes=VMEM_LIMIT),
        name="in_proj",
    )(x3d.reshape(b * s, d), g, w_bf16, qk_gain)


def _attn_kernel(slope_ref, q_ref, k_ref, v_ref, go_ref, o_ref,
                 m_ref, l_ref, acc_ref, bias_ref):
    seq = q_ref.shape[1]
    hp = pl.program_id(0)
    head0 = _head0_lanes()
    averager = _head_averager()
    per_residue = BLK // RESIDUES

    def block_runs(dil, cls, n):
        if dil == 1:
            return [(_aligned(n * BLK, BLK), BLK)]
        return [(_aligned((n * dil + wl) * BLK + (dil * m + cls) * per_residue, per_residue),
                 per_residue) for wl in range(dil) for m in range(RESIDUES // dil)]

    def member_of(u, dil):
        per_window = BLK // dil
        in_window = u & (per_window - 1)
        return (per_window * (u >> (per_window.bit_length() - 1))
                + (RESIDUES // dil) * (u & (per_residue - 1))
                + (in_window >> (per_residue.bit_length() - 1)))

    def load_rows(ref, runs):
        return jnp.concatenate([ref[0, pl.ds(start, length), :] for start, length in runs], axis=0)

    def store_rows(ref, p, runs, value):
        offset = 0
        for start, length in runs:
            ref[p, pl.ds(start, length), :] = value[offset:offset + length]
            offset += length

    @pl.when(pl.program_id(1) == 0)
    def _():
        qi = lax.broadcasted_iota(jnp.int32, (2 * BLK, 2 * BLK), 0)
        ci = lax.broadcasted_iota(jnp.int32, (2 * BLK, 2 * BLK), 1)
        slope = jnp.where(qi < BLK, slope_ref[2 * hp], slope_ref[2 * hp + 1]) * LOG2_E
        for p, (window, dil) in enumerate(DILATED_PATTERNS):
            steps = window // dil
            back = ((BLK + member_of(qi & (BLK - 1), dil))
                    - ((ci & BLK) + member_of(ci & (BLK - 1), dil)))
            valid = (back >= 0) & (back <= steps)
            bias_ref[p] = jnp.where(valid, -slope * (back.astype(F32) * dil), MASKED)

    ones_keys = jnp.ones((2 * BLK, LANES), BF16)

    def band_blocks(p, dil, blocks):
        each = lambda f, *lists: [f(*args) for args in zip(*lists)]
        bf = lambda v: v.astype(BF16)
        runs = [block_runs(dil, cls, n) for cls, n in blocks]
        has_prevs = [n > 0 for _, n in blocks]

        def keys_values(block, own, has_prev):
            kb = load_rows(k_ref, own)
            vb = load_rows(v_ref, own)
            if has_prev:
                prev = block_runs(dil, block[0], block[1] - 1)
                kb = jnp.concatenate([load_rows(k_ref, prev), kb], axis=0)
                vb = jnp.concatenate([load_rows(v_ref, prev), vb], axis=0)
            return bf(kb), jnp.concatenate([bf(vb), ones_keys[:vb.shape[0]]], axis=1)
        kv = each(keys_values, blocks, runs, has_prevs)
        qs = each(lambda own: bf(_stack_heads(load_rows(q_ref, own), head0)), runs)
        s = each(lambda q, kvb, has_prev:
                 _nt_dot(q, kvb[0]) + (bias_ref[p] if has_prev else bias_ref[p, :, BLK:]),
                 qs, kv, has_prevs)
        m_blk = each(lambda v: jnp.max(v, axis=-1, keepdims=True), s)
        pr = each(lambda v, m: bf(jnp.exp2(v - m)), s, m_blk)
        pv = each(lambda v, kvb: _dot(v, kvb[1]), pr, kv)

        m_pair = each(lambda m_b: jnp.where(head0, m_b[:BLK], m_b[BLK:]), m_blk)
        acc_pair = each(lambda v: jnp.where(head0, v[:BLK, :LANES], v[BLK:, :LANES]), pv)
        l_pair = each(lambda v: jnp.where(head0, v[:BLK, LANES:], v[BLK:, LANES:]), pv)
        if p + 1 < len(DILATED_PATTERNS):
            def keep(own, m_b, acc_b, l_b):
                store_rows(m_ref, p, own, m_b)
                store_rows(acc_ref, p, own, acc_b)
                store_rows(l_ref, p, own, l_b)
            each(keep, runs, m_pair, acc_pair, l_pair)
            return

        assert dil == RESIDUES and seq == BLK * RESIDUES
        part = lambda ref, q, own: jnp.concatenate(
            [ref[q, pl.ds(start, length), :] for start, length in own], axis=0)
        m_prev = [[part(m_ref, q, own) for q in range(p)] for own in runs]
        m_all = each(lambda mine, others: functools.reduce(jnp.maximum, others, mine),
                     m_pair, m_prev)
        w_mine = each(lambda mine, top: jnp.exp2(mine - top), m_pair, m_all)
        w_prev = each(lambda others, top: [jnp.exp2(m - top) for m in others], m_prev, m_all)
        num = each(lambda own, wm, a, ws:
                   wm * a + sum(w * part(acc_ref, q, own) for q, w in enumerate(ws)),
                   runs, w_mine, acc_pair, w_prev)
        den = each(lambda own, wm, l, ws:
                   wm * l + sum(w * part(l_ref, q, own) for q, w in enumerate(ws)),
                   runs, w_mine, l_pair, w_prev)
        o = each(lambda a, b: a / b, num, den)
        ms = each(lambda v: _head_mean_mxu(v * v, averager), o)
        for (cls, _), v, mean_sq in zip(blocks, o, ms):
            o_ref[0, pl.ds(cls, BLK, stride=RESIDUES), :] = (
                v * lax.rsqrt(mean_sq + NORM_EPS) * go_ref[...])

    for p, (window, dil) in enumerate(DILATED_PATTERNS):
        n_blk = seq // (BLK * dil)
        group = ATTN_GROUP
        if n_blk >= group:
            assert n_blk % group == 0
            groups = n_blk // group

            def residue(r, carry, p=p, dil=dil, groups=groups, group=group):
                for grp in range(groups):
                    band_blocks(p, dil, [(r, n) for n in range(grp * group, (grp + 1) * group)])
                return carry
        else:
            assert group % n_blk == 0 and dil % (group // n_blk) == 0
            per_group = group // n_blk

            def residue(i, carry, p=p, dil=dil, n_blk=n_blk, per_group=per_group):
                classes = [i * per_group + g for g in range(per_group)]
                band_blocks(p, dil, [(r, n) for r in classes for n in range(n_blk)])
                return carry
        n_steps = dil if n_blk >= group else dil // (group // n_blk)
        if n_steps == 1:
            residue(0, 0)
        else:
            lax.fori_loop(0, n_steps, residue, 0)


def _attention(qkv, slopes, go):
    b, s, _ = qkv.shape
    assert BLK & (BLK - 1) == 0 and s % (RESIDUES * BLK) == 0
    proj = qkv
    blk = lambda off: pl.BlockSpec((1, s, LANES), lambda j, i: (i, 0, off + j))
    return pl.pallas_call(
        _attn_kernel,
        grid=(_N_PAIR, b),
        in_specs=[
            pl.BlockSpec(memory_space=pltpu.SMEM),
            blk(_QB), blk(_KB), blk(_VB),
            pl.BlockSpec((1, LANES), lambda j, i: (0, j)),
        ],
        out_specs=pl.BlockSpec((1, s, LANES), lambda j, i: (i, 0, j)),
        out_shape=jax.ShapeDtypeStruct((b, s, ATTN_WIDTH), F32),
        scratch_shapes=[
            pltpu.VMEM((len(DILATED_PATTERNS) - 1, s, LANES), F32),
            pltpu.VMEM((len(DILATED_PATTERNS) - 1, s, LANES), F32),
            pltpu.VMEM((len(DILATED_PATTERNS) - 1, s, LANES), F32),
            pltpu.VMEM((len(DILATED_PATTERNS), 2 * BLK, 2 * BLK), F32),
        ],
        compiler_params=pltpu.CompilerParams(
            dimension_semantics=("arbitrary", "arbitrary"), vmem_limit_bytes=VMEM_LIMIT),
        name="dilated_attn",
    )(slopes, proj, proj, proj, go)


def _wkv_kernel(pr_ref, pk_ref, pv_ref, pwa_ref, pg_ref,
                mur_ref, muk_ref, muv_ref, muwa_ref, mug_ref,
                w0_ref, a0_ref, kk_ref, ka_ref, rk_ref, lng_ref, lnb_ref,
                wwa_ref, g2_ref, o_ref, h_ref, *slots):
    seq = pr_ref.shape[1]
    n_chunks = seq // BLK
    head0 = _head0_lanes()
    row = lax.broadcasted_iota(jnp.int32, (BLK, BLK), 0)
    col = lax.broadcasted_iota(jnp.int32, (BLK, BLK), 1)
    strict = col < row
    incl = col <= row
    tril_incl = jnp.where(incl, 1.0, 0.0).astype(BF16)
    first_row = lax.broadcasted_iota(jnp.int32, (BLK, LANES), 0) == 0
    srow = lax.broadcasted_iota(jnp.int32, (2 * BLK, LANES), 0) < BLK
    slane = lax.broadcasted_iota(jnp.int32, (2 * BLK, LANES), 1) < HEAD_DIM
    own_head = srow == slane
    zeros_blk = jnp.zeros((BLK, BLK), F32)
    zeros_bf = jnp.zeros((2 * BLK, 2 * BLK), BF16)
    eye_bf = jnp.where(lax.broadcasted_iota(jnp.int32, (2 * BLK, 2 * BLK), 0)
                       == lax.broadcasted_iota(jnp.int32, (2 * BLK, 2 * BLK), 1),
                       1.0, 0.0).astype(BF16)

    h_ref[...] = jnp.zeros(h_ref.shape, F32)

    each = lambda f, *lists: [f(*args) for args in zip(*lists)]
    bf = lambda v: v.astype(BF16)
    pairs = range(_N_PAIR)
    lanes = [slice(j * LANES, (j + 1) * LANES) for j in pairs]
    param = lambda ref: [ref[:, sl] for sl in lanes]

    def produce(c, slot):
        row0 = _aligned(c * BLK, BLK)
        rows = pl.ds(row0, BLK)
        prev8 = pl.ds(_aligned(max(row0 - 8, 0) if isinstance(row0, int)
                               else jnp.maximum(row0 - 8, 0), 8), 8)
        has_prev = jnp.where(c > 0, 1.0, 0.0).astype(F32)

        def shifted(ref, mu_ref, sl):
            p = ref[0, rows, sl]
            last = ref[0, prev8, sl][7:8, :] * has_prev
            prev = jnp.where(first_row, last, pltpu.roll(p, 1, 0))
            return p + (prev - p) * mu_ref[:, sl]

        xr = [shifted(pr_ref, mur_ref, sl) for sl in lanes]
        xk = [shifted(pk_ref, muk_ref, sl) for sl in lanes]
        xv = [shifted(pv_ref, muv_ref, sl) for sl in lanes]
        xwa = shifted(pwa_ref, muwa_ref, slice(None))
        xg = shifted(pg_ref, mug_ref, slice(None))
        yield

        lora_all = _dot(bf(jnp.where(head0, jnp.tanh(xwa), xwa)), wwa_ref[...])
        lora = [lora_all[:, 2 * LANES * j:2 * LANES * (j + 1)] for j in pairs]
        gate_all = _dot(bf(jax.nn.sigmoid(xg)), g2_ref[...])
        gate = [gate_all[:, sl] for sl in lanes]
        yield

        def decay_log(lo, w0):
            zw = -(w0 + lo[:, :LANES])
            softplus = jnp.maximum(zw, 0.0) + jnp.log(1.0 + jnp.exp(-jnp.abs(zw)))
            return -jnp.exp(-softplus - 0.5)
        log_decay = each(decay_log, lora, param(w0_ref))
        a_sig = each(lambda lo, a0: jax.nn.sigmoid(a0 + lo[:, LANES:]), lora, param(a0_ref))

        def unit_key(k, k_k):
            kkv = k * k_k
            return kkv / jnp.maximum(jnp.sqrt(_head_sum(kkv * kkv, head0)), 1e-12)
        kk = each(unit_key, xk, param(kk_ref))
        k2 = each(lambda k, a, k_a: k * (1.0 + (a - 1.0) * k_a), xk, a_sig, param(ka_ref))
        b_vec = each(lambda u, a: u * a, kk, a_sig)
        yield

        def cumulative(ld):
            ld_hi = bf(ld)
            ld_lo = bf(ld - ld_hi.astype(F32))
            cum = _dot(tril_incl, jnp.concatenate([ld_hi, ld_lo], axis=1))
            return cum[:, :LANES] + cum[:, LANES:]
        lw = each(cumulative, log_decay)
        yield
        lw_last = [v[BLK - 1:BLK, :] for v in lw]
        w_inv = each(lambda v: jnp.exp(-v), lw)
        w_tail = each(lambda v, last: jnp.exp(last - v), lw, lw_last)
        w_all = each(jnp.exp, lw_last)

        at = each(lambda u, v, ld: -u * jnp.exp(v - ld), kk, lw, log_decay)
        rt = each(lambda r, v: r * jnp.exp(v), xr, lw)
        bt = each(lambda b, w: b * w, b_vec, w_inv)
        kt = each(lambda k, w: k * w, k2, w_inv)
        bh = each(lambda b, w: b * w, b_vec, w_tail)
        kh = each(lambda k, w: k * w, k2, w_tail)
        yield

        def pair_products(a, r, b, k):
            lhs = jnp.concatenate([jnp.where(head0, a, 0.0), jnp.where(head0, r, 0.0),
                                   jnp.where(head0, 0.0, a), jnp.where(head0, 0.0, r)], axis=0)
            return _nt_dot(bf(lhs), bf(jnp.concatenate([b, k], axis=0)))
        pp = each(pair_products, at, rt, bt, kt)
        yield

        def n_matrix(q):
            aab0 = jnp.where(strict, q[0:BLK, :BLK], 0.0)
            aab1 = jnp.where(strict, q[2 * BLK:3 * BLK, :BLK], 0.0)
            return bf(jnp.concatenate([jnp.concatenate([aab0, zeros_blk], axis=1),
                                       jnp.concatenate([zeros_blk, aab1], axis=1)], axis=0))

        def aak_matrix(q):
            return bf(jnp.concatenate([jnp.where(strict, q[0:BLK, BLK:], 0.0),
                                       jnp.where(strict, q[2 * BLK:3 * BLK, BLK:], 0.0)], axis=0))
        akv = _pairwise_dots(each(aak_matrix, pp), each(bf, xv))

        def x_init(a, av):
            return jnp.concatenate([_stack_heads(a, head0), jnp.where(own_head, av, 0.0)], axis=1)

        def read_matrices(q):
            arb = jnp.concatenate([jnp.where(incl, q[BLK:2 * BLK, :BLK], 0.0),
                                   jnp.where(incl, q[3 * BLK:, :BLK], 0.0)], axis=1)
            ark = jnp.concatenate([jnp.where(incl, q[BLK:2 * BLK, BLK:], 0.0),
                                   jnp.where(incl, q[3 * BLK:, BLK:], 0.0)], axis=1)
            return bf(arb), bf(ark)

        for j in pairs:
            slot["n"][j] = n_matrix(pp[j])
            slot["x"][j] = bf(x_init(at[j], akv[j]))
            slot["arb"][j], slot["ark"][j] = read_matrices(pp[j])
        yield
        r_k = param(rk_ref)
        for j in pairs:
            slot["rt"][j] = rt[j]
            slot["bhs"][j] = bf(_stack_heads(bh[j], head0))
            slot["khs"][j] = bf(_stack_heads(kh[j], head0))
            slot["vs"][j] = bf(_stack_heads(xv[j], head0))
            slot["wall"][j] = jnp.broadcast_to(w_all[j], (8, LANES))
            slot["bonus"][j] = _head_sum(xr[j] * k2[j] * r_k[j], head0) * xv[j]
            slot["gate"][j] = gate[j]

    def consume(c, slot):
        n_chunk = _N_PAIR
        n_bf = [slot["n"][j] for j in range(n_chunk)]
        x_bf = [slot["x"][j] for j in range(n_chunk)]

        def live_rows(m, skip):
            if not skip:
                return m
            return jnp.concatenate([m[skip:BLK], m[BLK + skip:]], axis=0)

        def merge_rows(old, new, skip):
            if not skip:
                return new
            live = BLK - skip
            return jnp.concatenate([old[:skip], new[:live], old[BLK:BLK + skip], new[live:]],
                                   axis=0)

        span = 1
        while span < BLK:
            skip = span if span >= 16 else 0
            x_bf = each(lambda xx, nn: merge_rows(
                xx, bf(_dot(live_rows(nn + eye_bf, skip), xx)), skip), x_bf, n_bf)
            span *= 2
            if span < BLK:
                skip = span if span >= 16 else 0
                n_bf = each(lambda nn: merge_rows(
                    zeros_bf, bf(_dot(live_rows(nn, skip), nn)), skip), n_bf)
            yield
        v_s = [slot["vs"][j] for j in range(n_chunk)]

        def read_side(j, xb, vs):
            z = _dot(slot["arb"][j], xb)
            return slot["rt"][j] + z[:, :LANES], z[:, LANES:] + _dot(slot["ark"][j], vs)
        r_eff_y0 = each(read_side, range(n_chunk), x_bf, v_s)
        yield

        def state_side(j, xb, vs):
            d1 = _tn_dot(slot["bhs"][j], xb)
            d2 = _tn_dot(slot["khs"][j], vs)
            w_all = slot["wall"][j][0:1, :]
            return jnp.where(row == col, w_all, 0.0) + d1[:, :LANES], d1[:, LANES:] + d2
        m_eff_g_eff = each(state_side, range(n_chunk), x_bf, v_s)
        rm_lhs = each(lambda ry, mg: bf(jnp.concatenate([ry[0], mg[0]], axis=0)),
                      r_eff_y0, m_eff_g_eff)
        yield

        rm = _pairwise_dots(rm_lhs, [bf(h_ref[j]) for j in pairs])
        y = each(lambda v, ry: v[:BLK] + ry[1], rm, r_eff_y0)
        for j in pairs:
            h_ref[j] = rm[j][BLK:] + m_eff_g_eff[j][1]
        yield
        mean = each(lambda v: _head_sum(v, head0) * (1.0 / HEAD_DIM), y)
        yc = each(lambda v, m: v - m, y, mean)
        var = each(lambda v: _head_sum(v * v, head0) * (1.0 / HEAD_DIM), yc)
        rows = pl.ds(_aligned(c * BLK, BLK), BLK)
        lnx_g, lnx_b = param(lng_ref), param(lnb_ref)
        for j in pairs:
            yn = yc[j] * lax.rsqrt(var[j] + LNX_EPS) * lnx_g[j] + lnx_b[j]
            o_ref[0, rows, lanes[j]] = (yn + slot["bonus"][j]) * slot["gate"][j]

    def alternate(*stages):
        live = list(stages)
        while live:
            for gen in list(live):
                try:
                    next(gen)
                except StopIteration:
                    live.remove(gen)

    names = ("n", "x", "arb", "ark", "rt", "bhs", "khs", "vs", "wall", "bonus", "gate")
    slot_a = dict(zip(names, slots[:len(names)]))
    slot_b = dict(zip(names, slots[len(names):]))
    assert n_chunks % 2 == 0

    alternate(produce(0, slot_a))

    def chunk_pair(i, carry):
        c = 2 * i
        alternate(consume(c, slot_a), produce(c + 1, slot_b))
        alternate(consume(c + 1, slot_b), produce(jnp.minimum(c + 2, n_chunks - 1), slot_a))
        return carry

    lax.fori_loop(0, n_chunks // 2, chunk_pair, 0)


def _wkv_slot_shapes():
    g = _N_PAIR
    return [
        pltpu.VMEM((g, 2 * BLK, 2 * BLK), BF16),
        pltpu.VMEM((g, 2 * BLK, 2 * LANES), BF16),
        pltpu.VMEM((g, BLK, 2 * BLK), BF16),
        pltpu.VMEM((g, BLK, 2 * BLK), BF16),
        pltpu.VMEM((g, BLK, LANES), F32),
        pltpu.VMEM((g, 2 * BLK, LANES), BF16),
        pltpu.VMEM((g, 2 * BLK, LANES), BF16),
        pltpu.VMEM((g, 2 * BLK, LANES), BF16),
        pltpu.VMEM((g, 8, LANES), F32),
        pltpu.VMEM((g, BLK, LANES), F32),
        pltpu.VMEM((g, BLK, LANES), F32),
    ]


def _rwkv(proj, mu, w0, a0, k_k, k_a, r_k, lnx_g, lnx_b, wwa, g2_bf16):
    b, s, _ = proj.shape
    np_ = _N_PAIR
    w = RWKV_WIDTH
    wide = lambda off: pl.BlockSpec((1, s, w), lambda i: (i, 0, off))
    lane = lambda off: pl.BlockSpec((1, s, LANES), lambda i: (i, 0, off))
    vec_w = lambda off: pl.BlockSpec((1, w), lambda i: (0, off))
    vec_l = lambda off: pl.BlockSpec((1, LANES), lambda i: (0, off))
    full = lambda shape: pl.BlockSpec(shape, lambda i: (0,) * len(shape))
    return pl.pallas_call(
        _wkv_kernel,
        grid=(b,),
        in_specs=[
            wide(0), wide(1), wide(2),
            lane(3 * np_), lane(3 * np_ + 1),
            vec_w(0), vec_w(1), vec_w(2), vec_l(3 * np_), vec_l(3 * np_ + 1),
            vec_w(0), vec_w(0), vec_w(0), vec_w(0), vec_w(0), vec_w(0), vec_w(0),
            full((LANES, np_ * 2 * LANES)),
            full((GATE_LORA, w)),
        ],
        out_specs=pl.BlockSpec((1, s, w), lambda i: (i, 0, 0)),
        out_shape=jax.ShapeDtypeStruct((b, s, RWKV_WIDTH), F32),
        scratch_shapes=[pltpu.VMEM((np_, LANES, LANES), F32)] + 2 * _wkv_slot_shapes(),
        compiler_params=pltpu.CompilerParams(
            dimension_semantics=("parallel",), vmem_limit_bytes=VMEM_LIMIT),
        name="rwkv7",
    )(proj, proj, proj, proj, proj, mu, mu, mu, mu, mu,
      w0, a0, k_k, k_a, r_k, lnx_g, lnx_b, wwa, g2_bf16)


def _out_ffn_kernel(x_ref, attn_ref, rwkv_ref, woa_ref, wor_ref, g2_ref,
                    wg_ref, wu_ref, wd_ref, o_ref):
    x1 = (x_ref[...] + _dot(attn_ref[...].astype(BF16), woa_ref[...])
          + _dot(rwkv_ref[...].astype(BF16), wor_ref[...]))
    ms = jnp.mean(x1 * x1, axis=-1, keepdims=True)
    xn = (x1 * lax.rsqrt(ms + NORM_EPS) * g2_ref[...]).astype(BF16)
    gate = _dot(xn, wg_ref[...])
    up = _dot(xn, wu_ref[...])
    hidden = (gate * jax.nn.sigmoid(gate) * up).astype(BF16)
    o_ref[...] = x1 + _dot(hidden, wd_ref[...])


def _out_ffn(x2d, attn2d, rwkv2d, woa, wor, g2, wg, wu, wd, tm):
    t, d = x2d.shape
    f = wg.shape[1]
    tok = lambda w: pl.BlockSpec((tm, w), lambda i: (i, 0))
    res = lambda shape: pl.BlockSpec(shape, lambda i: (0, 0), pipeline_mode=pl.Buffered(1))
    return pl.pallas_call(
        _out_ffn_kernel,
        grid=(t // tm,),
        in_specs=[
            tok(d), tok(ATTN_WIDTH), tok(RWKV_WIDTH),
            res((ATTN_WIDTH, d)), res((RWKV_WIDTH, d)),
            pl.BlockSpec((1, d), lambda i: (0, 0)),
            res((d, f)), res((d, f)), res((f, d)),
        ],
        out_specs=tok(d),
        out_shape=jax.ShapeDtypeStruct((t, d), F32),
        compiler_params=pltpu.CompilerParams(
            dimension_semantics=("parallel",), vmem_limit_bytes=VMEM_LIMIT),
        name="out_ffn",
    )(x2d, attn2d, rwkv2d, woa, wor, g2, wg, wu, wd)


def _layer(x, norm1_g, w_in, q_norm_g, k_norm_g, attn_out_g, rwkv_mu, w0, w2, a0, a2, g2,
           k_k, k_a, r_k, lnx_g, lnx_b, w_out, norm2_g, w_gate, w_up, w_down):
    b, s, d = x.shape
    assert s % (BLK * max(dil for _, dil in DILATED_PATTERNS)) == 0
    assert all(window // dil == BLK for window, dil in DILATED_PATTERNS)
    t = b * s
    assert t % IN_PROJ_ROWS == 0 and t % FFN_ROWS == 0
    row = lambda v: v.reshape(1, -1).astype(F32)

    x2d = x.reshape(t, d)
    per_head = lambda v: jnp.tile(v.reshape(1, HEAD_DIM), (1, ATTN_HEADS)).astype(F32)
    qk_gain = jnp.concatenate([per_head(q_norm_g) * (HEAD_DIM ** -0.5 * LOG2_E),
                               per_head(k_norm_g)], axis=1)
    proj, qkv = _in_proj(x, row(norm1_g), w_in.astype(BF16), qk_gain, IN_PROJ_ROWS)
    proj = proj.reshape(b, s, -1)

    slopes = jnp.exp2(-8.0 * jnp.arange(1, ATTN_HEADS + 1, dtype=F32) / ATTN_HEADS)
    attn = _attention(qkv.reshape(b, s, -1), slopes, row(attn_out_g))

    w2p = w2.reshape(DECAY_LORA, _N_PAIR, LANES).transpose(1, 0, 2)
    a2p = a2.reshape(ICLR_LORA, _N_PAIR, LANES).transpose(1, 0, 2)
    zero = jnp.zeros_like(w2p)
    wwa = jnp.concatenate([jnp.concatenate([w2p, zero], axis=2),
                           jnp.concatenate([jnp.zeros_like(a2p), a2p], axis=2)], axis=1)
    wwa = wwa.transpose(1, 0, 2).reshape(LANES, _N_PAIR * 2 * LANES).astype(BF16)
    rwkv = _rwkv(proj, row(rwkv_mu), row(w0), row(a0), row(k_k), row(k_a), row(r_k),
                 row(lnx_g), row(lnx_b), wwa, g2.astype(BF16))

    out = _out_ffn(x2d, attn.reshape(t, ATTN_WIDTH), rwkv.reshape(t, RWKV_WIDTH),
                   w_out[:ATTN_WIDTH].astype(BF16), w_out[ATTN_WIDTH:].astype(BF16),
                   row(norm2_g), w_gate.astype(BF16), w_up.astype(BF16), w_down.astype(BF16),
                   FFN_ROWS)
    return out.reshape(b, s, d)


def kernel(x, norm1_g, w_in, q_norm_g, k_norm_g, attn_out_g, rwkv_mu, w0, w2, a0, a2, g2,
           k_k, k_a, r_k, lnx_g, lnx_b, w_out, norm2_g, w_gate, w_up, w_down):
    h = x
    for layer in range(norm1_g.shape[0]):
        h = _layer(h, norm1_g[layer], w_in[layer], q_norm_g[layer], k_norm_g[layer],
                   attn_out_g[layer], rwkv_mu[layer], w0[layer], w2[layer], a0[layer],
                   a2[layer], g2[layer], k_k[layer], k_a[layer], r_k[layer], lnx_g[layer],
                   lnx_b[layer], w_out[layer], norm2_g[layer], w_gate[layer], w_up[layer],
                   w_down[layer])
    return h
```

```python
import functools

import jax
import jax.numpy as jnp
from jax import lax
from jax.experimental import pallas as pl
from jax.experimental.pallas import tpu as pltpu

F32 = jnp.float32
BF16 = jnp.bfloat16

HEAD_DIM = 64
LANES = 128
ATTN_HEADS = 8
RWKV_HEADS = 8
ATTN_WIDTH = ATTN_HEADS * HEAD_DIM
RWKV_WIDTH = RWKV_HEADS * HEAD_DIM
DECAY_LORA = 64
ICLR_LORA = 64
GATE_LORA = 128
DILATED_PATTERNS = ((128, 1), (512, 4), (2048, 16))
RESIDUES = max(dil for _, dil in DILATED_PATTERNS)
BLK = 128
ATTN_GROUP = 16
IN_PROJ_ROWS = 1024
FFN_ROWS = 512
NORM_EPS = 1e-6
LNX_EPS = 64e-5
MASKED = -1e30
LOG2_E = 1.4426950408889634
VMEM_LIMIT = 56 * 1024 * 1024

_QB, _KB, _VB = 0, ATTN_WIDTH // LANES, 2 * ATTN_WIDTH // LANES
_N_PAIR = RWKV_WIDTH // LANES


def _nt_dot(a, b):
    return lax.dot_general(a, b, (((1,), (1,)), ((), ())), preferred_element_type=F32)


def _tn_dot(a, b):
    return lax.dot_general(a, b, (((0,), (0,)), ((), ())), preferred_element_type=F32)


def _dot(a, b):
    return jnp.dot(a, b, preferred_element_type=F32)


def _pairwise_dots(lhs, rhs):
    assert len(lhs) == len(rhs) and len(lhs) % 2 == 0
    out = []
    for j in range(0, len(lhs), 2):
        zero = jnp.zeros_like(rhs[j])
        both = _dot(jnp.concatenate([lhs[j], lhs[j + 1]], axis=1),
                    jnp.concatenate([jnp.concatenate([rhs[j], zero], axis=1),
                                     jnp.concatenate([zero, rhs[j + 1]], axis=1)], axis=0))
        out += [both[:, :LANES], both[:, LANES:]]
    return out


def _aligned(index, multiple):
    if isinstance(index, int):
        return index
    return pl.multiple_of(index, multiple)


def _head0_lanes():
    return lax.broadcasted_iota(jnp.int32, (1, LANES), 1) < HEAD_DIM


def _head_averager():
    i = lax.broadcasted_iota(jnp.int32, (LANES, LANES), 0)
    j = lax.broadcasted_iota(jnp.int32, (LANES, LANES), 1)
    return jnp.where((i < HEAD_DIM) == (j < HEAD_DIM), 1.0 / HEAD_DIM, 0.0).astype(BF16)


def _head_mean_mxu(x, averager):
    return _dot(x.astype(BF16), averager)


def _head_sum(x, head0):
    s0 = jnp.sum(jnp.where(head0, x, 0.0), axis=-1, keepdims=True)
    s1 = jnp.sum(jnp.where(head0, 0.0, x), axis=-1, keepdims=True)
    return jnp.where(head0, s0, s1)


def _stack_heads(x, head0):
    return jnp.concatenate([jnp.where(head0, x, 0.0), jnp.where(head0, 0.0, x)], axis=0)


def _window_row_token(u):
    per_residue = BLK // RESIDUES
    return RESIDUES * (u & (per_residue - 1)) + (u >> (per_residue.bit_length() - 1))


def _in_proj_kernel(x_ref, g_ref, w_ref, qk_gain_ref, rw_ref, qkv_ref):
    x = x_ref[...]
    ms = jnp.mean(x * x, axis=-1, keepdims=True)
    xn = (x * lax.rsqrt(ms + NORM_EPS) * g_ref[...]).astype(BF16)
    n_qk = qk_gain_ref.shape[1]
    n_attn = qkv_ref.shape[1]
    rw_ref[...] = _dot(xn, w_ref[:, n_attn:])
    u = lax.broadcasted_iota(jnp.int32, (BLK, BLK), 0)
    t = lax.broadcasted_iota(jnp.int32, (BLK, BLK), 1)
    regroup = jnp.where(t == _window_row_token(u), 1.0, 0.0).astype(BF16)
    xg = jnp.concatenate([_dot(regroup, xn[w * BLK:(w + 1) * BLK]).astype(BF16)
                          for w in range(x.shape[0] // BLK)], axis=0)
    qk = _dot(xg, w_ref[:, :n_qk])
    qkv_ref[:, n_qk:] = _dot(xg, w_ref[:, n_qk:n_attn])
    head0 = _head0_lanes()
    for j in range(n_qk // LANES):
        cols = slice(j * LANES, (j + 1) * LANES)
        v = qk[:, cols]
        mean_sq = _head_sum(v * v, head0) * (1.0 / HEAD_DIM)
        qkv_ref[:, cols] = v * lax.rsqrt(mean_sq + NORM_EPS) * qk_gain_ref[:, cols]


def _in_proj(x3d, g, w_bf16, qk_gain, tm):
    b, s, d = x3d.shape
    n = w_bf16.shape[1]
    n_attn = 3 * ATTN_WIDTH
    assert s % tm == 0 and tm % BLK == 0
    return pl.pallas_call(
        _in_proj_kernel,
        grid=(b * s // tm,),
        in_specs=[
            pl.BlockSpec((tm, d), lambda i: (i, 0)),
            pl.BlockSpec((1, d), lambda i: (0, 0)),
            pl.BlockSpec((d, n), lambda i: (0, 0), pipeline_mode=pl.Buffered(1)),
            pl.BlockSpec((1, qk_gain.shape[1]), lambda i: (0, 0)),
        ],
        out_specs=[
            pl.BlockSpec((tm, n - n_attn), lambda i: (i, 0)),
            pl.BlockSpec((tm, n_attn), lambda i: (i, 0)),
        ],
        out_shape=[
            jax.ShapeDtypeStruct((b * s, n - n_attn), F32),
            jax.ShapeDtypeStruct((b * s, n_attn), F32),
        ],
        compiler_params=pltpu.CompilerParams(
            dimension_semantics=("parallel",), vmem_limit_bytes=VMEM_LIMIT),
        name="in_proj",
    )(x3d.reshape(b * s, d), g, w_bf16, qk_gain)


def _attn_kernel(slope_ref, q_ref, k_ref, v_ref, go_ref, o_ref,
                 m_ref, l_ref, acc_ref, bias_ref):
    seq = q_ref.shape[1]
    hp = pl.program_id(0)
    head0 = _head0_lanes()
    averager = _head_averager()
    per_residue = BLK // RESIDUES

    def block_runs(dil, cls, n):
        if dil == 1:
            return [(_aligned(n * BLK, BLK), BLK)]
        return [(_aligned((n * dil + wl) * BLK + (dil * m + cls) * per_residue, per_residue),
                 per_residue) for wl in range(dil) for m in range(RESIDUES // dil)]

    def member_of(u, dil):
        per_window = BLK // dil
        in_window = u & (per_window - 1)
        return (per_window * (u >> (per_window.bit_length() - 1))
                + (RESIDUES // dil) * (u & (per_residue - 1))
                + (in_window >> (per_residue.bit_length() - 1)))

    def load_rows(ref, runs):
        return jnp.concatenate([ref[0, pl.ds(start, length), :] for start, length in runs], axis=0)

    def store_rows(ref, p, runs, value):
        offset = 0
        for start, length in runs:
            ref[p, pl.ds(start, length), :] = value[offset:offset + length]
            offset += length

    @pl.when(pl.program_id(1) == 0)
    def _():
        qi = lax.broadcasted_iota(jnp.int32, (2 * BLK, 2 * BLK), 0)
        ci = lax.broadcasted_iota(jnp.int32, (2 * BLK, 2 * BLK), 1)
        slope = jnp.where(qi < BLK, slope_ref[2 * hp], slope_ref[2 * hp + 1]) * LOG2_E
        for p, (window, dil) in enumerate(DILATED_PATTERNS):
            steps = window // dil
            back = ((BLK + member_of(qi & (BLK - 1), dil))
                    - ((ci & BLK) + member_of(ci & (BLK - 1), dil)))
            valid = (back >= 0) & (back <= steps)
            bias_ref[p] = jnp.where(valid, -slope * (back.astype(F32) * dil), MASKED)

    ones_keys = jnp.ones((2 * BLK, LANES), BF16)

    def band_blocks(p, dil, blocks):
        each = lambda f, *lists: [f(*args) for args in zip(*lists)]
        bf = lambda v: v.astype(BF16)
        runs = [block_runs(dil, cls, n) for cls, n in blocks]
        has_prevs = [n > 0 for _, n in blocks]

        def keys_values(block, own, has_prev):
            kb = load_rows(k_ref, own)
            vb = load_rows(v_ref, own)
            if has_prev:
                prev = block_runs(dil, block[0], block[1] - 1)
                kb = jnp.concatenate([load_rows(k_ref, prev), kb], axis=0)
                vb = jnp.concatenate([load_rows(v_ref, prev), vb], axis=0)
            return bf(kb), jnp.concatenate([bf(vb), ones_keys[:vb.shape[0]]], axis=1)
        kv = each(keys_values, blocks, runs, has_prevs)
        qs = each(lambda own: bf(_stack_heads(load_rows(q_ref, own), head0)), runs)
        s = each(lambda q, kvb, has_prev:
                 _nt_dot(q, kvb[0]) + (bias_ref[p] if has_prev else bias_ref[p, :, BLK:]),
                 qs, kv, has_prevs)
        m_blk = each(lambda v: jnp.max(v, axis=-1, keepdims=True), s)
        pr = each(lambda v, m: bf(jnp.exp2(v - m)), s, m_blk)
        pv = each(lambda v, kvb: _dot(v, kvb[1]), pr, kv)

        m_pair = each(lambda m_b: jnp.where(head0, m_b[:BLK], m_b[BLK:]), m_blk)
        acc_pair = each(lambda v: jnp.where(head0, v[:BLK, :LANES], v[BLK:, :LANES]), pv)
        l_pair = each(lambda v: jnp.where(head0, v[:BLK, LANES:], v[BLK:, LANES:]), pv)
        if p + 1 < len(DILATED_PATTERNS):
            def keep(own, m_b, acc_b, l_b):
                store_rows(m_ref, p, own, m_b)
                store_rows(acc_ref, p, own, acc_b)
                store_rows(l_ref, p, own, l_b)
            each(keep, runs, m_pair, acc_pair, l_pair)
            return

        assert dil == RESIDUES and seq == BLK * RESIDUES
        part = lambda ref, q, own: jnp.concatenate(
            [ref[q, pl.ds(start, length), :] for start, length in own], axis=0)
        m_prev = [[part(m_ref, q, own) for q in range(p)] for own in runs]
        m_all = each(lambda mine, others: functools.reduce(jnp.maximum, others, mine),
                     m_pair, m_prev)
        w_mine = each(lambda mine, top: jnp.exp2(mine - top), m_pair, m_all)
        w_prev = each(lambda others, top: [jnp.exp2(m - top) for m in others], m_prev, m_all)
        num = each(lambda own, wm, a, ws:
                   wm * a + sum(w * part(acc_ref, q, own) for q, w in enumerate(ws)),
                   runs, w_mine, acc_pair, w_prev)
        den = each(lambda own, wm, l, ws:
                   wm * l + sum(w * part(l_ref, q, own) for q, w in enumerate(ws)),
                   runs, w_mine, l_pair, w_prev)
        o = each(lambda a, b: a / b, num, den)
        ms = each(lambda v: _head_mean_mxu(v * v, averager), o)
        for (cls, _), v, mean_sq in zip(blocks, o, ms):
            o_ref[0, pl.ds(cls, BLK, stride=RESIDUES), :] = (
                v * lax.rsqrt(mean_sq + NORM_EPS) * go_ref[...])

    for p, (window, dil) in enumerate(DILATED_PATTERNS):
        n_blk = seq // (BLK * dil)
        group = ATTN_GROUP
        if n_blk >= group:
            assert n_blk % group == 0
            groups = n_blk // group

            def residue(r, carry, p=p, dil=dil, groups=groups, group=group):
                for grp in range(groups):
                    band_blocks(p, dil, [(r, n) for n in range(grp * group, (grp + 1) * group)])
                return carry
        else:
            assert group % n_blk == 0 and dil % (group // n_blk) == 0
            per_group = group // n_blk

            def residue(i, carry, p=p, dil=dil, n_blk=n_blk, per_group=per_group):
                classes = [i * per_group + g for g in range(per_group)]
                band_blocks(p, dil, [(r, n) for r in classes for n in range(n_blk)])
                return carry
        n_steps = dil if n_blk >= group else dil // (group // n_blk)
        if n_steps == 1:
            residue(0, 0)
        else:
            lax.fori_loop(0, n_steps, residue, 0)


def _attention(qkv, slopes, go):
    b, s, _ = qkv.shape
    assert BLK & (BLK - 1) == 0 and s % (RESIDUES * BLK) == 0
    blk = lambda off: pl.BlockSpec((1, s, LANES), lambda j, i: (i, 0, off + j))
    return pl.pallas_call(
        _attn_kernel,
        grid=(_N_PAIR, b),
        in_specs=[
            pl.BlockSpec(memory_space=pltpu.SMEM),
            blk(_QB), blk(_KB), blk(_VB),
            pl.BlockSpec((1, LANES), lambda j, i: (0, j)),
        ],
        out_specs=pl.BlockSpec((1, s, LANES), lambda j, i: (i, 0, j)),
        out_shape=jax.ShapeDtypeStruct((b, s, ATTN_WIDTH), F32),
        scratch_shapes=[
            pltpu.VMEM((len(DILATED_PATTERNS) - 1, s, LANES), F32),
            pltpu.VMEM((len(DILATED_PATTERNS) - 1, s, LANES), F32),
            pltpu.VMEM((len(DILATED_PATTERNS) - 1, s, LANES), F32),
            pltpu.VMEM((len(DILATED_PATTERNS), 2 * BLK, 2 * BLK), F32),
        ],
        compiler_params=pltpu.CompilerParams(
            dimension_semantics=("arbitrary", "arbitrary"), vmem_limit_bytes=VMEM_LIMIT),
        name="dilated_attn",
    )(slopes, qkv, qkv, qkv, go)


def _wkv_kernel(pr_ref, pk_ref, pv_ref, pwa_ref, pg_ref,
                mur_ref, muk_ref, muv_ref, muwa_ref, mug_ref,
                w0_ref, a0_ref, kk_ref, ka_ref, rk_ref, lng_ref, lnb_ref,
                wwa_ref, g2_ref, o_ref, h_ref, *slots):
    seq = pr_ref.shape[1]
    n_chunks = seq // BLK
    head0 = _head0_lanes()
    row = lax.broadcasted_iota(jnp.int32, (BLK, BLK), 0)
    col = lax.broadcasted_iota(jnp.int32, (BLK, BLK), 1)
    strict = col < row
    incl = col <= row
    tril_incl = jnp.where(incl, 1.0, 0.0).astype(BF16)
    first_row = lax.broadcasted_iota(jnp.int32, (BLK, LANES), 0) == 0
    srow = lax.broadcasted_iota(jnp.int32, (2 * BLK, LANES), 0) < BLK
    slane = lax.broadcasted_iota(jnp.int32, (2 * BLK, LANES), 1) < HEAD_DIM
    own_head = srow == slane
    zeros_blk = jnp.zeros((BLK, BLK), F32)
    zeros_bf = jnp.zeros((2 * BLK, 2 * BLK), BF16)
    eye_bf = jnp.where(lax.broadcasted_iota(jnp.int32, (2 * BLK, 2 * BLK), 0)
                       == lax.broadcasted_iota(jnp.int32, (2 * BLK, 2 * BLK), 1),
                       1.0, 0.0).astype(BF16)

    h_ref[...] = jnp.zeros(h_ref.shape, F32)

    each = lambda f, *lists: [f(*args) for args in zip(*lists)]
    bf = lambda v: v.astype(BF16)
    pairs = range(_N_PAIR)
    lanes = [slice(j * LANES, (j + 1) * LANES) for j in pairs]
    param = lambda ref: [ref[:, sl] for sl in lanes]

    def produce(c, slot):
        row0 = _aligned(c * BLK, BLK)
        rows = pl.ds(row0, BLK)
        prev8 = pl.ds(_aligned(max(row0 - 8, 0) if isinstance(row0, int)
                               else jnp.maximum(row0 - 8, 0), 8), 8)
        has_prev = jnp.where(c > 0, 1.0, 0.0).astype(F32)

        def shifted(ref, mu_ref, sl):
            p = ref[0, rows, sl]
            last = ref[0, prev8, sl][7:8, :] * has_prev
            prev = jnp.where(first_row, last, pltpu.roll(p, 1, 0))
            return p + (prev - p) * mu_ref[:, sl]

        xr = [shifted(pr_ref, mur_ref, sl) for sl in lanes]
        xk = [shifted(pk_ref, muk_ref, sl) for sl in lanes]
        xv = [shifted(pv_ref, muv_ref, sl) for sl in lanes]
        xwa = shifted(pwa_ref, muwa_ref, slice(None))
        xg = shifted(pg_ref, mug_ref, slice(None))
        yield

        lora_all = _dot(bf(jnp.where(head0, jnp.tanh(xwa), xwa)), wwa_ref[...])
        lora = [lora_all[:, 2 * LANES * j:2 * LANES * (j + 1)] for j in pairs]
        gate_all = _dot(bf(jax.nn.sigmoid(xg)), g2_ref[...])
        gate = [gate_all[:, sl] for sl in lanes]
        yield

        def decay_log(lo, w0):
            zw = -(w0 + lo[:, :LANES])
            softplus = jnp.maximum(zw, 0.0) + jnp.log(1.0 + jnp.exp(-jnp.abs(zw)))
            return -jnp.exp(-softplus - 0.5)
        log_decay = each(decay_log, lora, param(w0_ref))
        a_sig = each(lambda lo, a0: jax.nn.sigmoid(a0 + lo[:, LANES:]), lora, param(a0_ref))

        def unit_key(k, k_k):
            kkv = k * k_k
            return kkv / jnp.maximum(jnp.sqrt(_head_sum(kkv * kkv, head0)), 1e-12)
        kk = each(unit_key, xk, param(kk_ref))
        k2 = each(lambda k, a, k_a: k * (1.0 + (a - 1.0) * k_a), xk, a_sig, param(ka_ref))
        b_vec = each(lambda u, a: u * a, kk, a_sig)
        yield

        def cumulative(ld):
            ld_hi = bf(ld)
            ld_lo = bf(ld - ld_hi.astype(F32))
            cum = _dot(tril_incl, jnp.concatenate([ld_hi, ld_lo], axis=1))
            return cum[:, :LANES] + cum[:, LANES:]
        lw = each(cumulative, log_decay)
        yield
        lw_last = [v[BLK - 1:BLK, :] for v in lw]
        w_inv = each(lambda v: jnp.exp(-v), lw)
        w_tail = each(lambda v, last: jnp.exp(last - v), lw, lw_last)
        w_all = each(jnp.exp, lw_last)

        at = each(lambda u, v, ld: -u * jnp.exp(v - ld), kk, lw, log_decay)
        rt = each(lambda r, v: r * jnp.exp(v), xr, lw)
        bt = each(lambda b, w: b * w, b_vec, w_inv)
        kt = each(lambda k, w: k * w, k2, w_inv)
        bh = each(lambda b, w: b * w, b_vec, w_tail)
        kh = each(lambda k, w: k * w, k2, w_tail)
        yield

        def pair_products(a, r, b, k):
            lhs = jnp.concatenate([jnp.where(head0, a, 0.0), jnp.where(head0, r, 0.0),
                                   jnp.where(head0, 0.0, a), jnp.where(head0, 0.0, r)], axis=0)
            return _nt_dot(bf(lhs), bf(jnp.concatenate([b, k], axis=0)))
        pp = each(pair_products, at, rt, bt, kt)
        yield

        def n_matrix(q):
            aab0 = jnp.where(strict, q[0:BLK, :BLK], 0.0)
            aab1 = jnp.where(strict, q[2 * BLK:3 * BLK, :BLK], 0.0)
            return bf(jnp.concatenate([jnp.concatenate([aab0, zeros_blk], axis=1),
                                       jnp.concatenate([zeros_blk, aab1], axis=1)], axis=0))

        def aak_matrix(q):
            return bf(jnp.concatenate([jnp.where(strict, q[0:BLK, BLK:], 0.0),
                                       jnp.where(strict, q[2 * BLK:3 * BLK, BLK:], 0.0)], axis=0))
        akv = _pairwise_dots(each(aak_matrix, pp), each(bf, xv))

        def x_init(a, av):
            return jnp.concatenate([_stack_heads(a, head0), jnp.where(own_head, av, 0.0)], axis=1)

        def read_matrices(q):
            arb = jnp.concatenate([jnp.where(incl, q[BLK:2 * BLK, :BLK], 0.0),
                                   jnp.where(incl, q[3 * BLK:, :BLK], 0.0)], axis=1)
            ark = jnp.concatenate([jnp.where(incl, q[BLK:2 * BLK, BLK:], 0.0),
                                   jnp.where(incl, q[3 * BLK:, BLK:], 0.0)], axis=1)
            return bf(arb), bf(ark)

        for j in pairs:
            slot["n"][j] = n_matrix(pp[j])
            slot["x"][j] = bf(x_init(at[j], akv[j]))
            slot["arb"][j], slot["ark"][j] = read_matrices(pp[j])
        yield
        r_k = param(rk_ref)
        for j in pairs:
            slot["rt"][j] = rt[j]
            slot["bhs"][j] = bf(_stack_heads(bh[j], head0))
            slot["khs"][j] = bf(_stack_heads(kh[j], head0))
            slot["vs"][j] = bf(_stack_heads(xv[j], head0))
            slot["wall"][j] = jnp.broadcast_to(w_all[j], (8, LANES))
            slot["bonus"][j] = _head_sum(xr[j] * k2[j] * r_k[j], head0) * xv[j]
            slot["gate"][j] = gate[j]

    def consume(c, slot):
        n_chunk = _N_PAIR
        n_bf = [slot["n"][j] for j in range(n_chunk)]
        x_bf = [slot["x"][j] for j in range(n_chunk)]

        def live_rows(m, skip):
            if not skip:
                return m
            return jnp.concatenate([m[skip:BLK], m[BLK + skip:]], axis=0)

        def merge_rows(old, new, skip):
            if not skip:
                return new
            live = BLK - skip
            return jnp.concatenate([old[:skip], new[:live], old[BLK:BLK + skip], new[live:]],
                                   axis=0)

        span = 1
        while span < BLK:
            skip = span if span >= 16 else 0
            x_bf = each(lambda xx, nn: merge_rows(
                xx, bf(_dot(live_rows(nn + eye_bf, skip), xx)), skip), x_bf, n_bf)
            span *= 2
            if span < BLK:
                skip = span if span >= 16 else 0
                n_bf = each(lambda nn: merge_rows(
                    zeros_bf, bf(_dot(live_rows(nn, skip), nn)), skip), n_bf)
            yield
        v_s = [slot["vs"][j] for j in range(n_chunk)]

        def read_side(j, xb, vs):
            z = _dot(slot["arb"][j], xb)
            return slot["rt"][j] + z[:, :LANES], z[:, LANES:] + _dot(slot["ark"][j], vs)
        r_eff_y0 = each(read_side, range(n_chunk), x_bf, v_s)
        yield

        def state_side(j, xb, vs):
            d1 = _tn_dot(slot["bhs"][j], xb)
            d2 = _tn_dot(slot["khs"][j], vs)
            w_all = slot["wall"][j][0:1, :]
            return jnp.where(row == col, w_all, 0.0) + d1[:, :LANES], d1[:, LANES:] + d2
        m_eff_g_eff = each(state_side, range(n_chunk), x_bf, v_s)
        rm_lhs = each(lambda ry, mg: bf(jnp.concatenate([ry[0], mg[0]], axis=0)),
                      r_eff_y0, m_eff_g_eff)
        yield

        rm = _pairwise_dots(rm_lhs, [bf(h_ref[j]) for j in pairs])
        y = each(lambda v, ry: v[:BLK] + ry[1], rm, r_eff_y0)
        for j in pairs:
            h_ref[j] = rm[j][BLK:] + m_eff_g_eff[j][1]
        yield
        mean = each(lambda v: _head_sum(v, head0) * (1.0 / HEAD_DIM), y)
        yc = each(lambda v, m: v - m, y, mean)
        var = each(lambda v: _head_sum(v * v, head0) * (1.0 / HEAD_DIM), yc)
        rows = pl.ds(_aligned(c * BLK, BLK), BLK)
        lnx_g, lnx_b = param(lng_ref), param(lnb_ref)
        for j in pairs:
            yn = yc[j] * lax.rsqrt(var[j] + LNX_EPS) * lnx_g[j] + lnx_b[j]
            o_ref[0, rows, lanes[j]] = (yn + slot["bonus"][j]) * slot["gate"][j]

    def alternate(*stages):
        live = list(stages)
        while live:
            for gen in list(live):
                try:
                    next(gen)
                except StopIteration:
                    live.remove(gen)

    names = ("n", "x", "arb", "ark", "rt", "bhs", "khs", "vs", "wall", "bonus", "gate")
    slot_a = dict(zip(names, slots[:len(names)]))
    slot_b = dict(zip(names, slots[len(names):]))
    assert n_chunks % 2 == 0

    alternate(produce(0, slot_a))

    def chunk_pair(i, carry):
        c = 2 * i
        alternate(consume(c, slot_a), produce(c + 1, slot_b))
        alternate(consume(c + 1, slot_b), produce(jnp.minimum(c + 2, n_chunks - 1), slot_a))
        return carry

    lax.fori_loop(0, n_chunks // 2, chunk_pair, 0)


def _wkv_slot_shapes():
    g = _N_PAIR
    return [
        pltpu.VMEM((g, 2 * BLK, 2 * BLK), BF16),
        pltpu.VMEM((g, 2 * BLK, 2 * LANES), BF16),
        pltpu.VMEM((g, BLK, 2 * BLK), BF16),
        pltpu.VMEM((g, BLK, 2 * BLK), BF16),
        pltpu.VMEM((g, BLK, LANES), F32),
        pltpu.VMEM((g, 2 * BLK, LANES), BF16),
        pltpu.VMEM((g, 2 * BLK, LANES), BF16),
        pltpu.VMEM((g, 2 * BLK, LANES), BF16),
        pltpu.VMEM((g, 8, LANES), F32),
        pltpu.VMEM((g, BLK, LANES), F32),
        pltpu.VMEM((g, BLK, LANES), F32),
    ]


def _rwkv(proj, mu, w0, a0, k_k, k_a, r_k, lnx_g, lnx_b, wwa, g2_bf16):
    b, s, _ = proj.shape
    np_ = _N_PAIR
    w = RWKV_WIDTH
    wide = lambda off: pl.BlockSpec((1, s, w), lambda i: (i, 0, off))
    lane = lambda off: pl.BlockSpec((1, s, LANES), lambda i: (i, 0, off))
    vec_w = lambda off: pl.BlockSpec((1, w), lambda i: (0, off))
    vec_l = lambda off: pl.BlockSpec((1, LANES), lambda i: (0, off))
    full = lambda shape: pl.BlockSpec(shape, lambda i: (0,) * len(shape))
    return pl.pallas_call(
        _wkv_kernel,
        grid=(b,),
        in_specs=[
            wide(0), wide(1), wide(2),
            lane(3 * np_), lane(3 * np_ + 1),
            vec_w(0), vec_w(1), vec_w(2), vec_l(3 * np_), vec_l(3 * np_ + 1),
            vec_w(0), vec_w(0), vec_w(0), vec_w(0), vec_w(0), vec_w(0), vec_w(0),
            full((LANES, np_ * 2 * LANES)),
            full((GATE_LORA, w)),
        ],
        out_specs=pl.BlockSpec((1, s, w), lambda i: (i, 0, 0)),
        out_shape=jax.ShapeDtypeStruct((b, s, RWKV_WIDTH), F32),
        scratch_shapes=[pltpu.VMEM((np_, LANES, LANES), F32)] + 2 * _wkv_slot_shapes(),
        compiler_params=pltpu.CompilerParams(
            dimension_semantics=("parallel",), vmem_limit_bytes=VMEM_LIMIT),
        name="rwkv7",
    )(proj, proj, proj, proj, proj, mu, mu, mu, mu, mu,
      w0, a0, k_k, k_a, r_k, lnx_g, lnx_b, wwa, g2_bf16)


def _out_ffn_kernel(x_ref, attn_ref, rwkv_ref, woa_ref, wor_ref, g2_ref,
                    wg_ref, wu_ref, wd_ref, o_ref):
    x1 = (x_ref[...] + _dot(attn_ref[...].astype(BF16), woa_ref[...])
          + _dot(rwkv_ref[...].astype(BF16), wor_ref[...]))
    ms = jnp.mean(x1 * x1, axis=-1, keepdims=True)
    xn = (x1 * lax.rsqrt(ms + NORM_EPS) * g2_ref[...]).astype(BF16)
    gate = _dot(xn, wg_ref[...])
    up = _dot(xn, wu_ref[...])
    hidden = (gate * jax.nn.sigmoid(gate) * up).astype(BF16)
    o_ref[...] = x1 + _dot(hidden, wd_ref[...])


def _out_ffn(x2d, attn2d, rwkv2d, woa, wor, g2, wg, wu, wd, tm):
    t, d = x2d.shape
    f = wg.shape[1]
    tok = lambda w: pl.BlockSpec((tm, w), lambda i: (i, 0))
    res = lambda shape: pl.BlockSpec(shape, lambda i: (0, 0), pipeline_mode=pl.Buffered(1))
    return pl.pallas_call(
        _out_ffn_kernel,
        grid=(t // tm,),
        in_specs=[
            tok(d), tok(ATTN_WIDTH), tok(RWKV_WIDTH),
            res((ATTN_WIDTH, d)), res((RWKV_WIDTH, d)),
            pl.BlockSpec((1, d), lambda i: (0, 0)),
            res((d, f)), res((d, f)), res((f, d)),
        ],
        out_specs=tok(d),
        out_shape=jax.ShapeDtypeStruct((t, d), F32),
        compiler_params=pltpu.CompilerParams(
            dimension_semantics=("parallel",), vmem_limit_bytes=VMEM_LIMIT),
        name="out_ffn",
    )(x2d, attn2d, rwkv2d, woa, wor, g2, wg, wu, wd)


def _layer(x, norm1_g, w_in, q_norm_g, k_norm_g, attn_out_g, rwkv_mu, w0, w2, a0, a2, g2,
           k_k, k_a, r_k, lnx_g, lnx_b, w_out, norm2_g, w_gate, w_up, w_down):
    b, s, d = x.shape
    assert s % (BLK * max(dil for _, dil in DILATED_PATTERNS)) == 0
    assert all(window // dil == BLK for window, dil in DILATED_PATTERNS)
    t = b * s
    assert t % IN_PROJ_ROWS == 0 and t % FFN_ROWS == 0
    row = lambda v: v.reshape(1, -1).astype(F32)

    x2d = x.reshape(t, d)
    per_head = lambda v: jnp.tile(v.reshape(1, HEAD_DIM), (1, ATTN_HEADS)).astype(F32)
    qk_gain = jnp.concatenate([per_head(q_norm_g) * (HEAD_DIM ** -0.5 * LOG2_E),
                               per_head(k_norm_g)], axis=1)
    proj, qkv = _in_proj(x, row(norm1_g), w_in.astype(BF16), qk_gain, IN_PROJ_ROWS)
    proj = proj.reshape(b, s, -1)

    slopes = jnp.exp2(-8.0 * jnp.arange(1, ATTN_HEADS + 1, dtype=F32) / ATTN_HEADS)
    attn = _attention(qkv.reshape(b, s, -1), slopes, row(attn_out_g))

    w2p = w2.reshape(DECAY_LORA, _N_PAIR, LANES).transpose(1, 0, 2)
    a2p = a2.reshape(ICLR_LORA, _N_PAIR, LANES).transpose(1, 0, 2)
    zero = jnp.zeros_like(w2p)
    wwa = jnp.concatenate([jnp.concatenate([w2p, zero], axis=2),
                           jnp.concatenate([jnp.zeros_like(a2p), a2p], axis=2)], axis=1)
    wwa = wwa.transpose(1, 0, 2).reshape(LANES, _N_PAIR * 2 * LANES).astype(BF16)
    rwkv = _rwkv(proj, row(rwkv_mu), row(w0), row(a0), row(k_k), row(k_a), row(r_k),
                 row(lnx_g), row(lnx_b), wwa, g2.astype(BF16))

    out = _out_ffn(x2d, attn.reshape(t, ATTN_WIDTH), rwkv.reshape(t, RWKV_WIDTH),
                   w_out[:ATTN_WIDTH].astype(BF16), w_out[ATTN_WIDTH:].astype(BF16),
                   row(norm2_g), w_gate.astype(BF16), w_up.astype(BF16), w_down.astype(BF16),
                   FFN_ROWS)
    return out.reshape(b, s, d)


def kernel(x, norm1_g, w_in, q_norm_g, k_norm_g, attn_out_g, rwkv_mu, w0, w2, a0, a2, g2,
           k_k, k_a, r_k, lnx_g, lnx_b, w_out, norm2_g, w_gate, w_up, w_down):
    h = x
    for layer in range(norm1_g.shape[0]):
        h = _layer(h, norm1_g[layer], w_in[layer], q_norm_g[layer], k_norm_g[layer],
                   attn_out_g[layer], rwkv_mu[layer], w0[layer], w2[layer], a0[layer],
                   a2[layer], g2[layer], k_k[layer], k_a[layer], r_k[layer], lnx_g[layer],
                   lnx_b[layer], w_out[layer], norm2_g[layer], w_gate[layer], w_up[layer],
                   w_down[layer])
    return h
```

```python
import functools

import jax
import jax.numpy as jnp
from jax import lax
from jax.experimental import pallas as pl
from jax.experimental.pallas import tpu as pltpu

F32 = jnp.float32
BF16 = jnp.bfloat16

HEAD_DIM = 64
LANES = 128
ATTN_HEADS = 8
RWKV_HEADS = 8
ATTN_WIDTH = ATTN_HEADS * HEAD_DIM
RWKV_WIDTH = RWKV_HEADS * HEAD_DIM
DECAY_LORA = 64
ICLR_LORA = 64
GATE_LORA = 128
DILATED_PATTERNS = ((128, 1), (512, 4), (2048, 16))
RESIDUES = max(dil for _, dil in DILATED_PATTERNS)
BLK = 128
ATTN_GROUP = 16
WKV_UNROLL = 2
IN_PROJ_ROWS = 1024
FFN_ROWS = 512
NORM_EPS = 1e-6
LNX_EPS = 64e-5
MASKED = -1e30
LOG2_E = 1.4426950408889634
VMEM_LIMIT = 56 * 1024 * 1024

_QB, _KB, _VB = 0, ATTN_WIDTH // LANES, 2 * ATTN_WIDTH // LANES
_N_PAIR = RWKV_WIDTH // LANES


def _nt_dot(a, b):
    return lax.dot_general(a, b, (((1,), (1,)), ((), ())), preferred_element_type=F32)


def _tn_dot(a, b):
    return lax.dot_general(a, b, (((0,), (0,)), ((), ())), preferred_element_type=F32)


def _dot(a, b):
    return jnp.dot(a, b, preferred_element_type=F32)


def _pairwise_dots(lhs, rhs):
    assert len(lhs) == len(rhs) and len(lhs) % 2 == 0
    out = []
    for j in range(0, len(lhs), 2):
        zero = jnp.zeros_like(rhs[j])
        both = _dot(jnp.concatenate([lhs[j], lhs[j + 1]], axis=1),
                    jnp.concatenate([jnp.concatenate([rhs[j], zero], axis=1),
                                     jnp.concatenate([zero, rhs[j + 1]], axis=1)], axis=0))
        out += [both[:, :LANES], both[:, LANES:]]
    return out


def _aligned(index, multiple):
    if isinstance(index, int):
        return index
    return pl.multiple_of(index, multiple)


def _head0_lanes():
    return lax.broadcasted_iota(jnp.int32, (1, LANES), 1) < HEAD_DIM


def _head_averager():
    i = lax.broadcasted_iota(jnp.int32, (LANES, LANES), 0)
    j = lax.broadcasted_iota(jnp.int32, (LANES, LANES), 1)
    return jnp.where((i < HEAD_DIM) == (j < HEAD_DIM), 1.0 / HEAD_DIM, 0.0).astype(BF16)


def _head_mean_mxu(x, averager):
    return _dot(x.astype(BF16), averager)


def _head_sum(x, head0):
    s0 = jnp.sum(jnp.where(head0, x, 0.0), axis=-1, keepdims=True)
    s1 = jnp.sum(jnp.where(head0, 0.0, x), axis=-1, keepdims=True)
    return jnp.where(head0, s0, s1)


def _stack_heads(x, head0):
    return jnp.concatenate([jnp.where(head0, x, 0.0), jnp.where(head0, 0.0, x)], axis=0)


def _window_row_token(u):
    per_residue = BLK // RESIDUES
    return RESIDUES * (u & (per_residue - 1)) + (u >> (per_residue.bit_length() - 1))


def _in_proj_kernel(x_ref, g_ref, w_ref, qk_gain_ref, rw_ref, qkv_ref):
    x = x_ref[...]
    ms = jnp.mean(x * x, axis=-1, keepdims=True)
    xn = (x * lax.rsqrt(ms + NORM_EPS) * g_ref[...]).astype(BF16)
    n_qk = qk_gain_ref.shape[1]
    n_attn = qkv_ref.shape[1]
    rw_ref[...] = _dot(xn, w_ref[:, n_attn:])
    u = lax.broadcasted_iota(jnp.int32, (BLK, BLK), 0)
    t = lax.broadcasted_iota(jnp.int32, (BLK, BLK), 1)
    regroup = jnp.where(t == _window_row_token(u), 1.0, 0.0).astype(BF16)
    xg = jnp.concatenate([_dot(regroup, xn[w * BLK:(w + 1) * BLK]).astype(BF16)
                          for w in range(x.shape[0] // BLK)], axis=0)
    qk = _dot(xg, w_ref[:, :n_qk])
    qkv_ref[:, n_qk:] = _dot(xg, w_ref[:, n_qk:n_attn])
    head0 = _head0_lanes()
    for j in range(n_qk // LANES):
        cols = slice(j * LANES, (j + 1) * LANES)
        v = qk[:, cols]
        mean_sq = _head_sum(v * v, head0) * (1.0 / HEAD_DIM)
        qkv_ref[:, cols] = v * lax.rsqrt(mean_sq + NORM_EPS) * qk_gain_ref[:, cols]


def _in_proj(x3d, g, w_bf16, qk_gain, tm):
    b, s, d = x3d.shape
    n = w_bf16.shape[1]
    n_attn = 3 * ATTN_WIDTH
    assert s % tm == 0 and tm % BLK == 0
    return pl.pallas_call(
        _in_proj_kernel,
        grid=(b * s // tm,),
        in_specs=[
            pl.BlockSpec((tm, d), lambda i: (i, 0)),
            pl.BlockSpec((1, d), lambda i: (0, 0)),
            pl.BlockSpec((d, n), lambda i: (0, 0), pipeline_mode=pl.Buffered(1)),
            pl.BlockSpec((1, qk_gain.shape[1]), lambda i: (0, 0)),
        ],
        out_specs=[
            pl.BlockSpec((tm, n - n_attn), lambda i: (i, 0)),
            pl.BlockSpec((tm, n_attn), lambda i: (i, 0)),
        ],
        out_shape=[
            jax.ShapeDtypeStruct((b * s, n - n_attn), F32),
            jax.ShapeDtypeStruct((b * s, n_attn), F32),
        ],
        compiler_params=pltpu.CompilerParams(
            dimension_semantics=("parallel",), vmem_limit_bytes=VMEM_LIMIT),
        name="in_proj",
    )(x3d.reshape(b * s, d), g, w_bf16, qk_gain)


def _attn_kernel(slope_ref, q_ref, k_ref, v_ref, go_ref, o_ref,
                 m_ref, l_ref, acc_ref, bias_ref):
    seq = q_ref.shape[1]
    hp = pl.program_id(0)
    head0 = _head0_lanes()
    averager = _head_averager()
    per_residue = BLK // RESIDUES

    def block_runs(dil, cls, n):
        if dil == 1:
            return [(_aligned(n * BLK, BLK), BLK)]
        return [(_aligned((n * dil + wl) * BLK + (dil * m + cls) * per_residue, per_residue),
                 per_residue) for wl in range(dil) for m in range(RESIDUES // dil)]

    def member_of(u, dil):
        per_window = BLK // dil
        in_window = u & (per_window - 1)
        return (per_window * (u >> (per_window.bit_length() - 1))
                + (RESIDUES // dil) * (u & (per_residue - 1))
                + (in_window >> (per_residue.bit_length() - 1)))

    def load_rows(ref, runs):
        return jnp.concatenate([ref[0, pl.ds(start, length), :] for start, length in runs], axis=0)

    def store_rows(ref, p, runs, value):
        offset = 0
        for start, length in runs:
            ref[p, pl.ds(start, length), :] = value[offset:offset + length]
            offset += length

    @pl.when(pl.program_id(1) == 0)
    def _():
        qi = lax.broadcasted_iota(jnp.int32, (2 * BLK, 2 * BLK), 0)
        ci = lax.broadcasted_iota(jnp.int32, (2 * BLK, 2 * BLK), 1)
        slope = jnp.where(qi < BLK, slope_ref[2 * hp], slope_ref[2 * hp + 1]) * LOG2_E
        for p, (window, dil) in enumerate(DILATED_PATTERNS):
            steps = window // dil
            back = ((BLK + member_of(qi & (BLK - 1), dil))
                    - ((ci & BLK) + member_of(ci & (BLK - 1), dil)))
            valid = (back >= 0) & (back <= steps)
            bias_ref[p] = jnp.where(valid, -slope * (back.astype(F32) * dil), MASKED)

    ones_keys = jnp.ones((2 * BLK, LANES), BF16)

    def band_blocks(p, dil, blocks):
        each = lambda f, *lists: [f(*args) for args in zip(*lists)]
        bf = lambda v: v.astype(BF16)
        runs = [block_runs(dil, cls, n) for cls, n in blocks]
        has_prevs = [n > 0 for _, n in blocks]

        def keys_values(block, own, has_prev):
            kb = load_rows(k_ref, own)
            vb = load_rows(v_ref, own)
            if has_prev:
                prev = block_runs(dil, block[0], block[1] - 1)
                kb = jnp.concatenate([load_rows(k_ref, prev), kb], axis=0)
                vb = jnp.concatenate([load_rows(v_ref, prev), vb], axis=0)
            return bf(kb), jnp.concatenate([bf(vb), ones_keys[:vb.shape[0]]], axis=1)
        kv = each(keys_values, blocks, runs, has_prevs)
        qs = each(lambda own: bf(_stack_heads(load_rows(q_ref, own), head0)), runs)
        s = each(lambda q, kvb, has_prev:
                 _nt_dot(q, kvb[0]) + (bias_ref[p] if has_prev else bias_ref[p, :, BLK:]),
                 qs, kv, has_prevs)
        m_blk = each(lambda v: jnp.max(v, axis=-1, keepdims=True), s)
        pr = each(lambda v, m: bf(jnp.exp2(v - m)), s, m_blk)
        pv = each(lambda v, kvb: _dot(v, kvb[1]), pr, kv)

        m_pair = each(lambda m_b: jnp.where(head0, m_b[:BLK], m_b[BLK:]), m_blk)
        acc_pair = each(lambda v: jnp.where(head0, v[:BLK, :LANES], v[BLK:, :LANES]), pv)
        l_pair = each(lambda v: jnp.where(head0, v[:BLK, LANES:], v[BLK:, LANES:]), pv)
        if p + 1 < len(DILATED_PATTERNS):
            def keep(own, m_b, acc_b, l_b):
                store_rows(m_ref, p, own, m_b)
                store_rows(acc_ref, p, own, acc_b)
                store_rows(l_ref, p, own, l_b)
            each(keep, runs, m_pair, acc_pair, l_pair)
            return

        assert dil == RESIDUES and seq == BLK * RESIDUES
        part = lambda ref, q, own: jnp.concatenate(
            [ref[q, pl.ds(start, length), :] for start, length in own], axis=0)
        m_prev = [[part(m_ref, q, own) for q in range(p)] for own in runs]
        m_all = each(lambda mine, others: functools.reduce(jnp.maximum, others, mine),
                     m_pair, m_prev)
        w_mine = each(lambda mine, top: jnp.exp2(mine - top), m_pair, m_all)
        w_prev = each(lambda others, top: [jnp.exp2(m - top) for m in others], m_prev, m_all)
        num = each(lambda own, wm, a, ws:
                   wm * a + sum(w * part(acc_ref, q, own) for q, w in enumerate(ws)),
                   runs, w_mine, acc_pair, w_prev)
        den = each(lambda own, wm, l, ws:
                   wm * l + sum(w * part(l_ref, q, own) for q, w in enumerate(ws)),
                   runs, w_mine, l_pair, w_prev)
        o = each(lambda a, b: a / b, num, den)
        ms = each(lambda v: _head_mean_mxu(v * v, averager), o)
        for (cls, _), v, mean_sq in zip(blocks, o, ms):
            o_ref[0, pl.ds(cls, BLK, stride=RESIDUES), :] = (
                v * lax.rsqrt(mean_sq + NORM_EPS) * go_ref[...])

    for p, (window, dil) in enumerate(DILATED_PATTERNS):
        n_blk = seq // (BLK * dil)
        group = ATTN_GROUP
        if n_blk >= group:
            assert n_blk % group == 0
            groups = n_blk // group

            def residue(r, carry, p=p, dil=dil, groups=groups, group=group):
                for grp in range(groups):
                    band_blocks(p, dil, [(r, n) for n in range(grp * group, (grp + 1) * group)])
                return carry
        else:
            assert group % n_blk == 0 and dil % (group // n_blk) == 0
            per_group = group // n_blk

            def residue(i, carry, p=p, dil=dil, n_blk=n_blk, per_group=per_group):
                classes = [i * per_group + g for g in range(per_group)]
                band_blocks(p, dil, [(r, n) for r in classes for n in range(n_blk)])
                return carry
        n_steps = dil if n_blk >= group else dil // (group // n_blk)
        if n_steps == 1:
            residue(0, 0)
        else:
            lax.fori_loop(0, n_steps, residue, 0)


def _attention(qkv, slopes, go):
    b, s, _ = qkv.shape
    assert BLK & (BLK - 1) == 0 and s % (RESIDUES * BLK) == 0
    blk = lambda off: pl.BlockSpec((1, s, LANES), lambda j, i: (i, 0, off + j))
    return pl.pallas_call(
        _attn_kernel,
        grid=(_N_PAIR, b),
        in_specs=[
            pl.BlockSpec(memory_space=pltpu.SMEM),
            blk(_QB), blk(_KB), blk(_VB),
            pl.BlockSpec((1, LANES), lambda j, i: (0, j)),
        ],
        out_specs=pl.BlockSpec((1, s, LANES), lambda j, i: (i, 0, j)),
        out_shape=jax.ShapeDtypeStruct((b, s, ATTN_WIDTH), F32),
        scratch_shapes=[
            pltpu.VMEM((len(DILATED_PATTERNS) - 1, s, LANES), F32),
            pltpu.VMEM((len(DILATED_PATTERNS) - 1, s, LANES), F32),
            pltpu.VMEM((len(DILATED_PATTERNS) - 1, s, LANES), F32),
            pltpu.VMEM((len(DILATED_PATTERNS), 2 * BLK, 2 * BLK), F32),
        ],
        compiler_params=pltpu.CompilerParams(
            dimension_semantics=("arbitrary", "arbitrary"), vmem_limit_bytes=VMEM_LIMIT),
        name="dilated_attn",
    )(slopes, qkv, qkv, qkv, go)


def _wkv_kernel(pr_ref, pk_ref, pv_ref, pwa_ref, pg_ref,
                mur_ref, muk_ref, muv_ref, muwa_ref, mug_ref,
                w0_ref, a0_ref, kk_ref, ka_ref, rk_ref, lng_ref, lnb_ref,
                wwa_ref, g2_ref, o_ref, h_ref, *slots):
    seq = pr_ref.shape[1]
    n_chunks = seq // BLK
    head0 = _head0_lanes()
    row = lax.broadcasted_iota(jnp.int32, (BLK, BLK), 0)
    col = lax.broadcasted_iota(jnp.int32, (BLK, BLK), 1)
    strict = col < row
    incl = col <= row
    tril_incl = jnp.where(incl, 1.0, 0.0).astype(BF16)
    first_row = lax.broadcasted_iota(jnp.int32, (BLK, LANES), 0) == 0
    srow = lax.broadcasted_iota(jnp.int32, (2 * BLK, LANES), 0) < BLK
    slane = lax.broadcasted_iota(jnp.int32, (2 * BLK, LANES), 1) < HEAD_DIM
    own_head = srow == slane
    zeros_blk = jnp.zeros((BLK, BLK), F32)
    zeros_bf = jnp.zeros((2 * BLK, 2 * BLK), BF16)
    eye_bf = jnp.where(lax.broadcasted_iota(jnp.int32, (2 * BLK, 2 * BLK), 0)
                       == lax.broadcasted_iota(jnp.int32, (2 * BLK, 2 * BLK), 1),
                       1.0, 0.0).astype(BF16)

    h_ref[...] = jnp.zeros(h_ref.shape, F32)

    each = lambda f, *lists: [f(*args) for args in zip(*lists)]
    bf = lambda v: v.astype(BF16)
    pairs = range(_N_PAIR)
    lanes = [slice(j * LANES, (j + 1) * LANES) for j in pairs]
    param = lambda ref: [ref[:, sl] for sl in lanes]

    def produce(c, slot):
        row0 = _aligned(c * BLK, BLK)
        rows = pl.ds(row0, BLK)
        prev8 = pl.ds(_aligned(max(row0 - 8, 0) if isinstance(row0, int)
                               else jnp.maximum(row0 - 8, 0), 8), 8)
        has_prev = jnp.where(c > 0, 1.0, 0.0).astype(F32)

        def shifted(ref, mu_ref, sl):
            p = ref[0, rows, sl]
            last = ref[0, prev8, sl][7:8, :] * has_prev
            prev = jnp.where(first_row, last, pltpu.roll(p, 1, 0))
            return p + (prev - p) * mu_ref[:, sl]

        xr = [shifted(pr_ref, mur_ref, sl) for sl in lanes]
        xk = [shifted(pk_ref, muk_ref, sl) for sl in lanes]
        xv = [shifted(pv_ref, muv_ref, sl) for sl in lanes]
        xwa = shifted(pwa_ref, muwa_ref, slice(None))
        xg = shifted(pg_ref, mug_ref, slice(None))
        yield

        lora_all = _dot(bf(jnp.where(head0, jnp.tanh(xwa), xwa)), wwa_ref[...])
        lora = [lora_all[:, 2 * LANES * j:2 * LANES * (j + 1)] for j in pairs]
        gate_all = _dot(bf(jax.nn.sigmoid(xg)), g2_ref[...])
        gate = [gate_all[:, sl] for sl in lanes]
        yield

        def decay_log(lo, w0):
            zw = -(w0 + lo[:, :LANES])
            softplus = jnp.maximum(zw, 0.0) + jnp.log(1.0 + jnp.exp(-jnp.abs(zw)))
            return -jnp.exp(-softplus - 0.5)
        log_decay = each(decay_log, lora, param(w0_ref))
        a_sig = each(lambda lo, a0: jax.nn.sigmoid(a0 + lo[:, LANES:]), lora, param(a0_ref))

        def unit_key(k, k_k):
            kkv = k * k_k
            return kkv / jnp.maximum(jnp.sqrt(_head_sum(kkv * kkv, head0)), 1e-12)
        kk = each(unit_key, xk, param(kk_ref))
        k2 = each(lambda k, a, k_a: k * (1.0 + (a - 1.0) * k_a), xk, a_sig, param(ka_ref))
        b_vec = each(lambda u, a: u * a, kk, a_sig)
        yield

        def cumulative(ld):
            ld_hi = bf(ld)
            ld_lo = bf(ld - ld_hi.astype(F32))
            cum = _dot(tril_incl, jnp.concatenate([ld_hi, ld_lo], axis=1))
            return cum[:, :LANES] + cum[:, LANES:]
        lw = each(cumulative, log_decay)
        yield
        lw_last = [v[BLK - 1:BLK, :] for v in lw]
        w_inv = each(lambda v: jnp.exp(-v), lw)
        w_tail = each(lambda v, last: jnp.exp(last - v), lw, lw_last)
        w_all = each(jnp.exp, lw_last)

        at = each(lambda u, v, ld: -u * jnp.exp(v - ld), kk, lw, log_decay)
        rt = each(lambda r, v: r * jnp.exp(v), xr, lw)
        bt = each(lambda b, w: b * w, b_vec, w_inv)
        kt = each(lambda k, w: k * w, k2, w_inv)
        bh = each(lambda b, w: b * w, b_vec, w_tail)
        kh = each(lambda k, w: k * w, k2, w_tail)
        yield

        def pair_products(a, r, b, k):
            lhs = jnp.concatenate([jnp.where(head0, a, 0.0), jnp.where(head0, r, 0.0),
                                   jnp.where(head0, 0.0, a), jnp.where(head0, 0.0, r)], axis=0)
            return _nt_dot(bf(lhs), bf(jnp.concatenate([b, k], axis=0)))
        pp = each(pair_products, at, rt, bt, kt)
        yield

        def n_matrix(q):
            aab0 = jnp.where(strict, q[0:BLK, :BLK], 0.0)
            aab1 = jnp.where(strict, q[2 * BLK:3 * BLK, :BLK], 0.0)
            return bf(jnp.concatenate([jnp.concatenate([aab0, zeros_blk], axis=1),
                                       jnp.concatenate([zeros_blk, aab1], axis=1)], axis=0))

        def aak_matrix(q):
            return bf(jnp.concatenate([jnp.where(strict, q[0:BLK, BLK:], 0.0),
                                       jnp.where(strict, q[2 * BLK:3 * BLK, BLK:], 0.0)], axis=0))
        akv = _pairwise_dots(each(aak_matrix, pp), each(bf, xv))

        def x_init(a, av):
            return jnp.concatenate([_stack_heads(a, head0), jnp.where(own_head, av, 0.0)], axis=1)

        def read_matrices(q):
            arb = jnp.concatenate([jnp.where(incl, q[BLK:2 * BLK, :BLK], 0.0),
                                   jnp.where(incl, q[3 * BLK:, :BLK], 0.0)], axis=1)
            ark = jnp.concatenate([jnp.where(incl, q[BLK:2 * BLK, BLK:], 0.0),
                                   jnp.where(incl, q[3 * BLK:, BLK:], 0.0)], axis=1)
            return bf(arb), bf(ark)

        for j in pairs:
            slot["n"][j] = n_matrix(pp[j])
            slot["x"][j] = bf(x_init(at[j], akv[j]))
            slot["arb"][j], slot["ark"][j] = read_matrices(pp[j])
        yield
        r_k = param(rk_ref)
        for j in pairs:
            slot["rt"][j] = rt[j]
            slot["bhs"][j] = bf(_stack_heads(bh[j], head0))
            slot["khs"][j] = bf(_stack_heads(kh[j], head0))
            slot["vs"][j] = bf(_stack_heads(xv[j], head0))
            slot["wall"][j] = jnp.broadcast_to(w_all[j], (8, LANES))
            slot["bonus"][j] = _head_sum(xr[j] * k2[j] * r_k[j], head0) * xv[j]
            slot["gate"][j] = gate[j]

    def consume(c, slot):
        n_chunk = _N_PAIR
        n_bf = [slot["n"][j] for j in range(n_chunk)]
        x_bf = [slot["x"][j] for j in range(n_chunk)]

        def live_rows(m, skip):
            if not skip:
                return m
            return jnp.concatenate([m[skip:BLK], m[BLK + skip:]], axis=0)

        def merge_rows(old, new, skip):
            if not skip:
                return new
            live = BLK - skip
            return jnp.concatenate([old[:skip], new[:live], old[BLK:BLK + skip], new[live:]],
                                   axis=0)

        span = 1
        while span < BLK:
            skip = span if span >= 16 else 0
            x_bf = each(lambda xx, nn: merge_rows(
                xx, bf(_dot(live_rows(nn + eye_bf, skip), xx)), skip), x_bf, n_bf)
            span *= 2
            if span < BLK:
                skip = span if span >= 16 else 0
                n_bf = each(lambda nn: merge_rows(
                    zeros_bf, bf(_dot(live_rows(nn, skip), nn)), skip), n_bf)
            yield
        v_s = [slot["vs"][j] for j in range(n_chunk)]

        def read_side(j, xb, vs):
            z = _dot(slot["arb"][j], xb)
            return slot["rt"][j] + z[:, :LANES], z[:, LANES:] + _dot(slot["ark"][j], vs)
        r_eff_y0 = each(read_side, range(n_chunk), x_bf, v_s)
        yield

        def state_side(j, xb, vs):
            d1 = _tn_dot(slot["bhs"][j], xb)
            d2 = _tn_dot(slot["khs"][j], vs)
            w_all = slot["wall"][j][0:1, :]
            return jnp.where(row == col, w_all, 0.0) + d1[:, :LANES], d1[:, LANES:] + d2
        m_eff_g_eff = each(state_side, range(n_chunk), x_bf, v_s)
        rm_lhs = each(lambda ry, mg: bf(jnp.concatenate([ry[0], mg[0]], axis=0)),
                      r_eff_y0, m_eff_g_eff)
        yield

        rm = _pairwise_dots(rm_lhs, [bf(h_ref[j]) for j in pairs])
        y = each(lambda v, ry: v[:BLK] + ry[1], rm, r_eff_y0)
        for j in pairs:
            h_ref[j] = rm[j][BLK:] + m_eff_g_eff[j][1]
        yield
        mean = each(lambda v: _head_sum(v, head0) * (1.0 / HEAD_DIM), y)
        yc = each(lambda v, m: v - m, y, mean)
        var = each(lambda v: _head_sum(v * v, head0) * (1.0 / HEAD_DIM), yc)
        rows = pl.ds(_aligned(c * BLK, BLK), BLK)
        lnx_g, lnx_b = param(lng_ref), param(lnb_ref)
        for j in pairs:
            yn = yc[j] * lax.rsqrt(var[j] + LNX_EPS) * lnx_g[j] + lnx_b[j]
            o_ref[0, rows, lanes[j]] = (yn + slot["bonus"][j]) * slot["gate"][j]

    def alternate(*stages):
        live = list(stages)
        while live:
            for gen in list(live):
                try:
                    next(gen)
                except StopIteration:
                    live.remove(gen)

    names = ("n", "x", "arb", "ark", "rt", "bhs", "khs", "vs", "wall", "bonus", "gate")
    slot_a = dict(zip(names, slots[:len(names)]))
    slot_b = dict(zip(names, slots[len(names):]))
    assert n_chunks % 2 == 0

    alternate(produce(0, slot_a))

    def chunk_pair(i, carry):
        for j in range(WKV_UNROLL):
            c = 2 * (WKV_UNROLL * i + j)
            alternate(consume(c, slot_a), produce(c + 1, slot_b))
            alternate(consume(c + 1, slot_b), produce(jnp.minimum(c + 2, n_chunks - 1), slot_a))
        return carry

    assert n_chunks % (2 * WKV_UNROLL) == 0
    lax.fori_loop(0, n_chunks // (2 * WKV_UNROLL), chunk_pair, 0)


def _wkv_slot_shapes():
    g = _N_PAIR
    return [
        pltpu.VMEM((g, 2 * BLK, 2 * BLK), BF16),
        pltpu.VMEM((g, 2 * BLK, 2 * LANES), BF16),
        pltpu.VMEM((g, BLK, 2 * BLK), BF16),
        pltpu.VMEM((g, BLK, 2 * BLK), BF16),
        pltpu.VMEM((g, BLK, LANES), F32),
        pltpu.VMEM((g, 2 * BLK, LANES), BF16),
        pltpu.VMEM((g, 2 * BLK, LANES), BF16),
        pltpu.VMEM((g, 2 * BLK, LANES), BF16),
        pltpu.VMEM((g, 8, LANES), F32),
        pltpu.VMEM((g, BLK, LANES), F32),
        pltpu.VMEM((g, BLK, LANES), F32),
    ]


def _rwkv(proj, mu, w0, a0, k_k, k_a, r_k, lnx_g, lnx_b, wwa, g2_bf16):
    b, s, _ = proj.shape
    np_ = _N_PAIR
    w = RWKV_WIDTH
    wide = lambda off: pl.BlockSpec((1, s, w), lambda i: (i, 0, off))
    lane = lambda off: pl.BlockSpec((1, s, LANES), lambda i: (i, 0, off))
    vec_w = lambda off: pl.BlockSpec((1, w), lambda i: (0, off))
    vec_l = lambda off: pl.BlockSpec((1, LANES), lambda i: (0, off))
    full = lambda shape: pl.BlockSpec(shape, lambda i: (0,) * len(shape))
    return pl.pallas_call(
        _wkv_kernel,
        grid=(b,),
        in_specs=[
            wide(0), wide(1), wide(2),
            lane(3 * np_), lane(3 * np_ + 1),
            vec_w(0), vec_w(1), vec_w(2), vec_l(3 * np_), vec_l(3 * np_ + 1),
            vec_w(0), vec_w(0), vec_w(0), vec_w(0), vec_w(0), vec_w(0), vec_w(0),
            full((LANES, np_ * 2 * LANES)),
            full((GATE_LORA, w)),
        ],
        out_specs=pl.BlockSpec((1, s, w), lambda i: (i, 0, 0)),
        out_shape=jax.ShapeDtypeStruct((b, s, RWKV_WIDTH), F32),
        scratch_shapes=[pltpu.VMEM((np_, LANES, LANES), F32)] + 2 * _wkv_slot_shapes(),
        compiler_params=pltpu.CompilerParams(
            dimension_semantics=("parallel",), vmem_limit_bytes=VMEM_LIMIT),
        name="rwkv7",
    )(proj, proj, proj, proj, proj, mu, mu, mu, mu, mu,
      w0, a0, k_k, k_a, r_k, lnx_g, lnx_b, wwa, g2_bf16)


def _out_ffn_kernel(x_ref, attn_ref, rwkv_ref, woa_ref, wor_ref, g2_ref,
                    wg_ref, wu_ref, wd_ref, o_ref):
    x1 = (x_ref[...] + _dot(attn_ref[...].astype(BF16), woa_ref[...])
          + _dot(rwkv_ref[...].astype(BF16), wor_ref[...]))
    ms = jnp.mean(x1 * x1, axis=-1, keepdims=True)
    xn = (x1 * lax.rsqrt(ms + NORM_EPS) * g2_ref[...]).astype(BF16)
    gate = _dot(xn, wg_ref[...])
    up = _dot(xn, wu_ref[...])
    hidden = (gate * jax.nn.sigmoid(gate) * up).astype(BF16)
    o_ref[...] = x1 + _dot(hidden, wd_ref[...])


def _out_ffn(x2d, attn2d, rwkv2d, woa, wor, g2, wg, wu, wd, tm):
    t, d = x2d.shape
    f = wg.shape[1]
    tok = lambda w: pl.BlockSpec((tm, w), lambda i: (i, 0))
    res = lambda shape: pl.BlockSpec(shape, lambda i: (0, 0), pipeline_mode=pl.Buffered(1))
    return pl.pallas_call(
        _out_ffn_kernel,
        grid=(t // tm,),
        in_specs=[
            tok(d), tok(ATTN_WIDTH), tok(RWKV_WIDTH),
            res((ATTN_WIDTH, d)), res((RWKV_WIDTH, d)),
            pl.BlockSpec((1, d), lambda i: (0, 0)),
            res((d, f)), res((d, f)), res((f, d)),
        ],
        out_specs=tok(d),
        out_shape=jax.ShapeDtypeStruct((t, d), F32),
        compiler_params=pltpu.CompilerParams(
            dimension_semantics=("parallel",), vmem_limit_bytes=VMEM_LIMIT),
        name="out_ffn",
    )(x2d, attn2d, rwkv2d, woa, wor, g2, wg, wu, wd)


def _layer(x, norm1_g, w_in, q_norm_g, k_norm_g, attn_out_g, rwkv_mu, w0, w2, a0, a2, g2,
           k_k, k_a, r_k, lnx_g, lnx_b, w_out, norm2_g, w_gate, w_up, w_down):
    b, s, d = x.shape
    assert s % (BLK * max(dil for _, dil in DILATED_PATTERNS)) == 0
    assert all(window // dil == BLK for window, dil in DILATED_PATTERNS)
    t = b * s
    assert t % IN_PROJ_ROWS == 0 and t % FFN_ROWS == 0
    row = lambda v: v.reshape(1, -1).astype(F32)

    x2d = x.reshape(t, d)
    per_head = lambda v: jnp.tile(v.reshape(1, HEAD_DIM), (1, ATTN_HEADS)).astype(F32)
    qk_gain = jnp.concatenate([per_head(q_norm_g) * (HEAD_DIM ** -0.5 * LOG2_E),
                               per_head(k_norm_g)], axis=1)
    proj, qkv = _in_proj(x, row(norm1_g), w_in.astype(BF16), qk_gain, IN_PROJ_ROWS)
    proj = proj.reshape(b, s, -1)

    slopes = jnp.exp2(-8.0 * jnp.arange(1, ATTN_HEADS + 1, dtype=F32) / ATTN_HEADS)
    attn = _attention(qkv.reshape(b, s, -1), slopes, row(attn_out_g))

    w2p = w2.reshape(DECAY_LORA, _N_PAIR, LANES).transpose(1, 0, 2)
    a2p = a2.reshape(ICLR_LORA, _N_PAIR, LANES).transpose(1, 0, 2)
    zero = jnp.zeros_like(w2p)
    wwa = jnp.concatenate([jnp.concatenate([w2p, zero], axis=2),
                           jnp.concatenate([jnp.zeros_like(a2p), a2p], axis=2)], axis=1)
    wwa = wwa.transpose(1, 0, 2).reshape(LANES, _N_PAIR * 2 * LANES).astype(BF16)
    rwkv = _rwkv(proj, row(rwkv_mu), row(w0), row(a0), row(k_k), row(k_a), row(r_k),
                 row(lnx_g), row(lnx_b), wwa, g2.astype(BF16))

    out = _out_ffn(x2d, attn.reshape(t, ATTN_WIDTH), rwkv.reshape(t, RWKV_WIDTH),
                   w_out[:ATTN_WIDTH].astype(BF16), w_out[ATTN_WIDTH:].astype(BF16),
                   row(norm2_g), w_gate.astype(BF16), w_up.astype(BF16), w_down.astype(BF16),
                   FFN_ROWS)
    return out.reshape(b, s, d)


def kernel(x, norm1_g, w_in, q_norm_g, k_norm_g, attn_out_g, rwkv_mu, w0, w2, a0, a2, g2,
           k_k, k_a, r_k, lnx_g, lnx_b, w_out, norm2_g, w_gate, w_up, w_down):
    h = x
    for layer in range(norm1_g.shape[0]):
        h = _layer(h, norm1_g[layer], w_in[layer], q_norm_g[layer], k_norm_g[layer],
                   attn_out_g[layer], rwkv_mu[layer], w0[layer], w2[layer], a0[layer],
                   a2[layer], g2[layer], k_k[layer], k_a[layer], r_k[layer], lnx_g[layer],
                   lnx_b[layer], w_out[layer], norm2_g[layer], w_gate[layer], w_up[layer],
                   w_down[layer])
    return h
```

```python
import functools

import jax
import jax.numpy as jnp
from jax import lax
from jax.experimental import pallas as pl
from jax.experimental.pallas import tpu as pltpu

F32 = jnp.float32
BF16 = jnp.bfloat16

HEAD_DIM = 64
LANES = 128
ATTN_HEADS = 8
RWKV_HEADS = 8
ATTN_WIDTH = ATTN_HEADS * HEAD_DIM
RWKV_WIDTH = RWKV_HEADS * HEAD_DIM
DECAY_LORA = 64
ICLR_LORA = 64
GATE_LORA = 128
DILATED_PATTERNS = ((128, 1), (512, 4), (2048, 16))
RESIDUES = max(dil for _, dil in DILATED_PATTERNS)
BLK = 128
ATTN_GROUP = 16
ATTN_WAVE = 4
IN_PROJ_ROWS = 1024
FFN_ROWS = 512
NORM_EPS = 1e-6
LNX_EPS = 64e-5
MASKED = -1e30
LOG2_E = 1.4426950408889634
VMEM_LIMIT = 56 * 1024 * 1024

_QB, _KB, _VB = 0, ATTN_WIDTH // LANES, 2 * ATTN_WIDTH // LANES
_N_PAIR = RWKV_WIDTH // LANES


def _nt_dot(a, b):
    return lax.dot_general(a, b, (((1,), (1,)), ((), ())), preferred_element_type=F32)


def _tn_dot(a, b):
    return lax.dot_general(a, b, (((0,), (0,)), ((), ())), preferred_element_type=F32)


def _dot(a, b):
    return jnp.dot(a, b, preferred_element_type=F32)


def _pairwise_dots(lhs, rhs):
    assert len(lhs) == len(rhs) and len(lhs) % 2 == 0
    out = []
    for j in range(0, len(lhs), 2):
        zero = jnp.zeros_like(rhs[j])
        both = _dot(jnp.concatenate([lhs[j], lhs[j + 1]], axis=1),
                    jnp.concatenate([jnp.concatenate([rhs[j], zero], axis=1),
                                     jnp.concatenate([zero, rhs[j + 1]], axis=1)], axis=0))
        out += [both[:, :LANES], both[:, LANES:]]
    return out


def _aligned(index, multiple):
    if isinstance(index, int):
        return index
    return pl.multiple_of(index, multiple)


def _head0_lanes():
    return lax.broadcasted_iota(jnp.int32, (1, LANES), 1) < HEAD_DIM


def _head_averager():
    i = lax.broadcasted_iota(jnp.int32, (LANES, LANES), 0)
    j = lax.broadcasted_iota(jnp.int32, (LANES, LANES), 1)
    return jnp.where((i < HEAD_DIM) == (j < HEAD_DIM), 1.0 / HEAD_DIM, 0.0).astype(BF16)


def _head_mean_mxu(x, averager):
    return _dot(x.astype(BF16), averager)


def _head_sum(x, head0):
    s0 = jnp.sum(jnp.where(head0, x, 0.0), axis=-1, keepdims=True)
    s1 = jnp.sum(jnp.where(head0, 0.0, x), axis=-1, keepdims=True)
    return jnp.where(head0, s0, s1)


def _stack_heads(x, head0):
    return jnp.concatenate([jnp.where(head0, x, 0.0), jnp.where(head0, 0.0, x)], axis=0)


def _window_row_token(u):
    per_residue = BLK // RESIDUES
    return RESIDUES * (u & (per_residue - 1)) + (u >> (per_residue.bit_length() - 1))


def _in_proj_kernel(x_ref, g_ref, w_ref, qk_gain_ref, rw_ref, qkv_ref):
    x = x_ref[...]
    ms = jnp.mean(x * x, axis=-1, keepdims=True)
    xn = (x * lax.rsqrt(ms + NORM_EPS) * g_ref[...]).astype(BF16)
    n_qk = qk_gain_ref.shape[1]
    n_attn = qkv_ref.shape[1]
    rw_ref[...] = _dot(xn, w_ref[:, n_attn:])
    u = lax.broadcasted_iota(jnp.int32, (BLK, BLK), 0)
    t = lax.broadcasted_iota(jnp.int32, (BLK, BLK), 1)
    regroup = jnp.where(t == _window_row_token(u), 1.0, 0.0).astype(BF16)
    xg = jnp.concatenate([_dot(regroup, xn[w * BLK:(w + 1) * BLK]).astype(BF16)
                          for w in range(x.shape[0] // BLK)], axis=0)
    qk = _dot(xg, w_ref[:, :n_qk])
    qkv_ref[:, n_qk:] = _dot(xg, w_ref[:, n_qk:n_attn])
    head0 = _head0_lanes()
    for j in range(n_qk // LANES):
        cols = slice(j * LANES, (j + 1) * LANES)
        v = qk[:, cols]
        mean_sq = _head_sum(v * v, head0) * (1.0 / HEAD_DIM)
        qkv_ref[:, cols] = v * lax.rsqrt(mean_sq + NORM_EPS) * qk_gain_ref[:, cols]


def _in_proj(x3d, g, w_bf16, qk_gain, tm):
    b, s, d = x3d.shape
    n = w_bf16.shape[1]
    n_attn = 3 * ATTN_WIDTH
    assert s % tm == 0 and tm % BLK == 0
    return pl.pallas_call(
        _in_proj_kernel,
        grid=(b * s // tm,),
        in_specs=[
            pl.BlockSpec((tm, d), lambda i: (i, 0)),
            pl.BlockSpec((1, d), lambda i: (0, 0)),
            pl.BlockSpec((d, n), lambda i: (0, 0), pipeline_mode=pl.Buffered(1)),
            pl.BlockSpec((1, qk_gain.shape[1]), lambda i: (0, 0)),
        ],
        out_specs=[
            pl.BlockSpec((tm, n - n_attn), lambda i: (i, 0)),
            pl.BlockSpec((tm, n_attn), lambda i: (i, 0)),
        ],
        out_shape=[
            jax.ShapeDtypeStruct((b * s, n - n_attn), F32),
            jax.ShapeDtypeStruct((b * s, n_attn), F32),
        ],
        compiler_params=pltpu.CompilerParams(
            dimension_semantics=("parallel",), vmem_limit_bytes=VMEM_LIMIT),
        name="in_proj",
    )(x3d.reshape(b * s, d), g, w_bf16, qk_gain)


def _attn_kernel(slope_ref, q_ref, k_ref, v_ref, go_ref, o_ref,
                 m_ref, l_ref, acc_ref, bias_ref):
    seq = q_ref.shape[1]
    hp = pl.program_id(0)
    head0 = _head0_lanes()
    averager = _head_averager()
    per_residue = BLK // RESIDUES

    def block_runs(dil, cls, n):
        if dil == 1:
            return [(_aligned(n * BLK, BLK), BLK)]
        return [(_aligned((n * dil + wl) * BLK + (dil * m + cls) * per_residue, per_residue),
                 per_residue) for wl in range(dil) for m in range(RESIDUES // dil)]

    def member_of(u, dil):
        per_window = BLK // dil
        in_window = u & (per_window - 1)
        return (per_window * (u >> (per_window.bit_length() - 1))
                + (RESIDUES // dil) * (u & (per_residue - 1))
                + (in_window >> (per_residue.bit_length() - 1)))

    def load_rows(ref, runs):
        return jnp.concatenate([ref[0, pl.ds(start, length), :] for start, length in runs], axis=0)

    def store_rows(ref, p, runs, value):
        offset = 0
        for start, length in runs:
            ref[p, pl.ds(start, length), :] = value[offset:offset + length]
            offset += length

    @pl.when(pl.program_id(1) == 0)
    def _():
        qi = lax.broadcasted_iota(jnp.int32, (2 * BLK, 2 * BLK), 0)
        ci = lax.broadcasted_iota(jnp.int32, (2 * BLK, 2 * BLK), 1)
        slope = jnp.where(qi < BLK, slope_ref[2 * hp], slope_ref[2 * hp + 1]) * LOG2_E
        for p, (window, dil) in enumerate(DILATED_PATTERNS):
            steps = window // dil
            back = ((BLK + member_of(qi & (BLK - 1), dil))
                    - ((ci & BLK) + member_of(ci & (BLK - 1), dil)))
            valid = (back >= 0) & (back <= steps)
            bias_ref[p] = jnp.where(valid, -slope * (back.astype(F32) * dil), MASKED)

    ones_keys = jnp.ones((2 * BLK, LANES), BF16)

    def band_blocks(p, dil, blocks):
        each = lambda f, *lists: [f(*args) for args in zip(*lists)]
        bf = lambda v: v.astype(BF16)
        runs = [block_runs(dil, cls, n) for cls, n in blocks]
        has_prevs = [n > 0 for _, n in blocks]

        def keys_values(block, own, has_prev):
            kb = load_rows(k_ref, own)
            vb = load_rows(v_ref, own)
            if has_prev:
                prev = block_runs(dil, block[0], block[1] - 1)
                kb = jnp.concatenate([load_rows(k_ref, prev), kb], axis=0)
                vb = jnp.concatenate([load_rows(v_ref, prev), vb], axis=0)
            return bf(kb), jnp.concatenate([bf(vb), ones_keys[:vb.shape[0]]], axis=1)
        kv = each(keys_values, blocks, runs, has_prevs)
        qs = each(lambda own: bf(_stack_heads(load_rows(q_ref, own), head0)), runs)
        yield
        s = each(lambda q, kvb, has_prev:
                 _nt_dot(q, kvb[0]) + (bias_ref[p] if has_prev else bias_ref[p, :, BLK:]),
                 qs, kv, has_prevs)
        yield
        m_blk = each(lambda v: jnp.max(v, axis=-1, keepdims=True), s)
        yield
        pr = each(lambda v, m: bf(jnp.exp2(v - m)), s, m_blk)
        yield
        pv = each(lambda v, kvb: _dot(v, kvb[1]), pr, kv)
        yield

        m_pair = each(lambda m_b: jnp.where(head0, m_b[:BLK], m_b[BLK:]), m_blk)
        acc_pair = each(lambda v: jnp.where(head0, v[:BLK, :LANES], v[BLK:, :LANES]), pv)
        l_pair = each(lambda v: jnp.where(head0, v[:BLK, LANES:], v[BLK:, LANES:]), pv)
        if p + 1 < len(DILATED_PATTERNS):
            def keep(own, m_b, acc_b, l_b):
                store_rows(m_ref, p, own, m_b)
                store_rows(acc_ref, p, own, acc_b)
                store_rows(l_ref, p, own, l_b)
            each(keep, runs, m_pair, acc_pair, l_pair)
            return

        assert dil == RESIDUES and seq == BLK * RESIDUES
        part = lambda ref, q, own: jnp.concatenate(
            [ref[q, pl.ds(start, length), :] for start, length in own], axis=0)
        m_prev = [[part(m_ref, q, own) for q in range(p)] for own in runs]
        m_all = each(lambda mine, others: functools.reduce(jnp.maximum, others, mine),
                     m_pair, m_prev)
        w_mine = each(lambda mine, top: jnp.exp2(mine - top), m_pair, m_all)
        w_prev = each(lambda others, top: [jnp.exp2(m - top) for m in others], m_prev, m_all)
        num = each(lambda own, wm, a, ws:
                   wm * a + sum(w * part(acc_ref, q, own) for q, w in enumerate(ws)),
                   runs, w_mine, acc_pair, w_prev)
        den = each(lambda own, wm, l, ws:
                   wm * l + sum(w * part(l_ref, q, own) for q, w in enumerate(ws)),
                   runs, w_mine, l_pair, w_prev)
        o = each(lambda a, b: a / b, num, den)
        ms = each(lambda v: _head_mean_mxu(v * v, averager), o)
        for (cls, _), v, mean_sq in zip(blocks, o, ms):
            o_ref[0, pl.ds(cls, BLK, stride=RESIDUES), :] = (
                v * lax.rsqrt(mean_sq + NORM_EPS) * go_ref[...])

    def run_waves(p, dil, blocks):
        waves = [band_blocks(p, dil, blocks[i:i + ATTN_WAVE])
                 for i in range(0, len(blocks), ATTN_WAVE)]
        started = 0
        while waves:
            started = min(started + 1, len(waves))
            for gen in list(waves[:started]):
                try:
                    next(gen)
                except StopIteration:
                    waves.remove(gen)
                    started -= 1

    for p, (window, dil) in enumerate(DILATED_PATTERNS):
        n_blk = seq // (BLK * dil)
        group = ATTN_GROUP
        if n_blk >= group:
            assert n_blk % group == 0
            groups = n_blk // group

            def residue(r, carry, p=p, dil=dil, groups=groups, group=group):
                for grp in range(groups):
                    run_waves(p, dil, [(r, n) for n in range(grp * group, (grp + 1) * group)])
                return carry
        else:
            assert group % n_blk == 0 and dil % (group // n_blk) == 0
            per_group = group // n_blk

            def residue(i, carry, p=p, dil=dil, n_blk=n_blk, per_group=per_group):
                classes = [i * per_group + g for g in range(per_group)]
                run_waves(p, dil, [(r, n) for r in classes for n in range(n_blk)])
                return carry
        n_steps = dil if n_blk >= group else dil // (group // n_blk)
        if n_steps == 1:
            residue(0, 0)
        else:
            lax.fori_loop(0, n_steps, residue, 0)


def _attention(qkv, slopes, go):
    b, s, _ = qkv.shape
    assert BLK & (BLK - 1) == 0 and s % (RESIDUES * BLK) == 0
    blk = lambda off: pl.BlockSpec((1, s, LANES), lambda j, i: (i, 0, off + j))
    return pl.pallas_call(
        _attn_kernel,
        grid=(_N_PAIR, b),
        in_specs=[
            pl.BlockSpec(memory_space=pltpu.SMEM),
            blk(_QB), blk(_KB), blk(_VB),
            pl.BlockSpec((1, LANES), lambda j, i: (0, j)),
        ],
        out_specs=pl.BlockSpec((1, s, LANES), lambda j, i: (i, 0, j)),
        out_shape=jax.ShapeDtypeStruct((b, s, ATTN_WIDTH), F32),
        scratch_shapes=[
            pltpu.VMEM((len(DILATED_PATTERNS) - 1, s, LANES), F32),
            pltpu.VMEM((len(DILATED_PATTERNS) - 1, s, LANES), F32),
            pltpu.VMEM((len(DILATED_PATTERNS) - 1, s, LANES), F32),
            pltpu.VMEM((len(DILATED_PATTERNS), 2 * BLK, 2 * BLK), F32),
        ],
        compiler_params=pltpu.CompilerParams(
            dimension_semantics=("arbitrary", "arbitrary"), vmem_limit_bytes=VMEM_LIMIT),
        name="dilated_attn",
    )(slopes, qkv, qkv, qkv, go)


def _wkv_kernel(pr_ref, pk_ref, pv_ref, pwa_ref, pg_ref,
                mur_ref, muk_ref, muv_ref, muwa_ref, mug_ref,
                w0_ref, a0_ref, kk_ref, ka_ref, rk_ref, lng_ref, lnb_ref,
                wwa_ref, g2_ref, o_ref, h_ref, *slots):
    seq = pr_ref.shape[1]
    n_chunks = seq // BLK
    head0 = _head0_lanes()
    row = lax.broadcasted_iota(jnp.int32, (BLK, BLK), 0)
    col = lax.broadcasted_iota(jnp.int32, (BLK, BLK), 1)
    strict = col < row
    incl = col <= row
    tril_incl = jnp.where(incl, 1.0, 0.0).astype(BF16)
    first_row = lax.broadcasted_iota(jnp.int32, (BLK, LANES), 0) == 0
    srow = lax.broadcasted_iota(jnp.int32, (2 * BLK, LANES), 0) < BLK
    slane = lax.broadcasted_iota(jnp.int32, (2 * BLK, LANES), 1) < HEAD_DIM
    own_head = srow == slane
    zeros_blk = jnp.zeros((BLK, BLK), F32)
    zeros_bf = jnp.zeros((2 * BLK, 2 * BLK), BF16)
    eye_bf = jnp.where(lax.broadcasted_iota(jnp.int32, (2 * BLK, 2 * BLK), 0)
                       == lax.broadcasted_iota(jnp.int32, (2 * BLK, 2 * BLK), 1),
                       1.0, 0.0).astype(BF16)

    h_ref[...] = jnp.zeros(h_ref.shape, F32)

    each = lambda f, *lists: [f(*args) for args in zip(*lists)]
    bf = lambda v: v.astype(BF16)
    pairs = range(_N_PAIR)
    lanes = [slice(j * LANES, (j + 1) * LANES) for j in pairs]
    param = lambda ref: [ref[:, sl] for sl in lanes]

    def produce(c, slot):
        row0 = _aligned(c * BLK, BLK)
        rows = pl.ds(row0, BLK)
        prev8 = pl.ds(_aligned(max(row0 - 8, 0) if isinstance(row0, int)
                               else jnp.maximum(row0 - 8, 0), 8), 8)
        has_prev = jnp.where(c > 0, 1.0, 0.0).astype(F32)

        def shifted(ref, mu_ref, sl):
            p = ref[0, rows, sl]
            last = ref[0, prev8, sl][7:8, :] * has_prev
            prev = jnp.where(first_row, last, pltpu.roll(p, 1, 0))
            return p + (prev - p) * mu_ref[:, sl]

        xr = [shifted(pr_ref, mur_ref, sl) for sl in lanes]
        xk = [shifted(pk_ref, muk_ref, sl) for sl in lanes]
        xv = [shifted(pv_ref, muv_ref, sl) for sl in lanes]
        xwa = shifted(pwa_ref, muwa_ref, slice(None))
        xg = shifted(pg_ref, mug_ref, slice(None))
        yield

        lora_all = _dot(bf(jnp.where(head0, jnp.tanh(xwa), xwa)), wwa_ref[...])
        lora = [lora_all[:, 2 * LANES * j:2 * LANES * (j + 1)] for j in pairs]
        gate_all = _dot(bf(jax.nn.sigmoid(xg)), g2_ref[...])
        gate = [gate_all[:, sl] for sl in lanes]
        yield

        def decay_log(lo, w0):
            zw = -(w0 + lo[:, :LANES])
            softplus = jnp.maximum(zw, 0.0) + jnp.log(1.0 + jnp.exp(-jnp.abs(zw)))
            return -jnp.exp(-softplus - 0.5)
        log_decay = each(decay_log, lora, param(w0_ref))
        a_sig = each(lambda lo, a0: jax.nn.sigmoid(a0 + lo[:, LANES:]), lora, param(a0_ref))

        def unit_key(k, k_k):
            kkv = k * k_k
            return kkv / jnp.maximum(jnp.sqrt(_head_sum(kkv * kkv, head0)), 1e-12)
        kk = each(unit_key, xk, param(kk_ref))
        k2 = each(lambda k, a, k_a: k * (1.0 + (a - 1.0) * k_a), xk, a_sig, param(ka_ref))
        b_vec = each(lambda u, a: u * a, kk, a_sig)
        yield

        def cumulative(ld):
            ld_hi = bf(ld)
            ld_lo = bf(ld - ld_hi.astype(F32))
            cum = _dot(tril_incl, jnp.concatenate([ld_hi, ld_lo], axis=1))
            return cum[:, :LANES] + cum[:, LANES:]
        lw = each(cumulative, log_decay)
        yield
        lw_last = [v[BLK - 1:BLK, :] for v in lw]
        w_inv = each(lambda v: jnp.exp(-v), lw)
        w_tail = each(lambda v, last: jnp.exp(last - v), lw, lw_last)
        w_all = each(jnp.exp, lw_last)

        at = each(lambda u, v, ld: -u * jnp.exp(v - ld), kk, lw, log_decay)
        rt = each(lambda r, v: r * jnp.exp(v), xr, lw)
        bt = each(lambda b, w: b * w, b_vec, w_inv)
        kt = each(lambda k, w: k * w, k2, w_inv)
        bh = each(lambda b, w: b * w, b_vec, w_tail)
        kh = each(lambda k, w: k * w, k2, w_tail)
        yield

        def pair_products(a, r, b, k):
            lhs = jnp.concatenate([jnp.where(head0, a, 0.0), jnp.where(head0, r, 0.0),
                                   jnp.where(head0, 0.0, a), jnp.where(head0, 0.0, r)], axis=0)
            return _nt_dot(bf(lhs), bf(jnp.concatenate([b, k], axis=0)))
        pp = each(pair_products, at, rt, bt, kt)
        yield

        def n_matrix(q):
            aab0 = jnp.where(strict, q[0:BLK, :BLK], 0.0)
            aab1 = jnp.where(strict, q[2 * BLK:3 * BLK, :BLK], 0.0)
            return bf(jnp.concatenate([jnp.concatenate([aab0, zeros_blk], axis=1),
                                       jnp.concatenate([zeros_blk, aab1], axis=1)], axis=0))

        def aak_matrix(q):
            return bf(jnp.concatenate([jnp.where(strict, q[0:BLK, BLK:], 0.0),
                                       jnp.where(strict, q[2 * BLK:3 * BLK, BLK:], 0.0)], axis=0))
        akv = _pairwise_dots(each(aak_matrix, pp), each(bf, xv))

        def x_init(a, av):
            return jnp.concatenate([_stack_heads(a, head0), jnp.where(own_head, av, 0.0)], axis=1)

        def read_matrices(q):
            arb = jnp.concatenate([jnp.where(incl, q[BLK:2 * BLK, :BLK], 0.0),
                                   jnp.where(incl, q[3 * BLK:, :BLK], 0.0)], axis=1)
            ark = jnp.concatenate([jnp.where(incl, q[BLK:2 * BLK, BLK:], 0.0),
                                   jnp.where(incl, q[3 * BLK:, BLK:], 0.0)], axis=1)
            return bf(arb), bf(ark)

        for j in pairs:
            slot["n"][j] = n_matrix(pp[j])
            slot["x"][j] = bf(x_init(at[j], akv[j]))
            slot["arb"][j], slot["ark"][j] = read_matrices(pp[j])
        yield
        r_k = param(rk_ref)
        for j in pairs:
            slot["rt"][j] = rt[j]
            slot["bhs"][j] = bf(_stack_heads(bh[j], head0))
            slot["khs"][j] = bf(_stack_heads(kh[j], head0))
            slot["vs"][j] = bf(_stack_heads(xv[j], head0))
            slot["wall"][j] = jnp.broadcast_to(w_all[j], (8, LANES))
            slot["bonus"][j] = _head_sum(xr[j] * k2[j] * r_k[j], head0) * xv[j]
            slot["gate"][j] = gate[j]

    def consume(c, slot):
        n_chunk = _N_PAIR
        n_bf = [slot["n"][j] for j in range(n_chunk)]
        x_bf = [slot["x"][j] for j in range(n_chunk)]

        def live_rows(m, skip):
            if not skip:
                return m
            return jnp.concatenate([m[skip:BLK], m[BLK + skip:]], axis=0)

        def merge_rows(old, new, skip):
            if not skip:
                return new
            live = BLK - skip
            return jnp.concatenate([old[:skip], new[:live], old[BLK:BLK + skip], new[live:]],
                                   axis=0)

        span = 1
        while span < BLK:
            skip = span if span >= 16 else 0
            x_bf = each(lambda xx, nn: merge_rows(
                xx, bf(_dot(live_rows(nn + eye_bf, skip), xx)), skip), x_bf, n_bf)
            span *= 2
            if span < BLK:
                skip = span if span >= 16 else 0
                n_bf = each(lambda nn: merge_rows(
                    zeros_bf, bf(_dot(live_rows(nn, skip), nn)), skip), n_bf)
            yield
        v_s = [slot["vs"][j] for j in range(n_chunk)]

        def read_side(j, xb, vs):
            z = _dot(slot["arb"][j], xb)
            return slot["rt"][j] + z[:, :LANES], z[:, LANES:] + _dot(slot["ark"][j], vs)
        r_eff_y0 = each(read_side, range(n_chunk), x_bf, v_s)
        yield

        def state_side(j, xb, vs):
            d1 = _tn_dot(slot["bhs"][j], xb)
            d2 = _tn_dot(slot["khs"][j], vs)
            w_all = slot["wall"][j][0:1, :]
            return jnp.where(row == col, w_all, 0.0) + d1[:, :LANES], d1[:, LANES:] + d2
        m_eff_g_eff = each(state_side, range(n_chunk), x_bf, v_s)
        rm_lhs = each(lambda ry, mg: bf(jnp.concatenate([ry[0], mg[0]], axis=0)),
                      r_eff_y0, m_eff_g_eff)
        yield

        rm = _pairwise_dots(rm_lhs, [bf(h_ref[j]) for j in pairs])
        y = each(lambda v, ry: v[:BLK] + ry[1], rm, r_eff_y0)
        for j in pairs:
            h_ref[j] = rm[j][BLK:] + m_eff_g_eff[j][1]
        yield
        mean = each(lambda v: _head_sum(v, head0) * (1.0 / HEAD_DIM), y)
        yc = each(lambda v, m: v - m, y, mean)
        var = each(lambda v: _head_sum(v * v, head0) * (1.0 / HEAD_DIM), yc)
        rows = pl.ds(_aligned(c * BLK, BLK), BLK)
        lnx_g, lnx_b = param(lng_ref), param(lnb_ref)
        for j in pairs:
            yn = yc[j] * lax.rsqrt(var[j] + LNX_EPS) * lnx_g[j] + lnx_b[j]
            o_ref[0, rows, lanes[j]] = (yn + slot["bonus"][j]) * slot["gate"][j]

    def alternate(*stages):
        live = list(stages)
        while live:
            for gen in list(live):
                try:
                    next(gen)
                except StopIteration:
                    live.remove(gen)

    names = ("n", "x", "arb", "ark", "rt", "bhs", "khs", "vs", "wall", "bonus", "gate")
    slot_a = dict(zip(names, slots[:len(names)]))
    slot_b = dict(zip(names, slots[len(names):]))
    assert n_chunks % 2 == 0

    alternate(produce(0, slot_a))

    def chunk_pair(i, carry):
        c = 2 * i
        alternate(consume(c, slot_a), produce(c + 1, slot_b))
        alternate(consume(c + 1, slot_b), produce(jnp.minimum(c + 2, n_chunks - 1), slot_a))
        return carry

    lax.fori_loop(0, n_chunks // 2, chunk_pair, 0)


def _wkv_slot_shapes():
    g = _N_PAIR
    return [
        pltpu.VMEM((g, 2 * BLK, 2 * BLK), BF16),
        pltpu.VMEM((g, 2 * BLK, 2 * LANES), BF16),
        pltpu.VMEM((g, BLK, 2 * BLK), BF16),
        pltpu.VMEM((g, BLK, 2 * BLK), BF16),
        pltpu.VMEM((g, BLK, LANES), F32),
        pltpu.VMEM((g, 2 * BLK, LANES), BF16),
        pltpu.VMEM((g, 2 * BLK, LANES), BF16),
        pltpu.VMEM((g, 2 * BLK, LANES), BF16),
        pltpu.VMEM((g, 8, LANES), F32),
        pltpu.VMEM((g, BLK, LANES), F32),
        pltpu.VMEM((g, BLK, LANES), F32),
    ]


def _rwkv(proj, mu, w0, a0, k_k, k_a, r_k, lnx_g, lnx_b, wwa, g2_bf16):
    b, s, _ = proj.shape
    np_ = _N_PAIR
    w = RWKV_WIDTH
    wide = lambda off: pl.BlockSpec((1, s, w), lambda i: (i, 0, off))
    lane = lambda off: pl.BlockSpec((1, s, LANES), lambda i: (i, 0, off))
    vec_w = lambda off: pl.BlockSpec((1, w), lambda i: (0, off))
    vec_l = lambda off: pl.BlockSpec((1, LANES), lambda i: (0, off))
    full = lambda shape: pl.BlockSpec(shape, lambda i: (0,) * len(shape))
    return pl.pallas_call(
        _wkv_kernel,
        grid=(b,),
        in_specs=[
            wide(0), wide(1), wide(2),
            lane(3 * np_), lane(3 * np_ + 1),
            vec_w(0), vec_w(1), vec_w(2), vec_l(3 * np_), vec_l(3 * np_ + 1),
            vec_w(0), vec_w(0), vec_w(0), vec_w(0), vec_w(0), vec_w(0), vec_w(0),
            full((LANES, np_ * 2 * LANES)),
            full((GATE_LORA, w)),
        ],
        out_specs=pl.BlockSpec((1, s, w), lambda i: (i, 0, 0)),
        out_shape=jax.ShapeDtypeStruct((b, s, RWKV_WIDTH), F32),
        scratch_shapes=[pltpu.VMEM((np_, LANES, LANES), F32)] + 2 * _wkv_slot_shapes(),
        compiler_params=pltpu.CompilerParams(
            dimension_semantics=("parallel",), vmem_limit_bytes=VMEM_LIMIT),
        name="rwkv7",
    )(proj, proj, proj, proj, proj, mu, mu, mu, mu, mu,
      w0, a0, k_k, k_a, r_k, lnx_g, lnx_b, wwa, g2_bf16)


def _out_ffn_kernel(x_ref, attn_ref, rwkv_ref, woa_ref, wor_ref, g2_ref,
                    wg_ref, wu_ref, wd_ref, o_ref):
    x1 = (x_ref[...] + _dot(attn_ref[...].astype(BF16), woa_ref[...])
          + _dot(rwkv_ref[...].astype(BF16), wor_ref[...]))
    ms = jnp.mean(x1 * x1, axis=-1, keepdims=True)
    xn = (x1 * lax.rsqrt(ms + NORM_EPS) * g2_ref[...]).astype(BF16)
    gate = _dot(xn, wg_ref[...])
    up = _dot(xn, wu_ref[...])
    hidden = (gate * jax.nn.sigmoid(gate) * up).astype(BF16)
    o_ref[...] = x1 + _dot(hidden, wd_ref[...])


def _out_ffn(x2d, attn2d, rwkv2d, woa, wor, g2, wg, wu, wd, tm):
    t, d = x2d.shape
    f = wg.shape[1]
    tok = lambda w: pl.BlockSpec((tm, w), lambda i: (i, 0))
    res = lambda shape: pl.BlockSpec(shape, lambda i: (0, 0), pipeline_mode=pl.Buffered(1))
    return pl.pallas_call(
        _out_ffn_kernel,
        grid=(t // tm,),
        in_specs=[
            tok(d), tok(ATTN_WIDTH), tok(RWKV_WIDTH),
            res((ATTN_WIDTH, d)), res((RWKV_WIDTH, d)),
            pl.BlockSpec((1, d), lambda i: (0, 0)),
            res((d, f)), res((d, f)), res((f, d)),
        ],
        out_specs=tok(d),
        out_shape=jax.ShapeDtypeStruct((t, d), F32),
        compiler_params=pltpu.CompilerParams(
            dimension_semantics=("parallel",), vmem_limit_bytes=VMEM_LIMIT),
        name="out_ffn",
    )(x2d, attn2d, rwkv2d, woa, wor, g2, wg, wu, wd)


def _layer(x, norm1_g, w_in, q_norm_g, k_norm_g, attn_out_g, rwkv_mu, w0, w2, a0, a2, g2,
           k_k, k_a, r_k, lnx_g, lnx_b, w_out, norm2_g, w_gate, w_up, w_down):
    b, s, d = x.shape
    assert s % (BLK * max(dil for _, dil in DILATED_PATTERNS)) == 0
    assert all(window // dil == BLK for window, dil in DILATED_PATTERNS)
    t = b * s
    assert t % IN_PROJ_ROWS == 0 and t % FFN_ROWS == 0
    row = lambda v: v.reshape(1, -1).astype(F32)

    x2d = x.reshape(t, d)
    per_head = lambda v: jnp.tile(v.reshape(1, HEAD_DIM), (1, ATTN_HEADS)).astype(F32)
    qk_gain = jnp.concatenate([per_head(q_norm_g) * (HEAD_DIM ** -0.5 * LOG2_E),
                               per_head(k_norm_g)], axis=1)
    proj, qkv = _in_proj(x, row(norm1_g), w_in.astype(BF16), qk_gain, IN_PROJ_ROWS)
    proj = proj.reshape(b, s, -1)

    slopes = jnp.exp2(-8.0 * jnp.arange(1, ATTN_HEADS + 1, dtype=F32) / ATTN_HEADS)
    attn = _attention(qkv.reshape(b, s, -1), slopes, row(attn_out_g))

    w2p = w2.reshape(DECAY_LORA, _N_PAIR, LANES).transpose(1, 0, 2)
    a2p = a2.reshape(ICLR_LORA, _N_PAIR, LANES).transpose(1, 0, 2)
    zero = jnp.zeros_like(w2p)
    wwa = jnp.concatenate([jnp.concatenate([w2p, zero], axis=2),
                           jnp.concatenate([jnp.zeros_like(a2p), a2p], axis=2)], axis=1)
    wwa = wwa.transpose(1, 0, 2).reshape(LANES, _N_PAIR * 2 * LANES).astype(BF16)
    rwkv = _rwkv(proj, row(rwkv_mu), row(w0), row(a0), row(k_k), row(k_a), row(r_k),
                 row(lnx_g), row(lnx_b), wwa, g2.astype(BF16))

    out = _out_ffn(x2d, attn.reshape(t, ATTN_WIDTH), rwkv.reshape(t, RWKV_WIDTH),
                   w_out[:ATTN_WIDTH].astype(BF16), w_out[ATTN_WIDTH:].astype(BF16),
                   row(norm2_g), w_gate.astype(BF16), w_up.astype(BF16), w_down.astype(BF16),
                   FFN_ROWS)
    return out.reshape(b, s, d)


def kernel(x, norm1_g, w_in, q_norm_g, k_norm_g, attn_out_g, rwkv_mu, w0, w2, a0, a2, g2,
           k_k, k_a, r_k, lnx_g, lnx_b, w_out, norm2_g, w_gate, w_up, w_down):
    h = x
    for layer in range(norm1_g.shape[0]):
        h = _layer(h, norm1_g[layer], w_in[layer], q_norm_g[layer], k_norm_g[layer],
                   attn_out_g[layer], rwkv_mu[layer], w0[layer], w2[layer], a0[layer],
                   a2[layer], g2[layer], k_k[layer], k_a[layer], r_k[layer], lnx_g[layer],
                   lnx_b[layer], w_out[layer], norm2_g[layer], w_gate[layer], w_up[layer],
                   w_down[layer])
    return h
```

```python
import functools

import jax
import jax.numpy as jnp
from jax import lax
from jax.experimental import pallas as pl
from jax.experimental.pallas import tpu as pltpu

F32 = jnp.float32
BF16 = jnp.bfloat16

HEAD_DIM = 64
LANES = 128
ATTN_HEADS = 8
RWKV_HEADS = 8
ATTN_WIDTH = ATTN_HEADS * HEAD_DIM
RWKV_WIDTH = RWKV_HEADS * HEAD_DIM
DECAY_LORA = 64
ICLR_LORA = 64
GATE_LORA = 128
DILATED_PATTERNS = ((128, 1), (512, 4), (2048, 16))
RESIDUES = max(dil for _, dil in DILATED_PATTERNS)
BLK = 128
ATTN_GROUP = 16
IN_PROJ_ROWS = 1024
FFN_ROWS = 512
NORM_EPS = 1e-6
LNX_EPS = 64e-5
MASKED = -1e30
LOG2_E = 1.4426950408889634
VMEM_LIMIT = 56 * 1024 * 1024

_QB, _KB, _VB = 0, ATTN_WIDTH // LANES, 2 * ATTN_WIDTH // LANES
_N_PAIR = RWKV_WIDTH // LANES


def _nt_dot(a, b):
    return lax.dot_general(a, b, (((1,), (1,)), ((), ())), preferred_element_type=F32)


def _tn_dot(a, b):
    return lax.dot_general(a, b, (((0,), (0,)), ((), ())), preferred_element_type=F32)


def _dot(a, b):
    return jnp.dot(a, b, preferred_element_type=F32)


def _pairwise_dots(lhs, rhs):
    assert len(lhs) == len(rhs) and len(lhs) % 2 == 0
    out = []
    for j in range(0, len(lhs), 2):
        zero = jnp.zeros_like(rhs[j])
        both = _dot(jnp.concatenate([lhs[j], lhs[j + 1]], axis=1),
                    jnp.concatenate([jnp.concatenate([rhs[j], zero], axis=1),
                                     jnp.concatenate([zero, rhs[j + 1]], axis=1)], axis=0))
        out += [both[:, :LANES], both[:, LANES:]]
    return out


def _aligned(index, multiple):
    if isinstance(index, int):
        return index
    return pl.multiple_of(index, multiple)


def _head0_lanes():
    return lax.broadcasted_iota(jnp.int32, (1, LANES), 1) < HEAD_DIM


def _head_averager():
    i = lax.broadcasted_iota(jnp.int32, (LANES, LANES), 0)
    j = lax.broadcasted_iota(jnp.int32, (LANES, LANES), 1)
    return jnp.where((i < HEAD_DIM) == (j < HEAD_DIM), 1.0 / HEAD_DIM, 0.0).astype(BF16)


def _head_mean_mxu(x, averager):
    return _dot(x.astype(BF16), averager)


def _head_sum(x, head0):
    s0 = jnp.sum(jnp.where(head0, x, 0.0), axis=-1, keepdims=True)
    s1 = jnp.sum(jnp.where(head0, 0.0, x), axis=-1, keepdims=True)
    return jnp.where(head0, s0, s1)


def _stack_heads(x, head0):
    return jnp.concatenate([jnp.where(head0, x, 0.0), jnp.where(head0, 0.0, x)], axis=0)


def _window_row_token(u):
    per_residue = BLK // RESIDUES
    return RESIDUES * (u & (per_residue - 1)) + (u >> (per_residue.bit_length() - 1))


def _in_proj_kernel(x_ref, g_ref, w_ref, qk_gain_ref, rw_ref, qkv_ref):
    x = x_ref[...]
    ms = jnp.mean(x * x, axis=-1, keepdims=True)
    xn = (x * lax.rsqrt(ms + NORM_EPS) * g_ref[...]).astype(BF16)
    n_qk = qk_gain_ref.shape[1]
    n_attn = qkv_ref.shape[1]
    rw_ref[...] = _dot(xn, w_ref[:, n_attn:])
    u = lax.broadcasted_iota(jnp.int32, (BLK, BLK), 0)
    t = lax.broadcasted_iota(jnp.int32, (BLK, BLK), 1)
    regroup = jnp.where(t == _window_row_token(u), 1.0, 0.0).astype(BF16)
    xg = jnp.concatenate([_dot(regroup, xn[w * BLK:(w + 1) * BLK]).astype(BF16)
                          for w in range(x.shape[0] // BLK)], axis=0)
    qk = _dot(xg, w_ref[:, :n_qk])
    qkv_ref[:, n_qk:] = _dot(xg, w_ref[:, n_qk:n_attn])
    head0 = _head0_lanes()
    for j in range(n_qk // LANES):
        cols = slice(j * LANES, (j + 1) * LANES)
        v = qk[:, cols]
        mean_sq = _head_sum(v * v, head0) * (1.0 / HEAD_DIM)
        qkv_ref[:, cols] = v * lax.rsqrt(mean_sq + NORM_EPS) * qk_gain_ref[:, cols]


def _in_proj(x3d, g, w_bf16, qk_gain, tm):
    b, s, d = x3d.shape
    n = w_bf16.shape[1]
    n_attn = 3 * ATTN_WIDTH
    assert s % tm == 0 and tm % BLK == 0
    return pl.pallas_call(
        _in_proj_kernel,
        grid=(b * s // tm,),
        in_specs=[
            pl.BlockSpec((tm, d), lambda i: (i, 0)),
            pl.BlockSpec((1, d), lambda i: (0, 0)),
            pl.BlockSpec((d, n), lambda i: (0, 0), pipeline_mode=pl.Buffered(1)),
            pl.BlockSpec((1, qk_gain.shape[1]), lambda i: (0, 0)),
        ],
        out_specs=[
            pl.BlockSpec((tm, n - n_attn), lambda i: (i, 0)),
            pl.BlockSpec((tm, n_attn), lambda i: (i, 0)),
        ],
        out_shape=[
            jax.ShapeDtypeStruct((b * s, n - n_attn), F32),
            jax.ShapeDtypeStruct((b * s, n_attn), F32),
        ],
        compiler_params=pltpu.CompilerParams(
            dimension_semantics=("parallel",), vmem_limit_bytes=VMEM_LIMIT),
        name="in_proj",
    )(x3d.reshape(b * s, d), g, w_bf16, qk_gain)


def _attn_kernel(slope_ref, q_ref, k_ref, v_ref, go_ref, o_ref,
                 m_ref, l_ref, acc_ref, bias_ref):
    seq = q_ref.shape[1]
    hp = pl.program_id(0)
    head0 = _head0_lanes()
    averager = _head_averager()
    per_residue = BLK // RESIDUES

    def block_runs(dil, cls, n):
        if dil == 1:
            return [(_aligned(n * BLK, BLK), BLK)]
        return [(_aligned((n * dil + wl) * BLK + (dil * m + cls) * per_residue, per_residue),
                 per_residue) for wl in range(dil) for m in range(RESIDUES // dil)]

    def member_of(u, dil):
        per_window = BLK // dil
        in_window = u & (per_window - 1)
        return (per_window * (u >> (per_window.bit_length() - 1))
                + (RESIDUES // dil) * (u & (per_residue - 1))
                + (in_window >> (per_residue.bit_length() - 1)))

    def load_rows(ref, runs):
        return jnp.concatenate([ref[0, pl.ds(start, length), :] for start, length in runs], axis=0)

    def store_rows(ref, p, runs, value):
        offset = 0
        for start, length in runs:
            ref[p, pl.ds(start, length), :] = value[offset:offset + length]
            offset += length

    @pl.when(pl.program_id(1) == 0)
    def _():
        qi = lax.broadcasted_iota(jnp.int32, (2 * BLK, 2 * BLK), 0)
        ci = lax.broadcasted_iota(jnp.int32, (2 * BLK, 2 * BLK), 1)
        slope = jnp.where(qi < BLK, slope_ref[2 * hp], slope_ref[2 * hp + 1]) * LOG2_E
        for p, (window, dil) in enumerate(DILATED_PATTERNS):
            steps = window // dil
            back = ((BLK + member_of(qi & (BLK - 1), dil))
                    - ((ci & BLK) + member_of(ci & (BLK - 1), dil)))
            valid = (back >= 0) & (back <= steps)
            bias_ref[p] = jnp.where(valid, -slope * (back.astype(F32) * dil), MASKED)

    ones_keys = jnp.ones((2 * BLK, LANES), BF16)

    def band_blocks(p, dil, blocks):
        each = lambda f, *lists: [f(*args) for args in zip(*lists)]
        bf = lambda v: v.astype(BF16)
        runs = [block_runs(dil, cls, n) for cls, n in blocks]
        has_prevs = [n > 0 for _, n in blocks]

        def keys_values(block, own, has_prev):
            kb = load_rows(k_ref, own)
            vb = load_rows(v_ref, own)
            if has_prev:
                prev = block_runs(dil, block[0], block[1] - 1)
                kb = jnp.concatenate([load_rows(k_ref, prev), kb], axis=0)
                vb = jnp.concatenate([load_rows(v_ref, prev), vb], axis=0)
            return bf(kb), jnp.concatenate([bf(vb), ones_keys[:vb.shape[0]]], axis=1)
        kv = each(keys_values, blocks, runs, has_prevs)
        qs = each(lambda own: bf(_stack_heads(load_rows(q_ref, own), head0)), runs)
        s = each(lambda q, kvb, has_prev:
                 _nt_dot(q, kvb[0]) + (bias_ref[p] if has_prev else bias_ref[p, :, BLK:]),
                 qs, kv, has_prevs)
        m_blk = each(lambda v: jnp.max(v, axis=-1, keepdims=True), s)
        pr = each(lambda v, m: bf(jnp.exp2(v - m)), s, m_blk)
        pv = each(lambda v, kvb: _dot(v, kvb[1]), pr, kv)

        m_pair = each(lambda m_b: jnp.where(head0, m_b[:BLK], m_b[BLK:]), m_blk)
        acc_pair = each(lambda v: jnp.where(head0, v[:BLK, :LANES], v[BLK:, :LANES]), pv)
        l_pair = each(lambda v: jnp.where(head0, v[:BLK, LANES:], v[BLK:, LANES:]), pv)
        if p + 1 < len(DILATED_PATTERNS):
            def keep(own, m_b, acc_b, l_b):
                store_rows(m_ref, p, own, m_b)
                store_rows(acc_ref, p, own, acc_b)
                store_rows(l_ref, p, own, l_b)
            each(keep, runs, m_pair, acc_pair, l_pair)
            return

        assert dil == RESIDUES and seq == BLK * RESIDUES
        part = lambda ref, q, own: jnp.concatenate(
            [ref[q, pl.ds(start, length), :] for start, length in own], axis=0)
        m_prev = [[part(m_ref, q, own) for q in range(p)] for own in runs]
        m_all = each(lambda mine, others: functools.reduce(jnp.maximum, others, mine),
                     m_pair, m_prev)
        w_mine = each(lambda mine, top: jnp.exp2(mine - top), m_pair, m_all)
        w_prev = each(lambda others, top: [jnp.exp2(m - top) for m in others], m_prev, m_all)
        num = each(lambda own, wm, a, ws:
                   wm * a + sum(w * part(acc_ref, q, own) for q, w in enumerate(ws)),
                   runs, w_mine, acc_pair, w_prev)
        den = each(lambda own, wm, l, ws:
                   wm * l + sum(w * part(l_ref, q, own) for q, w in enumerate(ws)),
                   runs, w_mine, l_pair, w_prev)
        o = each(lambda a, b: a / b, num, den)
        ms = each(lambda v: _head_mean_mxu(v * v, averager), o)
        for (cls, _), v, mean_sq in zip(blocks, o, ms):
            o_ref[0, pl.ds(cls, BLK, stride=RESIDUES), :] = (
                v * lax.rsqrt(mean_sq + NORM_EPS) * go_ref[...])

    for p, (window, dil) in enumerate(DILATED_PATTERNS):
        n_blk = seq // (BLK * dil)
        group = ATTN_GROUP
        if n_blk >= group:
            assert n_blk % group == 0
            groups = n_blk // group

            def residue(r, carry, p=p, dil=dil, groups=groups, group=group):
                for grp in range(groups):
                    band_blocks(p, dil, [(r, n) for n in range(grp * group, (grp + 1) * group)])
                return carry
        else:
            assert group % n_blk == 0 and dil % (group // n_blk) == 0
            per_group = group // n_blk

            def residue(i, carry, p=p, dil=dil, n_blk=n_blk, per_group=per_group):
                classes = [i * per_group + g for g in range(per_group)]
                band_blocks(p, dil, [(r, n) for r in classes for n in range(n_blk)])
                return carry
        n_steps = dil if n_blk >= group else dil // (group // n_blk)
        if n_steps == 1:
            residue(0, 0)
        else:
            lax.fori_loop(0, n_steps, residue, 0)


def _attention(qkv, slopes, go):
    b, s, _ = qkv.shape
    assert BLK & (BLK - 1) == 0 and s % (RESIDUES * BLK) == 0
    blk = lambda off: pl.BlockSpec((1, s, LANES), lambda j, i: (i, 0, off + j))
    return pl.pallas_call(
        _attn_kernel,
        grid=(_N_PAIR, b),
        in_specs=[
            pl.BlockSpec(memory_space=pltpu.SMEM),
            blk(_QB), blk(_KB), blk(_VB),
            pl.BlockSpec((1, LANES), lambda j, i: (0, j)),
        ],
        out_specs=pl.BlockSpec((1, s, LANES), lambda j, i: (i, 0, j)),
        out_shape=jax.ShapeDtypeStruct((b, s, ATTN_WIDTH), F32),
        scratch_shapes=[
            pltpu.VMEM((len(DILATED_PATTERNS) - 1, s, LANES), F32),
            pltpu.VMEM((len(DILATED_PATTERNS) - 1, s, LANES), F32),
            pltpu.VMEM((len(DILATED_PATTERNS) - 1, s, LANES), F32),
            pltpu.VMEM((len(DILATED_PATTERNS), 2 * BLK, 2 * BLK), F32),
        ],
        compiler_params=pltpu.CompilerParams(
            dimension_semantics=("arbitrary", "arbitrary"), vmem_limit_bytes=VMEM_LIMIT),
        name="dilated_attn",
    )(slopes, qkv, qkv, qkv, go)


def _wkv_kernel(pr_ref, pk_ref, pv_ref, pwa_ref, pg_ref,
                mur_ref, muk_ref, muv_ref, muwa_ref, mug_ref,
                w0_ref, a0_ref, kk_ref, ka_ref, rk_ref, lng_ref, lnb_ref,
                wwa_ref, g2_ref, o_ref, h_ref, *slots):
    seq = pr_ref.shape[1]
    n_chunks = seq // BLK
    head0 = _head0_lanes()
    row = lax.broadcasted_iota(jnp.int32, (BLK, BLK), 0)
    col = lax.broadcasted_iota(jnp.int32, (BLK, BLK), 1)
    strict = col < row
    incl = col <= row
    tril_incl = jnp.where(incl, 1.0, 0.0).astype(BF16)
    first_row = lax.broadcasted_iota(jnp.int32, (BLK, LANES), 0) == 0
    srow = lax.broadcasted_iota(jnp.int32, (2 * BLK, LANES), 0) < BLK
    slane = lax.broadcasted_iota(jnp.int32, (2 * BLK, LANES), 1) < HEAD_DIM
    own_head = srow == slane
    zeros_blk = jnp.zeros((BLK, BLK), F32)
    zeros_bf = jnp.zeros((2 * BLK, 2 * BLK), BF16)
    eye_bf = jnp.where(lax.broadcasted_iota(jnp.int32, (2 * BLK, 2 * BLK), 0)
                       == lax.broadcasted_iota(jnp.int32, (2 * BLK, 2 * BLK), 1),
                       1.0, 0.0).astype(BF16)

    h_ref[...] = jnp.zeros(h_ref.shape, F32)

    each = lambda f, *lists: [f(*args) for args in zip(*lists)]
    bf = lambda v: v.astype(BF16)
    pairs = range(_N_PAIR)
    lanes = [slice(j * LANES, (j + 1) * LANES) for j in pairs]
    param = lambda ref: [ref[:, sl] for sl in lanes]

    def produce(c, slot):
        row0 = _aligned(c * BLK, BLK)
        rows = pl.ds(row0, BLK)
        prev8 = pl.ds(_aligned(max(row0 - 8, 0) if isinstance(row0, int)
                               else jnp.maximum(row0 - 8, 0), 8), 8)
        has_prev = jnp.where(c > 0, 1.0, 0.0).astype(F32)

        def shifted(ref, mu_ref, sl):
            p = ref[0, rows, sl]
            last = ref[0, prev8, sl][7:8, :] * has_prev
            prev = jnp.where(first_row, last, pltpu.roll(p, 1, 0))
            return p + (prev - p) * mu_ref[:, sl]

        xr = [shifted(pr_ref, mur_ref, sl) for sl in lanes]
        xk = [shifted(pk_ref, muk_ref, sl) for sl in lanes]
        xv = [shifted(pv_ref, muv_ref, sl) for sl in lanes]
        xwa = shifted(pwa_ref, muwa_ref, slice(None))
        xg = shifted(pg_ref, mug_ref, slice(None))
        yield

        lora_all = _dot(bf(jnp.where(head0, jnp.tanh(xwa), xwa)), wwa_ref[...])
        lora = [lora_all[:, 2 * LANES * j:2 * LANES * (j + 1)] for j in pairs]
        gate_all = _dot(bf(jax.nn.sigmoid(xg)), g2_ref[...])
        gate = [gate_all[:, sl] for sl in lanes]
        yield

        def decay_log(lo, w0):
            zw = -(w0 + lo[:, :LANES])
            softplus = jnp.maximum(zw, 0.0) + jnp.log(1.0 + jnp.exp(-jnp.abs(zw)))
            return -jnp.exp(-softplus - 0.5)
        log_decay = each(decay_log, lora, param(w0_ref))
        a_sig = each(lambda lo, a0: jax.nn.sigmoid(a0 + lo[:, LANES:]), lora, param(a0_ref))

        def unit_key(k, k_k):
            kkv = k * k_k
            return kkv / jnp.maximum(jnp.sqrt(_head_sum(kkv * kkv, head0)), 1e-12)
        kk = each(unit_key, xk, param(kk_ref))
        k2 = each(lambda k, a, k_a: k * (1.0 + (a - 1.0) * k_a), xk, a_sig, param(ka_ref))
        b_vec = each(lambda u, a: u * a, kk, a_sig)
        yield

        def cumulative(ld):
            ld_hi = bf(ld)
            ld_lo = bf(ld - ld_hi.astype(F32))
            cum = _dot(tril_incl, jnp.concatenate([ld_hi, ld_lo], axis=1))
            return cum[:, :LANES] + cum[:, LANES:]
        lw = each(cumulative, log_decay)
        yield
        lw_last = [v[BLK - 1:BLK, :] for v in lw]
        w_inv = each(lambda v: jnp.exp(-v), lw)
        w_tail = each(lambda v, last: jnp.exp(last - v), lw, lw_last)
        w_all = each(jnp.exp, lw_last)

        at = each(lambda u, v, ld: -u * jnp.exp(v - ld), kk, lw, log_decay)
        rt = each(lambda r, v: r * jnp.exp(v), xr, lw)
        bt = each(lambda b, w: b * w, b_vec, w_inv)
        kt = each(lambda k, w: k * w, k2, w_inv)
        bh = each(lambda b, w: b * w, b_vec, w_tail)
        kh = each(lambda k, w: k * w, k2, w_tail)
        yield

        def pair_products(a, r, b, k):
            lhs = jnp.concatenate([jnp.where(head0, a, 0.0), jnp.where(head0, r, 0.0),
                                   jnp.where(head0, 0.0, a), jnp.where(head0, 0.0, r)], axis=0)
            return _nt_dot(bf(lhs), bf(jnp.concatenate([b, k], axis=0)))
        pp = each(pair_products, at, rt, bt, kt)
        yield

        def n_matrix(q):
            aab0 = jnp.where(strict, q[0:BLK, :BLK], 0.0)
            aab1 = jnp.where(strict, q[2 * BLK:3 * BLK, :BLK], 0.0)
            return bf(jnp.concatenate([jnp.concatenate([aab0, zeros_blk], axis=1),
                                       jnp.concatenate([zeros_blk, aab1], axis=1)], axis=0))

        def aak_matrix(q):
            return bf(jnp.concatenate([jnp.where(strict, q[0:BLK, BLK:], 0.0),
                                       jnp.where(strict, q[2 * BLK:3 * BLK, BLK:], 0.0)], axis=0))
        akv = _pairwise_dots(each(aak_matrix, pp), each(bf, xv))

        def x_init(a, av):
            return jnp.concatenate([_stack_heads(a, head0), jnp.where(own_head, av, 0.0)], axis=1)

        def read_matrices(q):
            arb = jnp.concatenate([jnp.where(incl, q[BLK:2 * BLK, :BLK], 0.0),
                                   jnp.where(incl, q[3 * BLK:, :BLK], 0.0)], axis=1)
            ark = jnp.concatenate([jnp.where(incl, q[BLK:2 * BLK, BLK:], 0.0),
                                   jnp.where(incl, q[3 * BLK:, BLK:], 0.0)], axis=1)
            return bf(arb), bf(ark)

        for j in pairs:
            slot["n"][j] = n_matrix(pp[j])
            slot["x"][j] = bf(x_init(at[j], akv[j]))
            slot["arb"][j], slot["ark"][j] = read_matrices(pp[j])
        yield
        r_k = param(rk_ref)
        for j in pairs:
            slot["rt"][j] = rt[j]
            slot["bhs"][j] = bf(_stack_heads(bh[j], head0))
            slot["khs"][j] = bf(_stack_heads(kh[j], head0))
            slot["vs"][j] = bf(_stack_heads(xv[j], head0))
            slot["wall"][j] = jnp.broadcast_to(w_all[j], (8, LANES))
            slot["bonus"][j] = _head_sum(xr[j] * k2[j] * r_k[j], head0) * xv[j]
            slot["gate"][j] = gate[j]

    def consume(slot, hand):
        n_chunk = _N_PAIR
        n_bf = [slot["n"][j] for j in range(n_chunk)]
        x_bf = [slot["x"][j] for j in range(n_chunk)]

        def live_rows(m, skip):
            if not skip:
                return m
            return jnp.concatenate([m[skip:BLK], m[BLK + skip:]], axis=0)

        def merge_rows(old, new, skip):
            if not skip:
                return new
            live = BLK - skip
            return jnp.concatenate([old[:skip], new[:live], old[BLK:BLK + skip], new[live:]],
                                   axis=0)

        span = 1
        while span < BLK:
            skip = span if span >= 16 else 0
            x_bf = each(lambda xx, nn: merge_rows(
                xx, bf(_dot(live_rows(nn + eye_bf, skip), xx)), skip), x_bf, n_bf)
            span *= 2
            if span < BLK:
                skip = span if span >= 16 else 0
                n_bf = each(lambda nn: merge_rows(
                    zeros_bf, bf(_dot(live_rows(nn, skip), nn)), skip), n_bf)
            yield
        v_s = [slot["vs"][j] for j in range(n_chunk)]

        def read_side(j, xb, vs):
            z = _dot(slot["arb"][j], xb)
            return slot["rt"][j] + z[:, :LANES], z[:, LANES:] + _dot(slot["ark"][j], vs)
        r_eff_y0 = each(read_side, range(n_chunk), x_bf, v_s)
        yield

        def state_side(j, xb, vs):
            d1 = _tn_dot(slot["bhs"][j], xb)
            d2 = _tn_dot(slot["khs"][j], vs)
            w_all = slot["wall"][j][0:1, :]
            return jnp.where(row == col, w_all, 0.0) + d1[:, :LANES], d1[:, LANES:] + d2
        m_eff_g_eff = each(state_side, range(n_chunk), x_bf, v_s)
        for j in pairs:
            hand["rm_lhs"][j] = bf(jnp.concatenate([r_eff_y0[j][0], m_eff_g_eff[j][0]], axis=0))
            hand["y0"][j] = r_eff_y0[j][1]
            hand["g_eff"][j] = m_eff_g_eff[j][1]
            hand["bonus"][j] = slot["bonus"][j]
            hand["gate"][j] = slot["gate"][j]

    def finish(c, hand):
        rm = _pairwise_dots([hand["rm_lhs"][j] for j in pairs], [bf(h_ref[j]) for j in pairs])
        y = [rm[j][:BLK] + hand["y0"][j] for j in pairs]
        for j in pairs:
            h_ref[j] = rm[j][BLK:] + hand["g_eff"][j]
        yield
        mean = each(lambda v: _head_sum(v, head0) * (1.0 / HEAD_DIM), y)
        yc = each(lambda v, m: v - m, y, mean)
        yield
        var = each(lambda v: _head_sum(v * v, head0) * (1.0 / HEAD_DIM), yc)
        rows = pl.ds(_aligned(c * BLK, BLK), BLK)
        lnx_g, lnx_b = param(lng_ref), param(lnb_ref)
        yield
        for j in pairs:
            yn = yc[j] * lax.rsqrt(var[j] + LNX_EPS) * lnx_g[j] + lnx_b[j]
            o_ref[0, rows, lanes[j]] = (yn + hand["bonus"][j]) * hand["gate"][j]

    def alternate(*stages):
        live = list(stages)
        while live:
            for gen in list(live):
                try:
                    next(gen)
                except StopIteration:
                    live.remove(gen)

    names = ("n", "x", "arb", "ark", "rt", "bhs", "khs", "vs", "wall", "bonus", "gate")
    hand_names = ("rm_lhs", "y0", "g_eff", "bonus", "gate")
    n_slot, n_hand = len(names), len(hand_names)
    slot_a = dict(zip(names, slots[:n_slot]))
    slot_b = dict(zip(names, slots[n_slot:2 * n_slot]))
    hand_a = dict(zip(hand_names, slots[2 * n_slot:2 * n_slot + n_hand]))
    hand_b = dict(zip(hand_names, slots[2 * n_slot + n_hand:]))
    assert n_chunks % 2 == 0

    for ref in hand_b.values():
        ref[...] = jnp.zeros(ref.shape, ref.dtype)
    alternate(produce(0, slot_a))

    def chunk_pair(i, carry):
        c = 2 * i
        alternate(finish(jnp.maximum(c - 1, 0), hand_b), consume(slot_a, hand_a),
                  produce(c + 1, slot_b))
        alternate(finish(c, hand_a), consume(slot_b, hand_b),
                  produce(jnp.minimum(c + 2, n_chunks - 1), slot_a))
        return carry

    lax.fori_loop(0, n_chunks // 2, chunk_pair, 0)
    alternate(finish(n_chunks - 1, hand_b))


def _wkv_slot_shapes():
    g = _N_PAIR
    return [
        pltpu.VMEM((g, 2 * BLK, 2 * BLK), BF16),
        pltpu.VMEM((g, 2 * BLK, 2 * LANES), BF16),
        pltpu.VMEM((g, BLK, 2 * BLK), BF16),
        pltpu.VMEM((g, BLK, 2 * BLK), BF16),
        pltpu.VMEM((g, BLK, LANES), F32),
        pltpu.VMEM((g, 2 * BLK, LANES), BF16),
        pltpu.VMEM((g, 2 * BLK, LANES), BF16),
        pltpu.VMEM((g, 2 * BLK, LANES), BF16),
        pltpu.VMEM((g, 8, LANES), F32),
        pltpu.VMEM((g, BLK, LANES), F32),
        pltpu.VMEM((g, BLK, LANES), F32),
    ]


def _wkv_hand_shapes():
    g = _N_PAIR
    return [
        pltpu.VMEM((g, 2 * BLK, LANES), BF16),
        pltpu.VMEM((g, BLK, LANES), F32),
        pltpu.VMEM((g, BLK, LANES), F32),
        pltpu.VMEM((g, BLK, LANES), F32),
        pltpu.VMEM((g, BLK, LANES), F32),
    ]


def _rwkv(proj, mu, w0, a0, k_k, k_a, r_k, lnx_g, lnx_b, wwa, g2_bf16):
    b, s, _ = proj.shape
    np_ = _N_PAIR
    w = RWKV_WIDTH
    wide = lambda off: pl.BlockSpec((1, s, w), lambda i: (i, 0, off))
    lane = lambda off: pl.BlockSpec((1, s, LANES), lambda i: (i, 0, off))
    vec_w = lambda off: pl.BlockSpec((1, w), lambda i: (0, off))
    vec_l = lambda off: pl.BlockSpec((1, LANES), lambda i: (0, off))
    full = lambda shape: pl.BlockSpec(shape, lambda i: (0,) * len(shape))
    return pl.pallas_call(
        _wkv_kernel,
        grid=(b,),
        in_specs=[
            wide(0), wide(1), wide(2),
            lane(3 * np_), lane(3 * np_ + 1),
            vec_w(0), vec_w(1), vec_w(2), vec_l(3 * np_), vec_l(3 * np_ + 1),
            vec_w(0), vec_w(0), vec_w(0), vec_w(0), vec_w(0), vec_w(0), vec_w(0),
            full((LANES, np_ * 2 * LANES)),
            full((GATE_LORA, w)),
        ],
        out_specs=pl.BlockSpec((1, s, w), lambda i: (i, 0, 0)),
        out_shape=jax.ShapeDtypeStruct((b, s, RWKV_WIDTH), F32),
        scratch_shapes=([pltpu.VMEM((np_, LANES, LANES), F32)] + 2 * _wkv_slot_shapes()
                        + 2 * _wkv_hand_shapes()),
        compiler_params=pltpu.CompilerParams(
            dimension_semantics=("parallel",), vmem_limit_bytes=VMEM_LIMIT),
        name="rwkv7",
    )(proj, proj, proj, proj, proj, mu, mu, mu, mu, mu,
      w0, a0, k_k, k_a, r_k, lnx_g, lnx_b, wwa, g2_bf16)


def _out_ffn_kernel(x_ref, attn_ref, rwkv_ref, woa_ref, wor_ref, g2_ref,
                    wg_ref, wu_ref, wd_ref, o_ref):
    x1 = (x_ref[...] + _dot(attn_ref[...].astype(BF16), woa_ref[...])
          + _dot(rwkv_ref[...].astype(BF16), wor_ref[...]))
    ms = jnp.mean(x1 * x1, axis=-1, keepdims=True)
    xn = (x1 * lax.rsqrt(ms + NORM_EPS) * g2_ref[...]).astype(BF16)
    gate = _dot(xn, wg_ref[...])
    up = _dot(xn, wu_ref[...])
    hidden = (gate * jax.nn.sigmoid(gate) * up).astype(BF16)
    o_ref[...] = x1 + _dot(hidden, wd_ref[...])


def _out_ffn(x2d, attn2d, rwkv2d, woa, wor, g2, wg, wu, wd, tm):
    t, d = x2d.shape
    f = wg.shape[1]
    tok = lambda w: pl.BlockSpec((tm, w), lambda i: (i, 0))
    res = lambda shape: pl.BlockSpec(shape, lambda i: (0, 0), pipeline_mode=pl.Buffered(1))
    return pl.pallas_call(
        _out_ffn_kernel,
        grid=(t // tm,),
        in_specs=[
            tok(d), tok(ATTN_WIDTH), tok(RWKV_WIDTH),
            res((ATTN_WIDTH, d)), res((RWKV_WIDTH, d)),
            pl.BlockSpec((1, d), lambda i: (0, 0)),
            res((d, f)), res((d, f)), res((f, d)),
        ],
        out_specs=tok(d),
        out_shape=jax.ShapeDtypeStruct((t, d), F32),
        compiler_params=pltpu.CompilerParams(
            dimension_semantics=("parallel",), vmem_limit_bytes=VMEM_LIMIT),
        name="out_ffn",
    )(x2d, attn2d, rwkv2d, woa, wor, g2, wg, wu, wd)


def _layer(x, norm1_g, w_in, q_norm_g, k_norm_g, attn_out_g, rwkv_mu, w0, w2, a0, a2, g2,
           k_k, k_a, r_k, lnx_g, lnx_b, w_out, norm2_g, w_gate, w_up, w_down):
    b, s, d = x.shape
    assert s % (BLK * max(dil for _, dil in DILATED_PATTERNS)) == 0
    assert all(window // dil == BLK for window, dil in DILATED_PATTERNS)
    t = b * s
    assert t % IN_PROJ_ROWS == 0 and t % FFN_ROWS == 0
    row = lambda v: v.reshape(1, -1).astype(F32)

    x2d = x.reshape(t, d)
    per_head = lambda v: jnp.tile(v.reshape(1, HEAD_DIM), (1, ATTN_HEADS)).astype(F32)
    qk_gain = jnp.concatenate([per_head(q_norm_g) * (HEAD_DIM ** -0.5 * LOG2_E),
                               per_head(k_norm_g)], axis=1)
    proj, qkv = _in_proj(x, row(norm1_g), w_in.astype(BF16), qk_gain, IN_PROJ_ROWS)
    proj = proj.reshape(b, s, -1)

    slopes = jnp.exp2(-8.0 * jnp.arange(1, ATTN_HEADS + 1, dtype=F32) / ATTN_HEADS)
    attn = _attention(qkv.reshape(b, s, -1), slopes, row(attn_out_g))

    w2p = w2.reshape(DECAY_LORA, _N_PAIR, LANES).transpose(1, 0, 2)
    a2p = a2.reshape(ICLR_LORA, _N_PAIR, LANES).transpose(1, 0, 2)
    zero = jnp.zeros_like(w2p)
    wwa = jnp.concatenate([jnp.concatenate([w2p, zero], axis=2),
                           jnp.concatenate([jnp.zeros_like(a2p), a2p], axis=2)], axis=1)
    wwa = wwa.transpose(1, 0, 2).reshape(LANES, _N_PAIR * 2 * LANES).astype(BF16)
    rwkv = _rwkv(proj, row(rwkv_mu), row(w0), row(a0), row(k_k), row(k_a), row(r_k),
                 row(lnx_g), row(lnx_b), wwa, g2.astype(BF16))

    out = _out_ffn(x2d, attn.reshape(t, ATTN_WIDTH), rwkv.reshape(t, RWKV_WIDTH),
                   w_out[:ATTN_WIDTH].astype(BF16), w_out[ATTN_WIDTH:].astype(BF16),
                   row(norm2_g), w_gate.astype(BF16), w_up.astype(BF16), w_down.astype(BF16),
                   FFN_ROWS)
    return out.reshape(b, s, d)


def kernel(x, norm1_g, w_in, q_norm_g, k_norm_g, attn_out_g, rwkv_mu, w0, w2, a0, a2, g2,
           k_k, k_a, r_k, lnx_g, lnx_b, w_out, norm2_g, w_gate, w_up, w_down):
    h = x
    for layer in range(norm1_g.shape[0]):
        h = _layer(h, norm1_g[layer], w_in[layer], q_norm_g[layer], k_norm_g[layer],
                   attn_out_g[layer], rwkv_mu[layer], w0[layer], w2[layer], a0[layer],
                   a2[layer], g2[layer], k_k[layer], k_a[layer], r_k[layer], lnx_g[layer],
                   lnx_b[layer], w_out[layer], norm2_g[layer], w_gate[layer], w_up[layer],
                   w_down[layer])
    return h
```
